```python
import math
import jax
import jax.numpy as jnp
from jax import lax
import numpy as np


D_MODEL = 4096
BATCH = 2
SEQ = 8192
DEPTH = 2

GRID_W = 64
CTX_LEN = 256
NORM_EPS = 1e-6
NA_HEADS = 16
NA_HEAD_DIM = 128
NA_WIN_R = 8
NA_WIN_C = 16
S5_WIDTH = 2048
S5_GROUP = 16
S5_GROUPS = S5_WIDTH // S5_GROUP
S5_STATE = 64
S5_DT_MIN = 1e-3
S5_DT_MAX = 1e-1
AB_IN = 3 * NA_HEADS * NA_HEAD_DIM + S5_WIDTH
AB_WIDTH = NA_HEADS * NA_HEAD_DIM + S5_WIDTH
HG_HEADS = 32
HG_KEY_DIM = 128
HG_VAL_DIM = 128
HG_CHUNK = 64
HG_IN = HG_HEADS * (3 * HG_KEY_DIM + 2 * HG_VAL_DIM)
HG_WIDTH = HG_HEADS * HG_VAL_DIM
ROPE_THETA = 10000.0
N_EXPERTS = 16
EXPERT_FF = 1024
EC_CAPACITY_FACTOR = 2
N_AB_LAYERS = (DEPTH + 1) // 2
N_C_LAYERS = DEPTH // 2

kernel_name = 'hybrid_natten_s5_hgrn2_ec_moe_dit'


def rmsnorm(x, g):
    xf = x.astype(jnp.float32)
    y = xf * lax.rsqrt(jnp.mean(xf * xf, axis=-1, keepdims=True) + NORM_EPS)
    return (y * g.astype(jnp.float32)).astype(x.dtype)


def modulate(x, g, shift, scale):
    return rmsnorm(x, g) * (1 + scale[:, None, :]) + shift[:, None, :]


def axial_rope(x):
    t = jnp.arange(x.shape[1])
    half = x.shape[-1] // 2
    quarter = half // 2
    inv_freq = ROPE_THETA ** (-jnp.arange(quarter, dtype=jnp.float32) / quarter)

    def rotate(seg, pos):
        ang = pos.astype(jnp.float32)[:, None] * inv_freq[None, :]
        cos = jnp.cos(ang)[None, :, None, :].astype(x.dtype)
        sin = jnp.sin(ang)[None, :, None, :].astype(x.dtype)
        s1, s2 = seg[..., :quarter], seg[..., quarter:]
        return jnp.concatenate([s1 * cos - s2 * sin, s1 * sin + s2 * cos], axis=-1)

    return jnp.concatenate([rotate(x[..., :half], t // GRID_W), rotate(x[..., half:], t % GRID_W)], axis=-1)


def neighbourhood_attention(q, k, v, kc, vc, rpb):
    bsz, nh, seq, dh = q.shape
    rows = seq // GRID_W
    win_r = min(NA_WIN_R, rows)
    n_nb = win_r * NA_WIN_C
    grid = lambda z: z.reshape(bsz, nh, rows, GRID_W, dh)
    qg, kg, vg = grid(q), grid(k), grid(v)
    cols = jnp.arange(GRID_W)
    col_start = jnp.clip(cols - NA_WIN_C // 2, 0, GRID_W - NA_WIN_C)
    col_idx = col_start[:, None] + jnp.arange(NA_WIN_C)[None, :]
    bias_c = rpb[:, :, col_idx - cols[:, None] + NA_WIN_C - 1]

    def row_block(r):
        r0 = jnp.clip(r - win_r // 2, 0, rows - win_r)
        q_r = lax.dynamic_index_in_dim(qg, r, axis=2, keepdims=False)
        k_nb = lax.dynamic_slice_in_dim(kg, r0, win_r, axis=2)[:, :, :, col_idx]
        v_nb = lax.dynamic_slice_in_dim(vg, r0, win_r, axis=2)[:, :, :, col_idx]
        rel_r = r0 + jnp.arange(win_r) - r + NA_WIN_R - 1
        bias = jnp.transpose(bias_c[:, rel_r], (0, 2, 1, 3)).astype(jnp.float32)
        s_nb = jnp.einsum('bhqd,bhrqjd->bhqrj', q_r, k_nb).astype(jnp.float32) + bias[None]
        s_ctx = jnp.einsum('bhqd,bhkd->bhqk', q_r, kc).astype(jnp.float32)
        s = jnp.concatenate([s_nb.reshape(bsz, nh, GRID_W, n_nb), s_ctx], axis=-1)
        p = jax.nn.softmax(s, axis=-1).astype(v.dtype)
        p_nb = p[..., :n_nb].reshape(bsz, nh, GRID_W, win_r, NA_WIN_C)
        return (jnp.einsum('bhqrj,bhrqjd->bhqd', p_nb, v_nb)
                + jnp.einsum('bhqk,bhkd->bhqd', p[..., n_nb:], vc))

    out = lax.map(row_block, jnp.arange(rows))
    return jnp.transpose(out, (1, 2, 0, 3, 4)).reshape(bsz, nh, seq, dh)


def context_attention(qc, kc, vc):
    s = jnp.einsum('bhqd,bhkd->bhqk', qc, kc).astype(jnp.float32)
    p = jax.nn.softmax(s, axis=-1).astype(vc.dtype)
    return jnp.einsum('bhqk,bhkd->bhqd', p, vc)


def diag_linear_scan(bu, lam_bar):
    a = jnp.broadcast_to(lam_bar, (1, bu.shape[1]) + lam_bar.shape)

    def combine(e1, e2):
        a1, b1 = e1
        a2, b2 = e2
        return a1 * a2, a2 * b1 + b2

    return lax.associative_scan(combine, (a, bu), axis=1)[1]


def s5_direction(ug, ucg, lam_bar, b_bar, c_mat, need_ctx):
    xs_c = diag_linear_scan(jnp.einsum('btgi,gpi->btgp', ucg, b_bar), lam_bar)
    bu = jnp.einsum('btgi,gpi->btgp', ug, b_bar)
    bu = bu.at[:, 0].add(lam_bar * xs_c[:, -1])
    xs = diag_linear_scan(bu, lam_bar)
    y = jnp.real(jnp.einsum('btgp,gip->btgi', xs, c_mat))
    y_c = jnp.real(jnp.einsum('btgp,gip->btgi', xs_c, c_mat)) if need_ctx else None
    return y, y_c


def maybe_flip(z, direction):
    return z[:, ::-1] if direction == 1 else z


def s5_mixer(u, uc, lam_re, lam_im, log_dt, b_re, b_im, c_re, c_im, d_skip, glu_w, glu_b, need_ctx):
    f32 = jnp.float32
    lam = lax.complex(lam_re.astype(f32), lam_im.astype(f32))
    lam_bar = jnp.exp(lam * jnp.exp(log_dt.astype(f32))[..., None])
    b_mat = lax.complex(b_re.astype(f32), b_im.astype(f32))
    b_bar = ((lam_bar - 1.0) / lam)[..., None] * b_mat[None]
    c_mat = lax.complex(c_re.astype(f32), c_im.astype(f32))
    grp = lambda z: z.astype(f32).reshape(z.shape[0], z.shape[1], S5_GROUPS, S5_GROUP)
    ug, ucg = grp(u), grp(uc)
    d = d_skip.astype(f32)
    y = d * ug
    y_c = d * ucg if need_ctx else None
    for direction in range(2):
        yd, ycd = s5_direction(maybe_flip(ug, direction).astype(jnp.complex64),
                               maybe_flip(ucg, direction).astype(jnp.complex64),
                               lam_bar[direction], b_bar[direction], c_mat[direction], need_ctx)
        y = y + maybe_flip(yd, direction)
        if need_ctx:
            y_c = y_c + maybe_flip(ycd, direction)

    def glu(z, like):
        z = jax.nn.gelu(z.reshape(z.shape[0], z.shape[1], S5_WIDTH)).astype(like.dtype)
        return z * jax.nn.sigmoid(z @ glu_w + glu_b)

    return glu(y, u), (glu(y_c, uc) if need_ctx else None)


def mixer_na_s5(h, hc, w_in, w_out, rpb, lam_re, lam_im, log_dt, b_re, b_im, c_re, c_im,
                d_skip, glu_w, glu_b, need_ctx):
    na_w = NA_HEADS * NA_HEAD_DIM
    splits = [na_w, 2 * na_w, 3 * na_w]
    heads = lambda z: z.reshape(z.shape[0], z.shape[1], NA_HEADS, NA_HEAD_DIM).transpose(0, 2, 1, 3)
    merge = lambda z: z.transpose(0, 2, 1, 3).reshape(z.shape[0], z.shape[2], na_w)
    scale = NA_HEAD_DIM ** -0.5
    q, k, v, u = jnp.split(h @ w_in, splits, axis=-1)
    qc, kc, vc, uc = jnp.split(hc @ w_in, splits, axis=-1)
    kc, vc = heads(kc), heads(vc)
    o_na = neighbourhood_attention(heads(q) * scale, heads(k), heads(v), kc, vc, rpb)
    y_s5, yc_s5 = s5_mixer(u, uc, lam_re, lam_im, log_dt, b_re, b_im, c_re, c_im,
                           d_skip, glu_w, glu_b, need_ctx)
    y = jnp.concatenate([merge(o_na), y_s5], axis=-1) @ w_out
    if not need_ctx:
        return y, None
    oc_na = context_attention(heads(qc) * scale, kc, vc)
    yc = jnp.concatenate([merge(oc_na), yc_s5], axis=-1) @ w_out
    return y, yc


def gla_chunkwise(q, k, v, log_f, s0):
    bsz, nh, seq, _ = k.shape
    n = seq // HG_CHUNK
    chunk = lambda z: z.reshape(bsz, nh, n, HG_CHUNK, z.shape[-1])
    k, v, log_f = chunk(k), chunk(v), chunk(log_f)
    b = jnp.cumsum(log_f, axis=3)
    b_last = b[:, :, :, -1:, :]
    kd = k * jnp.exp(b_last - b)
    decay = jnp.exp(b_last[:, :, :, 0, :])
    to_front = lambda z: jnp.moveaxis(z, 2, 0)
    if q is None:
        def state_step(s, inp):
            kd_n, v_n, g_n = inp
            return g_n[..., None] * s + jnp.einsum('bhld,bhlv->bhdv', kd_n, v_n), None
        s_fin, _ = lax.scan(state_step, s0, (to_front(kd), to_front(v), to_front(decay)))
        return None, s_fin
    q = chunk(q)
    qe = q * jnp.exp(b)
    ke = k * jnp.exp(-b)
    lower = jnp.tril(jnp.ones((HG_CHUNK, HG_CHUNK), dtype=bool))
    att = jnp.where(lower, jnp.einsum('bhnld,bhnsd->bhnls', qe, ke), 0.0)
    o_intra = jnp.einsum('bhnls,bhnsv->bhnlv', att, v)

    def step(s, inp):
        qe_n, kd_n, v_n, g_n = inp
        o_n = jnp.einsum('bhld,bhdv->bhlv', qe_n, s)
        return g_n[..., None] * s + jnp.einsum('bhld,bhlv->bhdv', kd_n, v_n), o_n

    s_fin, o_inter = lax.scan(step, s0, (to_front(qe), to_front(kd), to_front(v), to_front(decay)))
    o = o_intra + jnp.moveaxis(o_inter, 0, 2)
    return o.reshape(bsz, nh, seq, v.shape[-1]), s_fin


def mixer_hgrn2(h, hc, w_in, w_out, lb, norm_g, need_ctx):
    f32 = jnp.float32
    dq = HG_HEADS * HG_KEY_DIM
    dv_all = HG_HEADS * HG_VAL_DIM
    lbf = lb.astype(f32)[None, :, None, :]
    to_heads = lambda z, d: z.reshape(z.shape[0], z.shape[1], HG_HEADS, d).transpose(0, 2, 1, 3)

    def forget(z):
        f = lbf + (1.0 - lbf) * jax.nn.sigmoid(to_heads(z, HG_KEY_DIM).astype(f32))
        return jnp.log(f), 1.0 - f

    def project(z, rope):
        p = z @ w_in
        q, f_fw, f_bw, i, g = jnp.split(p, [dq, 2 * dq, 3 * dq, 3 * dq + dv_all], axis=-1)
        q = jax.nn.sigmoid(q).reshape(z.shape[0], z.shape[1], HG_HEADS, HG_KEY_DIM)
        if rope:
            q = axial_rope(q)
        q = q.transpose(0, 2, 1, 3).astype(f32)
        return q, forget(f_fw), forget(f_bw), to_heads(i, HG_VAL_DIM).astype(f32), g

    q, (lf_fw, k_fw), (lf_bw, k_bw), v, g = project(h, True)
    qc, (lfc_fw, kc_fw), (lfc_bw, kc_bw), vc, gc = project(hc, False)
    flip = lambda z: z[:, :, ::-1]
    s0 = jnp.zeros((hc.shape[0], HG_HEADS, HG_KEY_DIM, HG_VAL_DIM), f32)
    oc_fw, sc_fw = gla_chunkwise(qc if need_ctx else None, kc_fw, vc, lfc_fw, s0)
    oc_bw, sc_bw = gla_chunkwise(flip(qc) if need_ctx else None, flip(kc_bw), flip(vc), flip(lfc_bw), s0)
    o_fw, _ = gla_chunkwise(q, k_fw, v, lf_fw, sc_fw)
    o_bw, _ = gla_chunkwise(flip(q), flip(k_bw), flip(v), flip(lf_bw), sc_bw)

    def readout(o, gate):
        o = rmsnorm(o.transpose(0, 2, 1, 3), norm_g.reshape(HG_HEADS, HG_VAL_DIM))
        o = o.reshape(o.shape[0], o.shape[1], dv_all).astype(gate.dtype) * jax.nn.silu(gate)
        return o @ w_out

    y = readout(o_fw + flip(o_bw), g)
    y_c = readout(oc_fw + flip(oc_bw), gc) if need_ctx else None
    return y, y_c


def expert_choice_moe(h, router, w1, w3, w2):
    bsz, seq, d = h.shape
    cap = EC_CAPACITY_FACTOR * seq // N_EXPERTS
    aff = jax.nn.softmax(jnp.einsum('btd,de->bte', h, router).astype(jnp.float32), axis=-1)
    gate, idx = lax.top_k(jnp.transpose(aff, (0, 2, 1)), cap)
    xin = jax.vmap(lambda hb, ib: hb[ib])(h, idx)
    hid = jax.nn.silu(jnp.einsum('becd,edf->becf', xin, w1)) * jnp.einsum('becd,edf->becf', xin, w3)
    y = jnp.einsum('becf,efd->becd', hid, w2) * gate[..., None].astype(h.dtype)
    return jax.vmap(lambda yb, ib: jnp.zeros((seq, d), yb.dtype).at[ib].add(yb))(y, idx)


def setup_inputs(seed: int = 0) -> dict:
    key = jax.random.key(seed)
    keys = iter(jax.random.split(key, 32))

    def normal(shape, std):
        return jax.random.normal(next(keys), shape, jnp.float32) * std

    D = D_MODEL
    G, P, Gi = S5_GROUPS, S5_STATE, S5_GROUP
    return {
        'x': normal((BATCH, SEQ, D), 1.0),
        'c': normal((BATCH, D), 1.0),
        'ctx': normal((BATCH, CTX_LEN, D), 1.0),
        'c_ctx': normal((D,), 1.0),
        'ada_w': normal((DEPTH, D, 6 * D), 0.5 * D ** -0.5),
        'ada_b': normal((DEPTH, 6 * D), 0.01),
        'norm_g': 1.0 + normal((DEPTH, 4, D), 0.05),
        'ab_w_in': normal((N_AB_LAYERS, D, AB_IN), D ** -0.5),
        'ab_w_out': normal((N_AB_LAYERS, AB_WIDTH, D), AB_WIDTH ** -0.5),
        'na_rpb': normal((N_AB_LAYERS, NA_HEADS, 2 * NA_WIN_R - 1, 2 * NA_WIN_C - 1), 0.1),
        's5_lam_re': -0.5 + normal((N_AB_LAYERS, 2, G, P), 0.01),
        's5_lam_im': math.pi * jnp.arange(P, dtype=jnp.float32) + normal((N_AB_LAYERS, 2, G, P), 0.01),
        's5_log_dt': jax.random.uniform(next(keys), (N_AB_LAYERS, 2, G), jnp.float32,
                                        math.log(S5_DT_MIN), math.log(S5_DT_MAX)),
        's5_b_re': normal((N_AB_LAYERS, G, P, Gi), (2 * Gi) ** -0.5),
        's5_b_im': normal((N_AB_LAYERS, G, P, Gi), (2 * Gi) ** -0.5),
        's5_c_re': normal((N_AB_LAYERS, 2, G, Gi, P), P ** -0.5),
        's5_c_im': normal((N_AB_LAYERS, 2, G, Gi, P), P ** -0.5),
        's5_d': normal((N_AB_LAYERS, G, Gi), 1.0),
        's5_glu_w': normal((N_AB_LAYERS, S5_WIDTH, S5_WIDTH), S5_WIDTH ** -0.5),
        's5_glu_b': normal((N_AB_LAYERS, S5_WIDTH), 0.01),
        'hg_w_in': normal((N_C_LAYERS, D, HG_IN), D ** -0.5),
        'hg_w_out': normal((N_C_LAYERS, HG_WIDTH, D), HG_WIDTH ** -0.5),
        'hg_lb_logits': normal((N_C_LAYERS + 1, HG_HEADS, HG_KEY_DIM), 0.1),
        'hg_norm_g': 1.0 + normal((N_C_LAYERS, HG_WIDTH), 0.05),
        'moe_router': normal((DEPTH, D, N_EXPERTS), D ** -0.5),
        'moe_w1': normal((DEPTH, N_EXPERTS, D, EXPERT_FF), D ** -0.5),
        'moe_w3': normal((DEPTH, N_EXPERTS, D, EXPERT_FF), D ** -0.5),
        'moe_w2': normal((DEPTH, N_EXPERTS, EXPERT_FF, D), EXPERT_FF ** -0.5),
    }


def reference(x, c, ctx, c_ctx, ada_w, ada_b, norm_g, ab_w_in, ab_w_out, na_rpb,
              s5_lam_re, s5_lam_im, s5_log_dt, s5_b_re, s5_b_im, s5_c_re, s5_c_im, s5_d,
              s5_glu_w, s5_glu_b, hg_w_in, hg_w_out, hg_lb_logits, hg_norm_g,
              moe_router, moe_w1, moe_w3, moe_w2):
    xc = ctx
    cond = jax.nn.silu(c)
    cond_ctx = jax.nn.silu(c_ctx)[None]
    hg_lb = jnp.cumsum(jax.nn.softmax(hg_lb_logits.astype(jnp.float32), axis=0), axis=0)
    for layer in range(DEPTH):
        need_ctx = layer < DEPTH - 1
        sh1, sc1, g1, sh2, sc2, g2 = jnp.split(cond @ ada_w[layer] + ada_b[layer], 6, axis=-1)
        csh1, csc1, cg1, csh2, csc2, cg2 = jnp.split(cond_ctx @ ada_w[layer] + ada_b[layer], 6, axis=-1)
        ng = norm_g[layer]
        h = modulate(x, ng[0], sh1, sc1)
        hc = modulate(xc, ng[0], csh1, csc1)
        j = layer // 2
        if layer % 2 == 0:
            y, yc = mixer_na_s5(h, hc, ab_w_in[j], ab_w_out[j], na_rpb[j], s5_lam_re[j], s5_lam_im[j],
                                s5_log_dt[j], s5_b_re[j], s5_b_im[j], s5_c_re[j], s5_c_im[j], s5_d[j],
                                s5_glu_w[j], s5_glu_b[j], need_ctx)
        else:
            y, yc = mixer_hgrn2(h, hc, hg_w_in[j], hg_w_out[j], hg_lb[j], hg_norm_g[j], need_ctx)
        x = x + g1[:, None, :] * rmsnorm(y, ng[1])
        h2 = modulate(x, ng[2], sh2, sc2)
        x = x + g2[:, None, :] * rmsnorm(expert_choice_moe(h2, moe_router[layer], moe_w1[layer],
                                                            moe_w3[layer], moe_w2[layer]), ng[3])
        if need_ctx:
            xc = xc + cg1[:, None, :] * rmsnorm(yc, ng[1])
            hc2 = modulate(xc, ng[2], csh2, csc2)
            xc = xc + cg2[:, None, :] * rmsnorm(expert_choice_moe(hc2, moe_router[layer], moe_w1[layer],
                                                                  moe_w3[layer], moe_w2[layer]), ng[3])
    return x
```

```python
import functools
import math

import numpy as np
import jax
import jax.numpy as jnp
from jax import lax
from jax.experimental import pallas as pl
from jax.experimental.pallas import tpu as pltpu

F32 = jnp.float32
BF16 = jnp.bfloat16
HIGHEST = lax.Precision.HIGHEST

D_MODEL = 4096
GRID_W = 64
NORM_EPS = 1e-6
NA_HEADS = 16
NA_HEAD_DIM = 128
NA_WIN_R = 8
NA_WIN_C = 16
NA_W = NA_HEADS * NA_HEAD_DIM
S5_WIDTH = 2048
S5_GROUP = 16
S5_GROUPS = S5_WIDTH // S5_GROUP
S5_STATE = 64
S5_CHUNK = 16
HG_HEADS = 32
HG_DIM = 128
HG_CHUNK = 64
HG_Q = HG_HEADS * HG_DIM
ROPE_THETA = 10000.0
N_EXPERTS = 16
EXPERT_FF = 1024
EC_CAPACITY_FACTOR = 2

VMEM_LIMIT_BYTES = 56 * 1024 * 1024
NA_QROWS = 8
NA_KROWS = 16
NEG_BIG = -1e30

_NT = (((1,), (1,)), ((), ()))


def _params(*sem):
    return pltpu.CompilerParams(dimension_semantics=sem, vmem_limit_bytes=VMEM_LIMIT_BYTES)


def _mm_kernel(a_ref, b_ref, o_ref):
    a = a_ref[...].astype(BF16)
    b = b_ref[...].astype(BF16)
    o_ref[...] = jnp.dot(a, b, preferred_element_type=F32).astype(o_ref.dtype)


def _matmul(a, b, out_dtype, tm, tn, name):
    m, k = a.shape
    _, n = b.shape
    tm = min(tm, m)
    tn = min(tn, n)
    assert m % tm == 0 and n % tn == 0, (m, n, tm, tn)
    a_bytes = m * k * a.dtype.itemsize
    b_bytes = k * n * b.dtype.itemsize
    weight_outer = a_bytes * (n // tn) + b_bytes < b_bytes * (m // tm) + a_bytes
    if weight_outer:
        grid = (n // tn, m // tm)
        a_map = lambda j, i: (i, 0)
        b_map = lambda j, i: (0, j)
        o_map = lambda j, i: (i, j)
    else:
        grid = (m // tm, n // tn)
        a_map = lambda i, j: (i, 0)
        b_map = lambda i, j: (0, j)
        o_map = lambda i, j: (i, j)
    return pl.pallas_call(
        _mm_kernel,
        grid=grid,
        in_specs=[pl.BlockSpec((tm, k), a_map), pl.BlockSpec((k, tn), b_map)],
        out_specs=pl.BlockSpec((tm, tn), o_map),
        out_shape=jax.ShapeDtypeStruct((m, n), out_dtype),
        compiler_params=_params("arbitrary", "arbitrary"),
        name=name,
    )(a, b)


def _modulate_kernel(x_ref, g_ref, sh_ref, sc_ref, o_ref):
    x = x_ref[0]
    ms = jnp.mean(x * x, axis=-1, keepdims=True)
    y = x * lax.rsqrt(ms + NORM_EPS) * g_ref[...]
    o_ref[0] = (y * (1.0 + sc_ref[0]) + sh_ref[0]).astype(o_ref.dtype)


def _modulate(x, g, shift, scale, tt=256):
    b, t, d = x.shape
    tt = min(tt, t)
    row = lambda bi, ti: (bi, 0, 0)
    return pl.pallas_call(
        _modulate_kernel,
        grid=(b, t // tt),
        in_specs=[pl.BlockSpec((1, tt, d), lambda bi, ti: (bi, ti, 0)),
                  pl.BlockSpec((1, d), lambda bi, ti: (0, 0)),
                  pl.BlockSpec((1, 1, d), row),
                  pl.BlockSpec((1, 1, d), row)],
        out_specs=pl.BlockSpec((1, tt, d), lambda bi, ti: (bi, ti, 0)),
        out_shape=jax.ShapeDtypeStruct((b, t, d), BF16),
        compiler_params=_params("arbitrary", "arbitrary"),
        name="modulate",
    )(x, g.reshape(1, d), shift.reshape(b, 1, d), scale.reshape(b, 1, d))


def _residual_kernel(x_ref, y_ref, g_ref, gate_ref, o_ref):
    y = y_ref[0].astype(F32)
    ms = jnp.mean(y * y, axis=-1, keepdims=True)
    o_ref[0] = x_ref[0] + gate_ref[0] * (y * lax.rsqrt(ms + NORM_EPS) * g_ref[...])


def _gated_residual(x, y, g, gate, tt=256):
    b, t, d = x.shape
    tt = min(tt, t)
    blk = pl.BlockSpec((1, tt, d), lambda bi, ti: (bi, ti, 0))
    return pl.pallas_call(
        _residual_kernel,
        grid=(b, t // tt),
        in_specs=[blk, blk,
                  pl.BlockSpec((1, d), lambda bi, ti: (0, 0)),
                  pl.BlockSpec((1, 1, d), lambda bi, ti: (bi, 0, 0))],
        out_specs=blk,
        out_shape=jax.ShapeDtypeStruct((b, t, d), F32),
        compiler_params=_params("arbitrary", "arbitrary"),
        name="gated_residual",
    )(x, y, g.reshape(1, d), gate.reshape(b, 1, d))


def _na_bias_table(rpb, rows):
    nblk = rows // NA_QROWS
    qc = np.arange(GRID_W)
    cs = np.clip(qc - NA_WIN_C // 2, 0, GRID_W - NA_WIN_C)
    kc = np.arange(GRID_W)
    in_c = (kc[None, :] >= cs[:, None]) & (kc[None, :] < cs[:, None] + NA_WIN_C)
    rel_c = np.clip(kc[None, :] - qc[:, None] + NA_WIN_C - 1, 0, 2 * NA_WIN_C - 2)
    tables = []
    for j in (0, 1, nblk - 1):
        qr = NA_QROWS * j + np.arange(NA_QROWS)
        ws = int(np.clip(NA_QROWS * j - NA_WIN_R // 2, 0, rows - NA_KROWS))
        kr = ws + np.arange(NA_KROWS)
        r0 = np.clip(qr - NA_WIN_R // 2, 0, rows - NA_WIN_R)
        in_r = (kr[None, :] >= r0[:, None]) & (kr[None, :] < r0[:, None] + NA_WIN_R)
        rel_r = np.clip(kr[None, :] - qr[:, None] + NA_WIN_R - 1, 0, 2 * NA_WIN_R - 2)
        valid = in_r[:, None, :, None] & in_c[None, :, None, :]
        rr = np.broadcast_to(rel_r[:, None, :, None], valid.shape)
        cc = np.broadcast_to(rel_c[None, :, None, :], valid.shape)
        bias = rpb.astype(F32)[:, rr, cc]
        bias = jnp.where(valid[None], bias, NEG_BIG)
        tables.append(bias.reshape(rpb.shape[0], NA_QROWS * GRID_W, NA_KROWS * GRID_W))
    return jnp.stack(tables)


def _na_kernel(q_ref, k_ref, v_ref, kc_ref, vc_ref, bias_ref, o_ref, *, rows):
    j = pl.program_id(2)
    ws = jnp.clip(NA_QROWS * j - NA_WIN_R // 2, 0, rows - NA_KROWS) * GRID_W
    ws = pl.multiple_of(ws, 256)
    nk = NA_KROWS * GRID_W
    scale = NA_HEAD_DIM ** -0.5
    q = q_ref[0]
    kw = k_ref[0, pl.ds(ws, nk), :]
    vw = v_ref[0, pl.ds(ws, nk), :]
    s_nb = lax.dot_general(q, kw, _NT, preferred_element_type=F32) * scale + bias_ref[0, 0]
    s_c = lax.dot_general(q, kc_ref[0], _NT, preferred_element_type=F32) * scale
    m = jnp.maximum(jnp.max(s_nb, axis=-1, keepdims=True), jnp.max(s_c, axis=-1, keepdims=True))
    p_nb = jnp.exp(s_nb - m)
    p_c = jnp.exp(s_c - m)
    denom = jnp.sum(p_nb, axis=-1, keepdims=True) + jnp.sum(p_c, axis=-1, keepdims=True)
    o = (jnp.dot(p_nb.astype(BF16), vw, preferred_element_type=F32)
         + jnp.dot(p_c.astype(BF16), vc_ref[0], preferred_element_type=F32))
    o_ref[0] = (o / denom).astype(o_ref.dtype)


def _neighbourhood_attention(qkv, qkv_c, rpb):
    b, t, _ = qkv.shape
    tc = qkv_c.shape[1]
    rows = t // GRID_W
    nblk = rows // NA_QROWS
    tq = NA_QROWS * GRID_W
    bias = _na_bias_table(rpb, rows)
    pat = lambda j: jnp.where(j == 0, 0, jnp.where(j == nblk - 1, 2, 1))
    return pl.pallas_call(
        functools.partial(_na_kernel, rows=rows),
        grid=(b, NA_HEADS, nblk),
        in_specs=[
            pl.BlockSpec((1, tq, NA_HEAD_DIM), lambda bi, h, j: (bi, j, h)),
            pl.BlockSpec((1, t, NA_HEAD_DIM), lambda bi, h, j: (bi, 0, NA_HEADS + h)),
            pl.BlockSpec((1, t, NA_HEAD_DIM), lambda bi, h, j: (bi, 0, 2 * NA_HEADS + h)),
            pl.BlockSpec((1, tc, NA_HEAD_DIM), lambda bi, h, j: (bi, 0, NA_HEADS + h)),
            pl.BlockSpec((1, tc, NA_HEAD_DIM), lambda bi, h, j: (bi, 0, 2 * NA_HEADS + h)),
            pl.BlockSpec((1, 1, tq, NA_KROWS * GRID_W), lambda bi, h, j: (pat(j), h, 0, 0)),
        ],
        out_specs=pl.BlockSpec((1, tq, NA_HEAD_DIM), lambda bi, h, j: (bi, j, h)),
        out_shape=jax.ShapeDtypeStruct((b, t, NA_W), BF16),
        compiler_params=_params("arbitrary", "arbitrary", "arbitrary"),
        name="neighbourhood_attention",
    )(qkv, qkv, qkv, qkv_c, qkv_c, bias)


def _ctx_attn_kernel(q_ref, k_ref, v_ref, o_ref):
    scale = NA_HEAD_DIM ** -0.5
    s = lax.dot_general(q_ref[0], k_ref[0], _NT, preferred_element_type=F32) * scale
    p = jnp.exp(s - jnp.max(s, axis=-1, keepdims=True))
    denom = jnp.sum(p, axis=-1, keepdims=True)
    o = jnp.dot(p.astype(BF16), v_ref[0], preferred_element_type=F32)
    o_ref[0] = (o / denom).astype(o_ref.dtype)


def _context_attention(qkv_c):
    b, tc, _ = qkv_c.shape
    blk = lambda off: pl.BlockSpec((1, tc, NA_HEAD_DIM), lambda bi, h: (bi, 0, off + h))
    return pl.pallas_call(
        _ctx_attn_kernel,
        grid=(b, NA_HEADS),
        in_specs=[blk(0), blk(NA_HEADS), blk(2 * NA_HEADS)],
        out_specs=blk(0),
        out_shape=jax.ShapeDtypeStruct((b, tc, NA_W), BF16),
        compiler_params=_params("arbitrary", "arbitrary"),
        name="context_attention",
    )(qkv_c, qkv_c, qkv_c)


def _s5_tables(lam_re, lam_im, log_dt, b_re, b_im, c_re, c_im, d_skip):
    ll, gi, p, g = S5_CHUNK, S5_GROUP, S5_STATE, S5_GROUPS
    f32 = F32
    lam = lax.complex(lam_re.astype(f32), lam_im.astype(f32))
    dt = jnp.exp(log_dt.astype(f32))[..., None]
    lam_dt = lam * dt
    lam_bar = jnp.exp(lam_dt)
    b_bar = ((lam_bar - 1.0) / lam)[..., None] * lax.complex(b_re.astype(f32), b_im.astype(f32))[None]
    c_mat = lax.complex(c_re.astype(f32), c_im.astype(f32))
    taus = jnp.arange(ll + 1, dtype=f32)
    pw = jnp.exp(lam_dt[None] * taus[:, None, None, None])

    def cmul_sum(a, b, spec):
        ar, ai, br, bi = jnp.real(a), jnp.imag(a), jnp.real(b), jnp.imag(b)
        e = lambda x, y: jnp.einsum(spec, x, y, precision=HIGHEST)
        return e(ar, br) - e(ai, bi), e(ar, bi) + e(ai, br)

    cp = c_mat[:, None] * jnp.moveaxis(pw[:ll], 0, 1)[:, :, :, None, :]
    k_re, _ = cmul_sum(cp, b_bar, 'dtgip,dgpj->dtgij')
    idx = np.arange(ll)
    tau_f = idx[None, :] - idx[:, None]
    kf = jnp.where((tau_f >= 0)[:, :, None, None, None], k_re[0][np.clip(tau_f, 0, ll - 1)], 0.0)
    kb = jnp.where((tau_f <= 0)[:, :, None, None, None], k_re[1][np.clip(-tau_f, 0, ll - 1)], 0.0)
    skip = (jnp.eye(ll, dtype=f32)[:, :, None, None, None]
            * (d_skip.astype(f32)[:, :, None] * jnp.eye(gi, dtype=f32))[None, None])
    t_sum = kf + kb + skip
    t_sum = jnp.transpose(t_sum, (2, 0, 4, 1, 3)).reshape(g, ll * gi, ll * gi)

    def pad_state(x):
        return jnp.pad(x, [(0, 0)] * (x.ndim - 1) + [(0, 128 - p)])

    wf = pw[ll - 1 - idx, 0][:, :, :, None] * b_bar[0][None]
    wb = pw[idx, 1][:, :, :, None] * b_bar[1][None]
    to_rows = lambda w: jnp.transpose(w, (1, 0, 3, 2)).reshape(g, ll * gi, p)
    w_blocks = [pad_state(to_rows(f(w))) for w in (wf, wb) for f in (jnp.real, jnp.imag)]
    m_in = jnp.concatenate([t_sum] + w_blocks, axis=-1)

    zf = c_mat[0][None] * pw[idx + 1, 0][:, :, None, :]
    zb = c_mat[1][None] * pw[ll - idx, 1][:, :, None, :]
    to_cols = lambda z: pad_state(jnp.transpose(z, (1, 0, 2, 3)).reshape(g, ll * gi, p))
    z_blocks = [jnp.swapaxes(to_cols(f(z)), 1, 2)
                for z in (zf, zb) for f in (jnp.real, lambda v: -jnp.imag(v))]
    z_out = jnp.concatenate(z_blocks, axis=1)

    lam_l = jnp.stack([pad_state(f(pw[ll, d])) for d in (0, 1) for f in (jnp.real, jnp.imag)], axis=1)
    return m_in, z_out, lam_l


def _s5_kernel(u_ref, m_ref, z_ref, lam_ref, y_ref, r_ref, s_ref, *, nbatch, nchunk, nctx):
    width = S5_CHUNK * S5_GROUP
    r_ref[...] = jnp.dot(u_ref[0], m_ref[0], precision=HIGHEST, preferred_element_type=F32)
    lam = lam_ref[0]
    lfr, lfi, lbr, lbi = lam[0:1], lam[1:2], lam[2:3], lam[3:4]
    sl = lambda i: slice(128 * i, 128 * (i + 1))
    vsl = lambda i: slice(width + 128 * i, width + 128 * (i + 1))

    sub = 8
    ngrp, nctx_g = nchunk // sub, nctx // sub

    def body(kg, carry):
        gb = jnp.where(kg < nctx_g, nctx_g - 1 - kg, ngrp - 1 - (kg - nctx_g))
        out = []
        for bi in range(nbatch):
            fr, fi, br, bim = carry[bi]
            rf = pl.ds(pl.multiple_of(bi * nchunk + kg * sub, sub), sub)
            rb = pl.ds(pl.multiple_of(bi * nchunk + gb * sub, sub), sub)
            vfr, vfi = r_ref[rf, vsl(0)], r_ref[rf, vsl(1)]
            vbr, vbi = r_ref[rb, vsl(2)], r_ref[rb, vsl(3)]
            sfr, sfi, sbr, sbi = [], [], [None] * sub, [None] * sub
            for i in range(sub):
                sfr.append(fr)
                sfi.append(fi)
                fr, fi = (lfr * fr - lfi * fi + vfr[i:i + 1], lfr * fi + lfi * fr + vfi[i:i + 1])
                j = sub - 1 - i
                sbr[j], sbi[j] = br, bim
                br, bim = (lbr * br - lbi * bim + vbr[j:j + 1], lbr * bim + lbi * br + vbi[j:j + 1])
            s_ref[rf, sl(0)] = jnp.concatenate(sfr, axis=0)
            s_ref[rf, sl(1)] = jnp.concatenate(sfi, axis=0)
            s_ref[rb, sl(2)] = jnp.concatenate(sbr, axis=0)
            s_ref[rb, sl(3)] = jnp.concatenate(sbi, axis=0)
            out.append((fr, fi, br, bim))
        return tuple(out)

    zero = jnp.zeros((1, 128), F32)
    lax.fori_loop(0, ngrp, body, tuple((zero, zero, zero, zero) for _ in range(nbatch)))
    y_ref[0] = r_ref[:, 0:width] + jnp.dot(s_ref[...], z_ref[0], precision=HIGHEST,
                                           preferred_element_type=F32)


def _s5_scan(u_cat, tables, nctx_tokens):
    m_in, z_out, lam_l = tables
    b, tt, _ = u_cat.shape
    ll, gi, g = S5_CHUNK, S5_GROUP, S5_GROUPS
    n = tt // ll
    width = ll * gi
    ug = u_cat.reshape(b, n, ll, g, gi).transpose(3, 0, 1, 2, 4).reshape(g, b * n, width)
    y = pl.pallas_call(
        functools.partial(_s5_kernel, nbatch=b, nchunk=n, nctx=nctx_tokens // ll),
        grid=(g,),
        in_specs=[pl.BlockSpec((1, b * n, width), lambda i: (i, 0, 0)),
                  pl.BlockSpec((1, width, width + 512), lambda i: (i, 0, 0)),
                  pl.BlockSpec((1, 512, width), lambda i: (i, 0, 0)),
                  pl.BlockSpec((1, 4, 128), lambda i: (i, 0, 0))],
        out_specs=pl.BlockSpec((1, b * n, width), lambda i: (i, 0, 0)),
        out_shape=jax.ShapeDtypeStruct((g, b * n, width), F32),
        scratch_shapes=[pltpu.VMEM((b * n, width + 512), F32), pltpu.VMEM((b * n, 512), F32)],
        compiler_params=_params("arbitrary"),
        name="s5_scan",
    )(ug, m_in, z_out, lam_l)
    return y.reshape(g, b, n, ll, gi).transpose(1, 2, 3, 0, 4).reshape(b, tt, g * gi)


def _glu_kernel(y_ref, w_ref, b_ref, o_ref):
    z = jax.nn.gelu(y_ref[...])
    a = jnp.dot(z.astype(BF16), w_ref[...], preferred_element_type=F32) + b_ref[...]
    o_ref[...] = (z * jax.nn.sigmoid(a)).astype(o_ref.dtype)


def _s5_glu(y, w, bias, tm=256):
    m, n = y.shape
    tm = min(tm, m)
    return pl.pallas_call(
        _glu_kernel,
        grid=(m // tm,),
        in_specs=[pl.BlockSpec((tm, n), lambda i: (i, 0)),
                  pl.BlockSpec((n, n), lambda i: (0, 0)),
                  pl.BlockSpec((1, n), lambda i: (0, 0))],
        out_specs=pl.BlockSpec((tm, n), lambda i: (i, 0)),
        out_shape=jax.ShapeDtypeStruct((m, n), BF16),
        compiler_params=_params("arbitrary"),
        name="s5_glu",
    )(y, w, bias.reshape(1, n))


def _gla_chunk(qz, fz, v, cos, sin, lb, tri, mask, st):
    f = lb + (1.0 - lb) * jax.nn.sigmoid(fz)
    logf = jnp.log(f)
    k = 1.0 - f
    b = jnp.dot(tri, logf, precision=HIGHEST, preferred_element_type=F32)
    tot = jnp.sum(logf, axis=0, keepdims=True)
    kd = k * jnp.exp(tot - b)
    new_st = st * jnp.exp(tot) + jnp.dot(v.T.astype(BF16), kd.astype(BF16), preferred_element_type=F32)
    if qz is None:
        return None, new_st
    sg = jax.nn.sigmoid(qz)
    lane = lax.broadcasted_iota(jnp.int32, sg.shape, 1)
    quarter = HG_DIM // 4
    partner = jnp.where((lane & quarter) == 0, pltpu.roll(sg, HG_DIM - quarter, 1),
                        pltpu.roll(sg, quarter, 1))
    q = sg * cos + partner * sin
    qe = (q * jnp.exp(b)).astype(BF16)
    ke = (k * jnp.exp(-b)).astype(BF16)
    att = lax.dot_general(qe, ke, _NT, preferred_element_type=F32)
    att = jnp.where(mask > 0.0, att, 0.0)
    o = (jnp.dot(att.astype(BF16), v.astype(BF16), preferred_element_type=F32)
         + lax.dot_general(qe, st.astype(BF16), _NT, preferred_element_type=F32))
    return o, new_st


def _gla_state_kernel(f_ref, v_ref, lb_ref, tri_ref, mask_ref, s_ref, *, nchunk):
    d = pl.program_id(2)
    st = jnp.zeros((HG_DIM, HG_DIM), F32)
    for c in range(nchunk):
        def load(ref, c=c):
            cc = jnp.where(d == 0, c, nchunk - 1 - c)
            return ref[0, pl.ds(pl.multiple_of(cc * HG_CHUNK, HG_CHUNK), HG_CHUNK), :]
        _, st = _gla_chunk(None, load(f_ref), load(v_ref), None, None, lb_ref[0],
                           tri_ref[0], mask_ref[0], st)
    s_ref[0, 0, 0] = st


def _gla_kernel(q_ref, f_ref, v_ref, cos_ref, sin_ref, lb_ref, tri_ref, mask_ref, s0_ref,
                o_ref, st_ref, *, nchunk):
    d = pl.program_id(2)

    @pl.when(pl.program_id(3) == 0)
    def _():
        st_ref[...] = s0_ref[0, 0, 0]

    def body(c, carry):
        cc = jnp.where(d == 0, c, nchunk - 1 - c)
        sl = pl.ds(pl.multiple_of(cc * HG_CHUNK, HG_CHUNK), HG_CHUNK)
        o, st = _gla_chunk(q_ref[0, sl, :], f_ref[0, sl, :], v_ref[0, sl, :], cos_ref[sl, :],
                           sin_ref[sl, :], lb_ref[0], tri_ref[0], mask_ref[0], st_ref[...])
        st_ref[...] = st
        o_ref[0, 0, sl, :] = o
        return carry

    lax.fori_loop(0, nchunk, body, 0)


def _rope_tables(t):
    pos = np.arange(t)
    quarter = HG_DIM // 4
    inv_freq = ROPE_THETA ** (-jnp.arange(quarter, dtype=F32) / quarter)

    def tab(p):
        ang = jnp.asarray(p, F32)[:, None] * inv_freq[None, :]
        c, s = jnp.cos(ang), jnp.sin(ang)
        return jnp.concatenate([c, c], axis=-1), jnp.concatenate([-s, s], axis=-1)

    c_r, s_r = tab(pos // GRID_W)
    c_c, s_c = tab(pos % GRID_W)
    return jnp.concatenate([c_r, c_c], axis=-1), jnp.concatenate([s_r, s_c], axis=-1)


def _gla_masks():
    i = np.arange(HG_CHUNK)
    lower = (i[:, None] >= i[None, :]).astype(np.float32)
    return jnp.asarray(np.stack([lower, lower.T]))


def _hgrn2_gla(proj, proj_c, lb, tblock=1024):
    b, t, _ = proj.shape
    tc = proj_c.shape[1]
    tblock = min(tblock, t)
    nt = t // tblock
    tri = _gla_masks()
    lb3 = lb.astype(F32).reshape(HG_HEADS, 1, HG_DIM)
    small = lambda: pl.BlockSpec((1, HG_CHUNK, HG_CHUNK), lambda bi, h, d, *_: (d, 0, 0))
    lb_spec = lambda: pl.BlockSpec((1, 1, HG_DIM), lambda bi, h, d, *_: (h, 0, 0))

    s0 = pl.pallas_call(
        functools.partial(_gla_state_kernel, nchunk=tc // HG_CHUNK),
        grid=(b, HG_HEADS, 2),
        in_specs=[pl.BlockSpec((1, tc, HG_DIM), lambda bi, h, d: (bi, 0, HG_HEADS * (1 + d) + h)),
                  pl.BlockSpec((1, tc, HG_DIM), lambda bi, h, d: (bi, 0, 3 * HG_HEADS + h)),
                  lb_spec(), small(), small()],
        out_specs=pl.BlockSpec((1, 1, 1, HG_DIM, HG_DIM), lambda bi, h, d: (bi, h, d, 0, 0)),
        out_shape=jax.ShapeDtypeStruct((b, HG_HEADS, 2, HG_DIM, HG_DIM), F32),
        compiler_params=_params("arbitrary", "arbitrary", "arbitrary"),
        name="hgrn2_ctx_state",
    )(proj_c, proj_c, lb3, tri, tri)

    cos, sin = _rope_tables(t)
    tb_of = lambda d, ti: jnp.where(d == 0, ti, nt - 1 - ti)
    return pl.pallas_call(
        functools.partial(_gla_kernel, nchunk=tblock // HG_CHUNK),
        grid=(b, HG_HEADS, 2, nt),
        in_specs=[pl.BlockSpec((1, tblock, HG_DIM), lambda bi, h, d, ti: (bi, tb_of(d, ti), h)),
                  pl.BlockSpec((1, tblock, HG_DIM),
                               lambda bi, h, d, ti: (bi, tb_of(d, ti), HG_HEADS * (1 + d) + h)),
                  pl.BlockSpec((1, tblock, HG_DIM),
                               lambda bi, h, d, ti: (bi, tb_of(d, ti), 3 * HG_HEADS + h)),
                  pl.BlockSpec((tblock, HG_DIM), lambda bi, h, d, ti: (tb_of(d, ti), 0)),
                  pl.BlockSpec((tblock, HG_DIM), lambda bi, h, d, ti: (tb_of(d, ti), 0)),
                  lb_spec(), small(), small(),
                  pl.BlockSpec((1, 1, 1, HG_DIM, HG_DIM), lambda bi, h, d, ti: (bi, h, d, 0, 0))],
        out_specs=pl.BlockSpec((1, 1, tblock, HG_DIM), lambda bi, h, d, ti: (d, bi, tb_of(d, ti), h)),
        out_shape=jax.ShapeDtypeStruct((2, b, t, HG_Q), F32),
        scratch_shapes=[pltpu.VMEM((HG_DIM, HG_DIM), F32)],
        compiler_params=_params("arbitrary", "arbitrary", "arbitrary", "arbitrary"),
        name="hgrn2_gla",
    )(proj, proj, proj, cos, sin, lb3, tri, tri, s0)


def _readout_kernel(of_ref, ob_ref, gate_ref, g_ref, o_ref):
    o = of_ref[0, 0] + ob_ref[0, 0]
    ms = jnp.mean(o * o, axis=-1, keepdims=True)
    y = o * lax.rsqrt(ms + NORM_EPS) * g_ref[0]
    gate = gate_ref[0]
    o_ref[0] = (y * (gate * jax.nn.sigmoid(gate))).astype(o_ref.dtype)


def _hgrn2_readout(o2, proj, norm_g, tt=1024):
    _, b, t, _ = o2.shape
    tt = min(tt, t)
    ospec = lambda d: pl.BlockSpec((1, 1, tt, HG_DIM), lambda bi, ti, h: (d, bi, ti, h))
    return pl.pallas_call(
        _readout_kernel,
        grid=(b, t // tt, HG_HEADS),
        in_specs=[ospec(0), ospec(1),
                  pl.BlockSpec((1, tt, HG_DIM), lambda bi, ti, h: (bi, ti, 4 * HG_HEADS + h)),
                  pl.BlockSpec((1, 1, HG_DIM), lambda bi, ti, h: (h, 0, 0))],
        out_specs=pl.BlockSpec((1, tt, HG_DIM), lambda bi, ti, h: (bi, ti, h)),
        out_shape=jax.ShapeDtypeStruct((b, t, HG_Q), BF16),
        compiler_params=_params("arbitrary", "arbitrary", "arbitrary"),
        name="hgrn2_readout",
    )(o2, o2, proj, norm_g.astype(F32).reshape(HG_HEADS, 1, HG_DIM))


def _router_kernel(h_ref, r_ref, o_ref):
    logits = lax.dot_general(r_ref[...], h_ref[0], _NT, preferred_element_type=F32)
    p = jnp.exp(logits - jnp.max(logits, axis=0, keepdims=True))
    o_ref[0] = p / jnp.sum(p, axis=0, keepdims=True)


def _router_affinity(h, router_t, tt=512):
    b, t, d = h.shape
    tt = min(tt, t)
    return pl.pallas_call(
        _router_kernel,
        grid=(b, t // tt),
        in_specs=[pl.BlockSpec((1, tt, d), lambda bi, ti: (bi, ti, 0)),
                  pl.BlockSpec((N_EXPERTS, d), lambda bi, ti: (0, 0))],
        out_specs=pl.BlockSpec((1, N_EXPERTS, tt), lambda bi, ti: (bi, 0, ti)),
        out_shape=jax.ShapeDtypeStruct((b, N_EXPERTS, t), F32),
        compiler_params=_params("arbitrary", "arbitrary"),
        name="moe_router",
    )(h, router_t)


def _expert_up_kernel(x_ref, w1_ref, w3_ref, o_ref):
    x = x_ref[0, 0]
    a = jnp.dot(x, w1_ref[0], preferred_element_type=F32)
    g = jnp.dot(x, w3_ref[0], preferred_element_type=F32)
    o_ref[0, 0] = (a * jax.nn.sigmoid(a) * g).astype(o_ref.dtype)


def _expert_down_kernel(h_ref, w2_ref, gate_ref, o_ref):
    y = jnp.dot(h_ref[0, 0], w2_ref[0], preferred_element_type=F32)
    o_ref[0, 0] = y * gate_ref[0, 0]


def _expert_ffn(xin, gate, w1, w3, w2, tm=512):
    b, e, cap, d = xin.shape
    ff = w1.shape[-1]
    tm = min(tm, cap)
    grid = (e, b, cap // tm)
    hid = pl.pallas_call(
        _expert_up_kernel,
        grid=grid,
        in_specs=[pl.BlockSpec((1, 1, tm, d), lambda ei, bi, mi: (bi, ei, mi, 0)),
                  pl.BlockSpec((1, d, ff), lambda ei, bi, mi: (ei, 0, 0)),
                  pl.BlockSpec((1, d, ff), lambda ei, bi, mi: (ei, 0, 0))],
        out_specs=pl.BlockSpec((1, 1, tm, ff), lambda ei, bi, mi: (bi, ei, mi, 0)),
        out_shape=jax.ShapeDtypeStruct((b, e, cap, ff), BF16),
        compiler_params=_params("arbitrary", "arbitrary", "arbitrary"),
        name="moe_expert_up",
    )(xin, w1, w3)
    return pl.pallas_call(
        _expert_down_kernel,
        grid=grid,
        in_specs=[pl.BlockSpec((1, 1, tm, ff), lambda ei, bi, mi: (bi, ei, mi, 0)),
                  pl.BlockSpec((1, ff, d), lambda ei, bi, mi: (ei, 0, 0)),
                  pl.BlockSpec((1, 1, tm, 1), lambda ei, bi, mi: (bi, ei, mi, 0))],
        out_specs=pl.BlockSpec((1, 1, tm, d), lambda ei, bi, mi: (bi, ei, mi, 0)),
        out_shape=jax.ShapeDtypeStruct((b, e, cap, d), F32),
        compiler_params=_params("arbitrary", "arbitrary", "arbitrary"),
        name="moe_expert_down",
    )(hid, w2, gate[..., None])


def _expert_choice_moe(h, router_t, w1, w3, w2):
    b, t, d = h.shape
    cap = EC_CAPACITY_FACTOR * t // N_EXPERTS
    aff = _router_affinity(h, router_t)
    gate, idx = lax.top_k(aff, cap)
    xin = jax.vmap(lambda hb, ib: hb[ib])(h, idx)
    y = _expert_ffn(xin, gate, w1, w3, w2)
    return jax.vmap(lambda yb, ib: jnp.zeros((t, d), F32).at[ib].add(yb))(
        y.reshape(b, N_EXPERTS * cap, d), idx.reshape(b, N_EXPERTS * cap))


def _project(h, w, out_dtype, name):
    b, t, d = h.shape
    return _matmul(h.reshape(b * t, d), w, out_dtype, 512, 1024, name).reshape(b, t, w.shape[1])


def _ada_params(cond3, ada_w, ada_b):
    out = _matmul(cond3, ada_w, F32, 8, 512, "adaln")
    return out + ada_b[None]


def _mixer_na_s5(h, hc, w_qkv, w_u, w_out, rpb, s5_tables, glu_w, glu_b):
    b, t, _ = h.shape
    tc = hc.shape[1]
    qkv = _project(h, w_qkv, BF16, "ab_in_qkv")
    qkv_c = _project(hc, w_qkv, BF16, "ab_in_qkv_ctx")
    u = _project(h, w_u, F32, "ab_in_u")
    u_c = _project(hc, w_u, F32, "ab_in_u_ctx")
    o_na = _neighbourhood_attention(qkv, qkv_c, rpb)
    oc_na = _context_attention(qkv_c)
    y_s5 = _s5_scan(jnp.concatenate([u_c, u], axis=1), s5_tables, tc)
    z = _s5_glu(y_s5.reshape(b * (tc + t), S5_WIDTH), glu_w, glu_b).reshape(b, tc + t, S5_WIDTH)
    mixed = jnp.concatenate([o_na, z[:, tc:]], axis=-1)
    mixed_c = jnp.concatenate([oc_na, z[:, :tc]], axis=-1)
    return _project(mixed, w_out, F32, "ab_out"), _project(mixed_c, w_out, F32, "ab_out_ctx")


def _mixer_hgrn2(h, hc, w_in, w_out, lb, norm_g):
    proj = _project(h, w_in, F32, "hg_in")
    proj_c = _project(hc, w_in, F32, "hg_in_ctx")
    o2 = _hgrn2_gla(proj, proj_c, lb)
    return _project(_hgrn2_readout(o2, proj, norm_g), w_out, F32, "hg_out")


def kernel(x, c, ctx, c_ctx, ada_w, ada_b, norm_g, ab_w_in, ab_w_out, na_rpb, s5_lam_re, s5_lam_im,
           s5_log_dt, s5_b_re, s5_b_im, s5_c_re, s5_c_im, s5_d, s5_glu_w, s5_glu_b, hg_w_in, hg_w_out,
           hg_lb_logits, hg_norm_g, moe_router, moe_w1, moe_w3, moe_w2):
    bsz, _, d = x.shape
    depth = ada_w.shape[0]
    assert depth == 2 and bsz + 1 <= 8
    xc = ctx
    cond = jax.nn.silu(c)
    cond_ctx = jax.nn.silu(c_ctx)[None]
    cond3 = jnp.concatenate([cond, cond_ctx, jnp.zeros((8 - bsz - 1, d), F32)], axis=0)
    hg_lb = jnp.cumsum(jax.nn.softmax(hg_lb_logits.astype(F32), axis=0), axis=0)
    for layer in range(depth):
        need_ctx = layer < depth - 1
        ada = _ada_params(cond3, ada_w[layer], ada_b[layer])
        sh1, sc1, g1, sh2, sc2, g2 = jnp.split(ada[:bsz], 6, axis=-1)
        csh1, csc1, cg1, csh2, csc2, cg2 = [jnp.broadcast_to(a, (bsz, d))
                                            for a in jnp.split(ada[bsz:bsz + 1], 6, axis=-1)]
        ng = norm_g[layer].astype(F32)
        h = _modulate(x, ng[0], sh1, sc1)
        hc = _modulate(xc, ng[0], csh1, csc1)
        if layer == 0:
            w_in = ab_w_in[0].astype(BF16)
            tables = _s5_tables(s5_lam_re[0], s5_lam_im[0], s5_log_dt[0], s5_b_re[0], s5_b_im[0],
                                s5_c_re[0], s5_c_im[0], s5_d[0])
            y, yc = _mixer_na_s5(h, hc, w_in[:, :3 * NA_W], w_in[:, 3 * NA_W:], ab_w_out[0].astype(BF16),
                                 na_rpb[0], tables, s5_glu_w[0].astype(BF16), s5_glu_b[0].astype(F32))
        else:
            y = _mixer_hgrn2(h, hc, hg_w_in[0].astype(BF16), hg_w_out[0].astype(BF16), hg_lb[0],
                             hg_norm_g[0])
            yc = None
        x = _gated_residual(x, y, ng[1], g1)
        router_t = moe_router[layer].T.astype(BF16)
        w1, w3, w2 = (w[layer].astype(BF16) for w in (moe_w1, moe_w3, moe_w2))
        h2 = _modulate(x, ng[2], sh2, sc2)
        x = _gated_residual(x, _expert_choice_moe(h2, router_t, w1, w3, w2), ng[3], g2)
        if need_ctx:
            xc = _gated_residual(xc, yc, ng[1], cg1)
            hc2 = _modulate(xc, ng[2], csh2, csc2)
            xc = _gated_residual(xc, _expert_choice_moe(hc2, router_t, w1, w3, w2), ng[3], cg2)
    return x
```

```python
import functools
import math

import numpy as np
import jax
import jax.numpy as jnp
from jax import lax
from jax.experimental import pallas as pl
from jax.experimental.pallas import tpu as pltpu

F32 = jnp.float32
BF16 = jnp.bfloat16
HIGHEST = lax.Precision.HIGHEST

D_MODEL = 4096
GRID_W = 64
NORM_EPS = 1e-6
NA_HEADS = 16
NA_HEAD_DIM = 128
NA_WIN_R = 8
NA_WIN_C = 16
NA_W = NA_HEADS * NA_HEAD_DIM
S5_WIDTH = 2048
S5_GROUP = 16
S5_GROUPS = S5_WIDTH // S5_GROUP
S5_STATE = 64
S5_CHUNK = 16
HG_HEADS = 32
HG_DIM = 128
HG_CHUNK = 64
HG_Q = HG_HEADS * HG_DIM
ROPE_THETA = 10000.0
N_EXPERTS = 16
EXPERT_FF = 1024
EC_CAPACITY_FACTOR = 2

VMEM_LIMIT_BYTES = 56 * 1024 * 1024
MOE_FF_TILE = 512
MOE_OUT_TILE = 2048
NA_QROWS = 8
NA_KROWS = 16
NEG_BIG = -1e30

_NT = (((1,), (1,)), ((), ()))


def _params(*sem):
    return pltpu.CompilerParams(dimension_semantics=sem, vmem_limit_bytes=VMEM_LIMIT_BYTES)


def _mm_kernel(a_ref, w_ref, o_ref, wb_ref):
    @pl.when(pl.program_id(1) == 0)
    def _():
        wb_ref[...] = w_ref[...].astype(BF16)

    o_ref[...] = jnp.dot(a_ref[...].astype(BF16), wb_ref[...],
                         preferred_element_type=F32).astype(o_ref.dtype)


def _matmul(a, w, layer, out_dtype, name, col0=0, n=None, tm=1024, tn=512):
    m, k = a.shape
    n = w.shape[2] - col0 if n is None else n
    tm = min(tm, m)
    tn = min(tn, n)
    assert m % tm == 0 and n % tn == 0 and col0 % tn == 0, (m, n, col0, tm, tn)
    cb = col0 // tn
    return pl.pallas_call(
        _mm_kernel,
        grid=(n // tn, m // tm),
        in_specs=[pl.BlockSpec((tm, k), lambda j, i: (i, 0)),
                  pl.BlockSpec((None, k, tn), lambda j, i: (layer, 0, cb + j))],
        out_specs=pl.BlockSpec((tm, tn), lambda j, i: (i, j)),
        out_shape=jax.ShapeDtypeStruct((m, n), out_dtype),
        scratch_shapes=[pltpu.VMEM((k, tn), BF16)],
        compiler_params=_params("arbitrary", "arbitrary"),
        name=name,
    )(a, w)


def _modulate_kernel(x_ref, g_ref, sh_ref, sc_ref, o_ref):
    x = x_ref[0]
    ms = jnp.mean(x * x, axis=-1, keepdims=True)
    y = x * lax.rsqrt(ms + NORM_EPS) * g_ref[...]
    o_ref[0] = (y * (1.0 + sc_ref[0]) + sh_ref[0]).astype(o_ref.dtype)


def _modulate(x, g, shift, scale, tt=256):
    b, t, d = x.shape
    tt = min(tt, t)
    row = lambda bi, ti: (bi, 0, 0)
    return pl.pallas_call(
        _modulate_kernel,
        grid=(b, t // tt),
        in_specs=[pl.BlockSpec((1, tt, d), lambda bi, ti: (bi, ti, 0)),
                  pl.BlockSpec((1, d), lambda bi, ti: (0, 0)),
                  pl.BlockSpec((1, 1, d), row),
                  pl.BlockSpec((1, 1, d), row)],
        out_specs=pl.BlockSpec((1, tt, d), lambda bi, ti: (bi, ti, 0)),
        out_shape=jax.ShapeDtypeStruct((b, t, d), BF16),
        compiler_params=_params("arbitrary", "arbitrary"),
        name="modulate",
    )(x, g.reshape(1, d), shift.reshape(b, 1, d), scale.reshape(b, 1, d))


def _residual_kernel(x_ref, y_ref, g_ref, gate_ref, o_ref):
    y = y_ref[0].astype(F32)
    ms = jnp.mean(y * y, axis=-1, keepdims=True)
    o_ref[0] = x_ref[0] + gate_ref[0] * (y * lax.rsqrt(ms + NORM_EPS) * g_ref[...])


def _gated_residual(x, y, g, gate, tt=256):
    b, t, d = x.shape
    tt = min(tt, t)
    blk = pl.BlockSpec((1, tt, d), lambda bi, ti: (bi, ti, 0))
    return pl.pallas_call(
        _residual_kernel,
        grid=(b, t // tt),
        in_specs=[blk, blk,
                  pl.BlockSpec((1, d), lambda bi, ti: (0, 0)),
                  pl.BlockSpec((1, 1, d), lambda bi, ti: (bi, 0, 0))],
        out_specs=blk,
        out_shape=jax.ShapeDtypeStruct((b, t, d), F32),
        compiler_params=_params("arbitrary", "arbitrary"),
        name="gated_residual",
    )(x, y, g.reshape(1, d), gate.reshape(b, 1, d))


def _na_bias_table(rpb, rows):
    nblk = rows // NA_QROWS
    qc = np.arange(GRID_W)
    cs = np.clip(qc - NA_WIN_C // 2, 0, GRID_W - NA_WIN_C)
    kc = np.arange(GRID_W)
    in_c = (kc[None, :] >= cs[:, None]) & (kc[None, :] < cs[:, None] + NA_WIN_C)
    rel_c = kc[None, :] - qc[:, None] + NA_WIN_C - 1
    sel_c = (rel_c[:, :, None] == np.arange(2 * NA_WIN_C - 1)).astype(np.float32)
    tables = []
    for j in (0, 1, nblk - 1):
        qr = NA_QROWS * j + np.arange(NA_QROWS)
        ws = int(np.clip(NA_QROWS * j - NA_WIN_R // 2, 0, rows - NA_KROWS))
        kr = ws + np.arange(NA_KROWS)
        r0 = np.clip(qr - NA_WIN_R // 2, 0, rows - NA_WIN_R)
        in_r = (kr[None, :] >= r0[:, None]) & (kr[None, :] < r0[:, None] + NA_WIN_R)
        rel_r = kr[None, :] - qr[:, None] + NA_WIN_R - 1
        sel_r = (rel_r[:, :, None] == np.arange(2 * NA_WIN_R - 1)).astype(np.float32)
        valid = in_r[:, None, :, None] & in_c[None, :, None, :]
        by_row = jnp.einsum('hrc,qkr->hqkc', rpb.astype(F32), sel_r, precision=HIGHEST)
        bias = jnp.einsum('hqkc,xyc->hqxky', by_row, sel_c, precision=HIGHEST)
        bias = jnp.where(valid[None], bias, NEG_BIG)
        tables.append(bias.reshape(rpb.shape[0], NA_QROWS * GRID_W, NA_KROWS * GRID_W))
    return jnp.stack(tables)


def _na_kernel(q_ref, k_ref, v_ref, kc_ref, vc_ref, bias_ref, o_ref, *, rows):
    j = pl.program_id(2)
    ws = jnp.clip(NA_QROWS * j - NA_WIN_R // 2, 0, rows - NA_KROWS) * GRID_W
    ws = pl.multiple_of(ws, 256)
    nk = NA_KROWS * GRID_W
    scale = NA_HEAD_DIM ** -0.5
    q = q_ref[0]
    kw = k_ref[0, pl.ds(ws, nk), :]
    vw = v_ref[0, pl.ds(ws, nk), :]
    s_nb = lax.dot_general(q, kw, _NT, preferred_element_type=F32) * scale + bias_ref[0, 0]
    s_c = lax.dot_general(q, kc_ref[0], _NT, preferred_element_type=F32) * scale
    m = jnp.maximum(jnp.max(s_nb, axis=-1, keepdims=True), jnp.max(s_c, axis=-1, keepdims=True))
    p_nb = jnp.exp(s_nb - m)
    p_c = jnp.exp(s_c - m)
    denom = jnp.sum(p_nb, axis=-1, keepdims=True) + jnp.sum(p_c, axis=-1, keepdims=True)
    o = (jnp.dot(p_nb.astype(BF16), vw, preferred_element_type=F32)
         + jnp.dot(p_c.astype(BF16), vc_ref[0], preferred_element_type=F32))
    o_ref[0] = (o / denom).astype(o_ref.dtype)


def _neighbourhood_attention(qkv, qkv_c, rpb):
    b, t, _ = qkv.shape
    tc = qkv_c.shape[1]
    rows = t // GRID_W
    nblk = rows // NA_QROWS
    tq = NA_QROWS * GRID_W
    bias = _na_bias_table(rpb, rows)
    pat = lambda j: jnp.where(j == 0, 0, jnp.where(j == nblk - 1, 2, 1))
    return pl.pallas_call(
        functools.partial(_na_kernel, rows=rows),
        grid=(b, NA_HEADS, nblk),
        in_specs=[
            pl.BlockSpec((1, tq, NA_HEAD_DIM), lambda bi, h, j: (bi, j, h)),
            pl.BlockSpec((1, t, NA_HEAD_DIM), lambda bi, h, j: (bi, 0, NA_HEADS + h)),
            pl.BlockSpec((1, t, NA_HEAD_DIM), lambda bi, h, j: (bi, 0, 2 * NA_HEADS + h)),
            pl.BlockSpec((1, tc, NA_HEAD_DIM), lambda bi, h, j: (bi, 0, NA_HEADS + h)),
            pl.BlockSpec((1, tc, NA_HEAD_DIM), lambda bi, h, j: (bi, 0, 2 * NA_HEADS + h)),
            pl.BlockSpec((1, 1, tq, NA_KROWS * GRID_W), lambda bi, h, j: (pat(j), h, 0, 0)),
        ],
        out_specs=pl.BlockSpec((1, tq, NA_HEAD_DIM), lambda bi, h, j: (bi, j, h)),
        out_shape=jax.ShapeDtypeStruct((b, t, NA_W), BF16),
        compiler_params=_params("arbitrary", "arbitrary", "arbitrary"),
        name="neighbourhood_attention",
    )(qkv, qkv, qkv, qkv_c, qkv_c, bias)


def _ctx_attn_kernel(q_ref, k_ref, v_ref, o_ref):
    scale = NA_HEAD_DIM ** -0.5
    s = lax.dot_general(q_ref[0], k_ref[0], _NT, preferred_element_type=F32) * scale
    p = jnp.exp(s - jnp.max(s, axis=-1, keepdims=True))
    denom = jnp.sum(p, axis=-1, keepdims=True)
    o = jnp.dot(p.astype(BF16), v_ref[0], preferred_element_type=F32)
    o_ref[0] = (o / denom).astype(o_ref.dtype)


def _context_attention(qkv_c):
    b, tc, _ = qkv_c.shape
    blk = lambda off: pl.BlockSpec((1, tc, NA_HEAD_DIM), lambda bi, h: (bi, 0, off + h))
    return pl.pallas_call(
        _ctx_attn_kernel,
        grid=(b, NA_HEADS),
        in_specs=[blk(0), blk(NA_HEADS), blk(2 * NA_HEADS)],
        out_specs=blk(0),
        out_shape=jax.ShapeDtypeStruct((b, tc, NA_W), BF16),
        compiler_params=_params("arbitrary", "arbitrary"),
        name="context_attention",
    )(qkv_c, qkv_c, qkv_c)


def _s5_tables(lam_re, lam_im, log_dt, b_re, b_im, c_re, c_im, d_skip):
    ll, gi, p, g = S5_CHUNK, S5_GROUP, S5_STATE, S5_GROUPS
    f32 = F32
    lam = lax.complex(lam_re.astype(f32), lam_im.astype(f32))
    dt = jnp.exp(log_dt.astype(f32))[..., None]
    lam_dt = lam * dt
    lam_bar = jnp.exp(lam_dt)
    b_bar = ((lam_bar - 1.0) / lam)[..., None] * lax.complex(b_re.astype(f32), b_im.astype(f32))[None]
    c_mat = lax.complex(c_re.astype(f32), c_im.astype(f32))
    taus = jnp.arange(ll + 1, dtype=f32)
    pw = jnp.exp(lam_dt[None] * taus[:, None, None, None])

    def cmul_sum(a, b, spec):
        ar, ai, br, bi = jnp.real(a), jnp.imag(a), jnp.real(b), jnp.imag(b)
        e = lambda x, y: jnp.einsum(spec, x, y, precision=HIGHEST)
        return e(ar, br) - e(ai, bi), e(ar, bi) + e(ai, br)

    cp = c_mat[:, None] * jnp.moveaxis(pw[:ll], 0, 1)[:, :, :, None, :]
    k_re, _ = cmul_sum(cp, b_bar, 'dtgip,dgpj->dtgij')
    idx = np.arange(ll)
    tau_f = idx[None, :] - idx[:, None]
    kf = jnp.where((tau_f >= 0)[:, :, None, None, None], k_re[0][np.clip(tau_f, 0, ll - 1)], 0.0)
    kb = jnp.where((tau_f <= 0)[:, :, None, None, None], k_re[1][np.clip(-tau_f, 0, ll - 1)], 0.0)
    skip = (jnp.eye(ll, dtype=f32)[:, :, None, None, None]
            * (d_skip.astype(f32)[:, :, None] * jnp.eye(gi, dtype=f32))[None, None])
    t_sum = kf + kb + skip
    t_sum = jnp.transpose(t_sum, (2, 0, 4, 1, 3)).reshape(g, ll * gi, ll * gi)

    def pad_state(x):
        return jnp.pad(x, [(0, 0)] * (x.ndim - 1) + [(0, 128 - p)])

    wf = pw[ll - 1 - idx, 0][:, :, :, None] * b_bar[0][None]
    wb = pw[idx, 1][:, :, :, None] * b_bar[1][None]
    to_rows = lambda w: jnp.transpose(w, (1, 0, 3, 2)).reshape(g, ll * gi, p)
    w_blocks = [pad_state(to_rows(f(w))) for w in (wf, wb) for f in (jnp.real, jnp.imag)]
    m_in = jnp.concatenate([t_sum] + w_blocks, axis=-1)

    zf = c_mat[0][None] * pw[idx + 1, 0][:, :, None, :]
    zb = c_mat[1][None] * pw[ll - idx, 1][:, :, None, :]
    to_cols = lambda z: pad_state(jnp.transpose(z, (1, 0, 2, 3)).reshape(g, ll * gi, p))
    z_blocks = [jnp.swapaxes(to_cols(f(z)), 1, 2)
                for z in (zf, zb) for f in (jnp.real, lambda v: -jnp.imag(v))]
    z_out = jnp.concatenate(z_blocks, axis=1)

    lam_l = jnp.stack([pad_state(f(pw[ll, d])) for d in (0, 1) for f in (jnp.real, jnp.imag)], axis=1)
    return m_in, z_out, lam_l


def _s5_kernel(u_ref, m_ref, z_ref, lam_ref, y_ref, r_ref, s_ref, *, nbatch, nchunk, nctx):
    width = S5_CHUNK * S5_GROUP
    r_ref[...] = jnp.dot(u_ref[0], m_ref[0], precision=HIGHEST, preferred_element_type=F32)
    lam = lam_ref[0]
    lfr, lfi, lbr, lbi = lam[0:1], lam[1:2], lam[2:3], lam[3:4]
    sl = lambda i: slice(128 * i, 128 * (i + 1))
    vsl = lambda i: slice(width + 128 * i, width + 128 * (i + 1))

    sub = 8
    ngrp, nctx_g = nchunk // sub, nctx // sub

    def body(kg, carry):
        gb = jnp.where(kg < nctx_g, nctx_g - 1 - kg, ngrp - 1 - (kg - nctx_g))
        out = []
        for bi in range(nbatch):
            fr, fi, br, bim = carry[bi]
            rf = pl.ds(pl.multiple_of(bi * nchunk + kg * sub, sub), sub)
            rb = pl.ds(pl.multiple_of(bi * nchunk + gb * sub, sub), sub)
            vfr, vfi = r_ref[rf, vsl(0)], r_ref[rf, vsl(1)]
            vbr, vbi = r_ref[rb, vsl(2)], r_ref[rb, vsl(3)]
            sfr, sfi, sbr, sbi = [], [], [None] * sub, [None] * sub
            for i in range(sub):
                sfr.append(fr)
                sfi.append(fi)
                fr, fi = (lfr * fr - lfi * fi + vfr[i:i + 1], lfr * fi + lfi * fr + vfi[i:i + 1])
                j = sub - 1 - i
                sbr[j], sbi[j] = br, bim
                br, bim = (lbr * br - lbi * bim + vbr[j:j + 1], lbr * bim + lbi * br + vbi[j:j + 1])
            s_ref[rf, sl(0)] = jnp.concatenate(sfr, axis=0)
            s_ref[rf, sl(1)] = jnp.concatenate(sfi, axis=0)
            s_ref[rb, sl(2)] = jnp.concatenate(sbr, axis=0)
            s_ref[rb, sl(3)] = jnp.concatenate(sbi, axis=0)
            out.append((fr, fi, br, bim))
        return tuple(out)

    zero = jnp.zeros((1, 128), F32)
    lax.fori_loop(0, ngrp, body, tuple((zero, zero, zero, zero) for _ in range(nbatch)))
    y_ref[0] = r_ref[:, 0:width] + jnp.dot(s_ref[...], z_ref[0], precision=HIGHEST,
                                           preferred_element_type=F32)


def _s5_scan(u_cat, tables, nctx_tokens):
    m_in, z_out, lam_l = tables
    b, tt, _ = u_cat.shape
    ll, gi, g = S5_CHUNK, S5_GROUP, S5_GROUPS
    n = tt // ll
    width = ll * gi
    ug = u_cat.reshape(b, n, ll, g, gi).transpose(3, 0, 1, 2, 4).reshape(g, b * n, width)
    y = pl.pallas_call(
        functools.partial(_s5_kernel, nbatch=b, nchunk=n, nctx=nctx_tokens // ll),
        grid=(g,),
        in_specs=[pl.BlockSpec((1, b * n, width), lambda i: (i, 0, 0)),
                  pl.BlockSpec((1, width, width + 512), lambda i: (i, 0, 0)),
                  pl.BlockSpec((1, 512, width), lambda i: (i, 0, 0)),
                  pl.BlockSpec((1, 4, 128), lambda i: (i, 0, 0))],
        out_specs=pl.BlockSpec((1, b * n, width), lambda i: (i, 0, 0)),
        out_shape=jax.ShapeDtypeStruct((g, b * n, width), F32),
        scratch_shapes=[pltpu.VMEM((b * n, width + 512), F32), pltpu.VMEM((b * n, 512), F32)],
        compiler_params=_params("arbitrary"),
        name="s5_scan",
    )(ug, m_in, z_out, lam_l)
    return y.reshape(g, b, n, ll, gi).transpose(1, 2, 3, 0, 4).reshape(b, tt, g * gi)


def _glu_kernel(y_ref, w_ref, b_ref, o_ref, wb_ref):
    @pl.when(pl.program_id(0) == 0)
    def _():
        wb_ref[...] = w_ref[...].astype(BF16)

    z = jax.nn.gelu(y_ref[...])
    a = jnp.dot(z.astype(BF16), wb_ref[...], preferred_element_type=F32) + b_ref[...]
    o_ref[...] = (z * jax.nn.sigmoid(a)).astype(o_ref.dtype)


def _s5_glu(y, w, bias, tm=256):
    m, n = y.shape
    tm = min(tm, m)
    return pl.pallas_call(
        _glu_kernel,
        grid=(m // tm,),
        in_specs=[pl.BlockSpec((tm, n), lambda i: (i, 0)),
                  pl.BlockSpec((n, n), lambda i: (0, 0)),
                  pl.BlockSpec((1, n), lambda i: (0, 0))],
        out_specs=pl.BlockSpec((tm, n), lambda i: (i, 0)),
        out_shape=jax.ShapeDtypeStruct((m, n), BF16),
        scratch_shapes=[pltpu.VMEM((n, n), BF16)],
        compiler_params=_params("arbitrary"),
        name="s5_glu",
    )(y, w, bias.reshape(1, n))


def _gla_state_kernel(f_ref, v_ref, lb_ref, tri_ref, s_ref, *, nchunk):
    d = pl.program_id(2)
    lb = lb_ref[0]
    st = jnp.zeros((HG_DIM, HG_DIM), F32)
    for c in range(nchunk):
        cc = jnp.where(d == 0, c, nchunk - 1 - c)
        sl = pl.ds(pl.multiple_of(cc * HG_CHUNK, HG_CHUNK), HG_CHUNK)
        f = lb + (1.0 - lb) * jax.nn.sigmoid(f_ref[0, sl, :])
        logf = jnp.log(f)
        b = jnp.dot(tri_ref[0], logf, precision=HIGHEST, preferred_element_type=F32)
        tot = jnp.sum(logf, axis=0, keepdims=True)
        kd = ((1.0 - f) * jnp.exp(tot - b)).astype(BF16)
        st = st * jnp.exp(tot) + jnp.dot(v_ref[0, sl, :].T.astype(BF16), kd, preferred_element_type=F32)
    s_ref[0, 0, 0] = st


def _gla_kernel(q_ref, f_ref, v_ref, cos_ref, sin_ref, lb_ref, tri_ref, s0_ref, o_ref, st_ref,
                *, nchunk, reverse):
    @pl.when(pl.program_id(2) == 0)
    def _():
        st_ref[...] = s0_ref[0, 0, 0]

    pair = 2 * HG_CHUNK
    lb = lb_ref[0]
    tri = tri_ref[...]
    keep = tri > 0
    f = lb + (1.0 - lb) * jax.nn.sigmoid(f_ref[0])
    logf = jnp.log(f)
    k = 1.0 - f
    v = v_ref[0]
    sg = jax.nn.sigmoid(q_ref[0])
    lane = lax.broadcasted_iota(jnp.int32, sg.shape, 1)
    quarter = HG_DIM // 4
    partner = jnp.where((lane & quarter) == 0, pltpu.roll(sg, HG_DIM - quarter, 1),
                        pltpu.roll(sg, quarter, 1))
    q = sg * cos_ref[...] + partner * sin_ref[...]
    hi = logf.astype(BF16)
    rem = logf - hi.astype(F32)
    mid = rem.astype(BF16)
    lo = (rem - mid.astype(F32)).astype(BF16)
    st = st_ref[...]
    pairs = range(nchunk // 2)
    for p in (reversed(pairs) if reverse else pairs):
        sl = slice(pair * p, pair * (p + 1))
        b = (jnp.dot(tri, hi[sl], preferred_element_type=F32)
             + jnp.dot(tri, mid[sl], preferred_element_type=F32)
             + jnp.dot(tri, lo[sl], preferred_element_type=F32))
        qe = (q[sl] * jnp.exp(b)).astype(BF16)
        ke = (k[sl] * jnp.exp(-b)).astype(BF16)
        att = jnp.where(keep, lax.dot_general(qe, ke, _NT, preferred_element_type=F32), 0.0)
        o_intra = jnp.dot(att.astype(BF16), v[sl].astype(BF16), preferred_element_type=F32)
        o_inter = [None, None]
        for half in ((1, 0) if reverse else (0, 1)):
            hs = slice(HG_CHUNK * half, HG_CHUNK * (half + 1))
            rows = slice(pair * p + HG_CHUNK * half, pair * p + HG_CHUNK * (half + 1))
            tot = jnp.sum(logf[rows], axis=0, keepdims=True)
            kd = (k[rows] * jnp.exp(tot - b[hs])).astype(BF16)
            o_inter[half] = lax.dot_general(qe[hs], st.astype(BF16), _NT, preferred_element_type=F32)
            st = st * jnp.exp(tot) + jnp.dot(v[rows].T.astype(BF16), kd, preferred_element_type=F32)
        o_ref[0, sl, :] = o_intra + jnp.concatenate(o_inter, axis=0)
    st_ref[...] = st


def _rope_tables(t):
    pos = np.arange(t)
    quarter = HG_DIM // 4
    inv_freq = ROPE_THETA ** (-jnp.arange(quarter, dtype=F32) / quarter)

    def tab(p):
        ang = jnp.asarray(p, F32)[:, None] * inv_freq[None, :]
        c, s = jnp.cos(ang), jnp.sin(ang)
        return jnp.concatenate([c, c], axis=-1), jnp.concatenate([-s, s], axis=-1)

    c_r, s_r = tab(pos // GRID_W)
    c_c, s_c = tab(pos % GRID_W)
    return jnp.concatenate([c_r, c_c], axis=-1), jnp.concatenate([s_r, s_c], axis=-1)


def _gla_masks():
    i = np.arange(HG_CHUNK)
    lower = (i[:, None] >= i[None, :]).astype(np.float32)
    return jnp.asarray(np.stack([lower, lower.T]))


def _hgrn2_gla(proj, proj_c, lb, tblock=1024):
    b, t, _ = proj.shape
    tc = proj_c.shape[1]
    tblock = min(tblock, t)
    nt = t // tblock
    tri = _gla_masks()
    lb3 = lb.astype(F32).reshape(HG_HEADS, 1, HG_DIM)

    s0 = pl.pallas_call(
        functools.partial(_gla_state_kernel, nchunk=tc // HG_CHUNK),
        grid=(b, HG_HEADS, 2),
        in_specs=[pl.BlockSpec((1, tc, HG_DIM), lambda bi, h, d: (bi, 0, HG_HEADS * (1 + d) + h)),
                  pl.BlockSpec((1, tc, HG_DIM), lambda bi, h, d: (bi, 0, 3 * HG_HEADS + h)),
                  pl.BlockSpec((1, 1, HG_DIM), lambda bi, h, d: (h, 0, 0)),
                  pl.BlockSpec((1, HG_CHUNK, HG_CHUNK), lambda bi, h, d: (d, 0, 0))],
        out_specs=pl.BlockSpec((1, 1, 1, HG_DIM, HG_DIM), lambda bi, h, d: (bi, h, d, 0, 0)),
        out_shape=jax.ShapeDtypeStruct((b, HG_HEADS, 2, HG_DIM, HG_DIM), F32),
        compiler_params=_params("arbitrary", "arbitrary", "arbitrary"),
        name="hgrn2_ctx_state",
    )(proj_c, proj_c, lb3, tri)

    cos, sin = _rope_tables(t)
    eye2 = jnp.eye(2, dtype=F32)
    outs = []
    for d in (0, 1):
        tb = (lambda ti: ti) if d == 0 else (lambda ti: nt - 1 - ti)
        tri2 = jnp.kron(eye2, tri[d]).astype(BF16)
        outs.append(pl.pallas_call(
            functools.partial(_gla_kernel, nchunk=tblock // HG_CHUNK, reverse=bool(d)),
            grid=(b, HG_HEADS, nt),
            in_specs=[pl.BlockSpec((1, tblock, HG_DIM), lambda bi, h, ti, tb=tb: (bi, tb(ti), h)),
                      pl.BlockSpec((1, tblock, HG_DIM),
                                   lambda bi, h, ti, tb=tb, d=d: (bi, tb(ti), HG_HEADS * (1 + d) + h)),
                      pl.BlockSpec((1, tblock, HG_DIM),
                                   lambda bi, h, ti, tb=tb: (bi, tb(ti), 3 * HG_HEADS + h)),
                      pl.BlockSpec((tblock, HG_DIM), lambda bi, h, ti, tb=tb: (tb(ti), 0)),
                      pl.BlockSpec((tblock, HG_DIM), lambda bi, h, ti, tb=tb: (tb(ti), 0)),
                      pl.BlockSpec((1, 1, HG_DIM), lambda bi, h, ti: (h, 0, 0)),
                      pl.BlockSpec((2 * HG_CHUNK, 2 * HG_CHUNK), lambda bi, h, ti: (0, 0)),
                      pl.BlockSpec((1, 1, 1, HG_DIM, HG_DIM), lambda bi, h, ti, d=d: (bi, h, d, 0, 0))],
            out_specs=pl.BlockSpec((1, tblock, HG_DIM), lambda bi, h, ti, tb=tb: (bi, tb(ti), h)),
            out_shape=jax.ShapeDtypeStruct((b, t, HG_Q), F32),
            scratch_shapes=[pltpu.VMEM((HG_DIM, HG_DIM), F32)],
            compiler_params=_params("arbitrary", "arbitrary", "arbitrary"),
            name="hgrn2_gla_bwd" if d else "hgrn2_gla_fwd",
        )(proj, proj, proj, cos, sin, lb3, tri2, s0))
    return outs


def _readout_kernel(of_ref, ob_ref, gate_ref, g_ref, o_ref):
    o = of_ref[0] + ob_ref[0]
    ms = jnp.mean(o * o, axis=-1, keepdims=True)
    y = o * lax.rsqrt(ms + NORM_EPS) * g_ref[0]
    gate = gate_ref[0]
    o_ref[0] = (y * (gate * jax.nn.sigmoid(gate))).astype(o_ref.dtype)


def _hgrn2_readout(o_fwd, o_bwd, proj, norm_g, tt=1024):
    b, t, _ = o_fwd.shape
    tt = min(tt, t)
    ospec = lambda: pl.BlockSpec((1, tt, HG_DIM), lambda bi, ti, h: (bi, ti, h))
    return pl.pallas_call(
        _readout_kernel,
        grid=(b, t // tt, HG_HEADS),
        in_specs=[ospec(), ospec(),
                  pl.BlockSpec((1, tt, HG_DIM), lambda bi, ti, h: (bi, ti, 4 * HG_HEADS + h)),
                  pl.BlockSpec((1, 1, HG_DIM), lambda bi, ti, h: (h, 0, 0))],
        out_specs=pl.BlockSpec((1, tt, HG_DIM), lambda bi, ti, h: (bi, ti, h)),
        out_shape=jax.ShapeDtypeStruct((b, t, HG_Q), BF16),
        compiler_params=_params("arbitrary", "arbitrary", "arbitrary"),
        name="hgrn2_readout",
    )(o_fwd, o_bwd, proj, norm_g.astype(F32).reshape(HG_HEADS, 1, HG_DIM))


def _router_kernel(h_ref, r_ref, o_ref):
    logits = lax.dot_general(r_ref[...], h_ref[0], _NT, preferred_element_type=F32)
    p = jnp.exp(logits - jnp.max(logits, axis=0, keepdims=True))
    o_ref[0] = p / jnp.sum(p, axis=0, keepdims=True)


def _router_affinity(h, router_t, tt=512):
    b, t, d = h.shape
    tt = min(tt, t)
    return pl.pallas_call(
        _router_kernel,
        grid=(b, t // tt),
        in_specs=[pl.BlockSpec((1, tt, d), lambda bi, ti: (bi, ti, 0)),
                  pl.BlockSpec((N_EXPERTS, d), lambda bi, ti: (0, 0))],
        out_specs=pl.BlockSpec((1, N_EXPERTS, tt), lambda bi, ti: (bi, 0, ti)),
        out_shape=jax.ShapeDtypeStruct((b, N_EXPERTS, t), F32),
        compiler_params=_params("arbitrary", "arbitrary"),
        name="moe_router",
    )(h, router_t)


def _first_row_tile():
    return jnp.logical_and(pl.program_id(2) == 0, pl.program_id(3) == 0)


def _expert_up_kernel(x_ref, w1_ref, w3_ref, o_ref, w1b_ref, w3b_ref):
    @pl.when(_first_row_tile())
    def _():
        w1b_ref[...] = w1_ref[...].astype(BF16)
        w3b_ref[...] = w3_ref[...].astype(BF16)

    x = x_ref[0, 0]
    a = jnp.dot(x, w1b_ref[...], preferred_element_type=F32)
    g = jnp.dot(x, w3b_ref[...], preferred_element_type=F32)
    o_ref[0, 0] = (a * jax.nn.sigmoid(a) * g).astype(o_ref.dtype)


def _expert_down_kernel(h_ref, w2_ref, gate_ref, o_ref, w2b_ref):
    @pl.when(_first_row_tile())
    def _():
        w2b_ref[...] = w2_ref[...].astype(BF16)

    y = jnp.dot(h_ref[0, 0], w2b_ref[...], preferred_element_type=F32)
    o_ref[0, 0] = y * gate_ref[0, 0]


def _expert_ffn(xin, gate, w1, w3, w2, layer, tm=512):
    b, e, cap, d = xin.shape
    ff = w1.shape[-1]
    tm = min(tm, cap)
    ft, ot = min(MOE_FF_TILE, ff), min(MOE_OUT_TILE, d)
    hid = pl.pallas_call(
        _expert_up_kernel,
        grid=(e, ff // ft, b, cap // tm),
        in_specs=[pl.BlockSpec((1, 1, tm, d), lambda ei, fi, bi, mi: (bi, ei, mi, 0)),
                  pl.BlockSpec((None, None, d, ft), lambda ei, fi, bi, mi: (layer, ei, 0, fi)),
                  pl.BlockSpec((None, None, d, ft), lambda ei, fi, bi, mi: (layer, ei, 0, fi))],
        out_specs=pl.BlockSpec((1, 1, tm, ft), lambda ei, fi, bi, mi: (bi, ei, mi, fi)),
        out_shape=jax.ShapeDtypeStruct((b, e, cap, ff), BF16),
        scratch_shapes=[pltpu.VMEM((d, ft), BF16), pltpu.VMEM((d, ft), BF16)],
        compiler_params=_params("arbitrary", "arbitrary", "arbitrary", "arbitrary"),
        name="moe_expert_up",
    )(xin, w1, w3)
    return pl.pallas_call(
        _expert_down_kernel,
        grid=(e, d // ot, b, cap // tm),
        in_specs=[pl.BlockSpec((1, 1, tm, ff), lambda ei, ni, bi, mi: (bi, ei, mi, 0)),
                  pl.BlockSpec((None, None, ff, ot), lambda ei, ni, bi, mi: (layer, ei, 0, ni)),
                  pl.BlockSpec((1, 1, tm, 1), lambda ei, ni, bi, mi: (bi, ei, mi, 0))],
        out_specs=pl.BlockSpec((1, 1, tm, ot), lambda ei, ni, bi, mi: (bi, ei, mi, ni)),
        out_shape=jax.ShapeDtypeStruct((b, e, cap, d), F32),
        scratch_shapes=[pltpu.VMEM((ff, ot), BF16)],
        compiler_params=_params("arbitrary", "arbitrary", "arbitrary", "arbitrary"),
        name="moe_expert_down",
    )(hid, w2, gate[..., None])


def _expert_choice_moe(h, router_t, w1, w3, w2, layer):
    b, t, d = h.shape
    cap = EC_CAPACITY_FACTOR * t // N_EXPERTS
    aff = _router_affinity(h, router_t)
    gate, idx = lax.top_k(aff, cap)
    xin = jax.vmap(lambda hb, ib: hb[ib])(h, idx)
    y = _expert_ffn(xin, gate, w1, w3, w2, layer)
    return jax.vmap(lambda yb, ib: jnp.zeros((t, d), F32).at[ib].add(yb))(
        y.reshape(b, N_EXPERTS * cap, d), idx.reshape(b, N_EXPERTS * cap))


def _project(h, w, out_dtype, name, col0=0, n=None):
    b, t, d = h.shape
    out = _matmul(h.reshape(b * t, d), w, 0, out_dtype, name, col0=col0, n=n)
    return out.reshape(b, t, out.shape[1])


def _mixer_na_s5(h, hc, w_in, w_out, rpb, s5_tables, glu_w, glu_b):
    b, t, _ = h.shape
    tc = hc.shape[1]
    qkv = _project(h, w_in, BF16, "ab_in_qkv", 0, 3 * NA_W)
    qkv_c = _project(hc, w_in, BF16, "ab_in_qkv_ctx", 0, 3 * NA_W)
    u = _project(h, w_in, F32, "ab_in_u", 3 * NA_W, S5_WIDTH)
    u_c = _project(hc, w_in, F32, "ab_in_u_ctx", 3 * NA_W, S5_WIDTH)
    o_na = _neighbourhood_attention(qkv, qkv_c, rpb)
    oc_na = _context_attention(qkv_c)
    y_s5 = _s5_scan(jnp.concatenate([u_c, u], axis=1), s5_tables, tc)
    z = _s5_glu(y_s5.reshape(b * (tc + t), S5_WIDTH), glu_w, glu_b).reshape(b, tc + t, S5_WIDTH)
    mixed = jnp.concatenate([o_na, z[:, tc:]], axis=-1)
    mixed_c = jnp.concatenate([oc_na, z[:, :tc]], axis=-1)
    return _project(mixed, w_out, F32, "ab_out"), _project(mixed_c, w_out, F32, "ab_out_ctx")


def _mixer_hgrn2(h, hc, w_in, w_out, lb, norm_g):
    proj = _project(h, w_in, F32, "hg_in")
    proj_c = _project(hc, w_in, F32, "hg_in_ctx")
    o_fwd, o_bwd = _hgrn2_gla(proj, proj_c, lb)
    return _project(_hgrn2_readout(o_fwd, o_bwd, proj, norm_g), w_out, F32, "hg_out")


def kernel(x, c, ctx, c_ctx, ada_w, ada_b, norm_g, ab_w_in, ab_w_out, na_rpb, s5_lam_re, s5_lam_im,
           s5_log_dt, s5_b_re, s5_b_im, s5_c_re, s5_c_im, s5_d, s5_glu_w, s5_glu_b, hg_w_in, hg_w_out,
           hg_lb_logits, hg_norm_g, moe_router, moe_w1, moe_w3, moe_w2):
    bsz, _, d = x.shape
    depth = ada_w.shape[0]
    assert depth == 2 and bsz + 1 <= 8
    assert ab_w_in.shape[0] == 1 and hg_w_in.shape[0] == 1
    xc = ctx
    cond = jax.nn.silu(c)
    cond_ctx = jax.nn.silu(c_ctx)[None]
    cond3 = jnp.concatenate([cond, cond_ctx, jnp.zeros((8 - bsz - 1, d), F32)], axis=0)
    hg_lb = jnp.cumsum(jax.nn.softmax(hg_lb_logits.astype(F32), axis=0), axis=0)
    moe_w = (moe_w1, moe_w3, moe_w2)
    for layer in range(depth):
        need_ctx = layer < depth - 1
        ada = _matmul(cond3, ada_w, layer, F32, "adaln") + ada_b[layer][None]
        sh1, sc1, g1, sh2, sc2, g2 = jnp.split(ada[:bsz], 6, axis=-1)
        csh1, csc1, cg1, csh2, csc2, cg2 = [jnp.broadcast_to(a, (bsz, d))
                                            for a in jnp.split(ada[bsz:bsz + 1], 6, axis=-1)]
        ng = norm_g[layer].astype(F32)
        h = _modulate(x, ng[0], sh1, sc1)
        hc = _modulate(xc, ng[0], csh1, csc1)
        if layer == 0:
            tables = _s5_tables(s5_lam_re[0], s5_lam_im[0], s5_log_dt[0], s5_b_re[0], s5_b_im[0],
                                s5_c_re[0], s5_c_im[0], s5_d[0])
            y, yc = _mixer_na_s5(h, hc, ab_w_in, ab_w_out, na_rpb[0], tables, s5_glu_w[0],
                                 s5_glu_b[0].astype(F32))
        else:
            y = _mixer_hgrn2(h, hc, hg_w_in, hg_w_out, hg_lb[0], hg_norm_g[0])
            yc = None
        x = _gated_residual(x, y, ng[1], g1)
        router_t = moe_router[layer].T.astype(BF16)
        h2 = _modulate(x, ng[2], sh2, sc2)
        x = _gated_residual(x, _expert_choice_moe(h2, router_t, *moe_w, layer), ng[3], g2)
        if need_ctx:
            xc = _gated_residual(xc, yc, ng[1], cg1)
            hc2 = _modulate(xc, ng[2], csh2, csc2)
            xc = _gated_residual(xc, _expert_choice_moe(hc2, router_t, *moe_w, layer), ng[3], cg2)
    return x
```

```python
import functools
import math

import numpy as np
import jax
import jax.numpy as jnp
from jax import lax
from jax.experimental import pallas as pl
from jax.experimental.pallas import tpu as pltpu

F32 = jnp.float32
BF16 = jnp.bfloat16
HIGHEST = lax.Precision.HIGHEST

D_MODEL = 4096
GRID_W = 64
NORM_EPS = 1e-6
NA_HEADS = 16
NA_HEAD_DIM = 128
NA_WIN_R = 8
NA_WIN_C = 16
NA_W = NA_HEADS * NA_HEAD_DIM
S5_WIDTH = 2048
S5_GROUP = 16
S5_GROUPS = S5_WIDTH // S5_GROUP
S5_STATE = 64
S5_CHUNK = 16
HG_HEADS = 32
HG_DIM = 128
HG_CHUNK = 64
HG_Q = HG_HEADS * HG_DIM
ROPE_THETA = 10000.0
N_EXPERTS = 16
EXPERT_FF = 1024
EC_CAPACITY_FACTOR = 2

VMEM_LIMIT_BYTES = 56 * 1024 * 1024
MOE_FF_TILE = 512
MOE_OUT_TILE = 2048
NA_QROWS = 8
NA_KROWS = 16
NEG_BIG = -1e30

_NT = (((1,), (1,)), ((), ()))


def _params(*sem):
    return pltpu.CompilerParams(dimension_semantics=sem, vmem_limit_bytes=VMEM_LIMIT_BYTES)


def _mm_kernel(a_ref, w_ref, o_ref, wb_ref):
    @pl.when(pl.program_id(1) == 0)
    def _():
        wb_ref[...] = w_ref[...].astype(BF16)

    o_ref[...] = jnp.dot(a_ref[...].astype(BF16), wb_ref[...],
                         preferred_element_type=F32).astype(o_ref.dtype)


def _matmul(a, w, layer, out_dtype, name, col0=0, n=None, tm=1024, tn=512):
    m, k = a.shape
    n = w.shape[2] - col0 if n is None else n
    tm = min(tm, m)
    tn = min(tn, n)
    assert m % tm == 0 and n % tn == 0 and col0 % tn == 0, (m, n, col0, tm, tn)
    cb = col0 // tn
    return pl.pallas_call(
        _mm_kernel,
        grid=(n // tn, m // tm),
        in_specs=[pl.BlockSpec((tm, k), lambda j, i: (i, 0)),
                  pl.BlockSpec((None, k, tn), lambda j, i: (layer, 0, cb + j))],
        out_specs=pl.BlockSpec((tm, tn), lambda j, i: (i, j)),
        out_shape=jax.ShapeDtypeStruct((m, n), out_dtype),
        scratch_shapes=[pltpu.VMEM((k, tn), BF16)],
        compiler_params=_params("arbitrary", "arbitrary"),
        name=name,
    )(a, w)


def _modulate_kernel(x_ref, g_ref, sh_ref, sc_ref, o_ref):
    x = x_ref[0]
    ms = jnp.mean(x * x, axis=-1, keepdims=True)
    y = x * lax.rsqrt(ms + NORM_EPS) * g_ref[...]
    o_ref[0] = (y * (1.0 + sc_ref[0]) + sh_ref[0]).astype(o_ref.dtype)


def _modulate(x, g, shift, scale, out_dtype=BF16, tt=256):
    b, t, d = x.shape
    tt = min(tt, t)
    row = lambda bi, ti: (bi, 0, 0)
    return pl.pallas_call(
        _modulate_kernel,
        grid=(b, t // tt),
        in_specs=[pl.BlockSpec((1, tt, d), lambda bi, ti: (bi, ti, 0)),
                  pl.BlockSpec((1, d), lambda bi, ti: (0, 0)),
                  pl.BlockSpec((1, 1, d), row),
                  pl.BlockSpec((1, 1, d), row)],
        out_specs=pl.BlockSpec((1, tt, d), lambda bi, ti: (bi, ti, 0)),
        out_shape=jax.ShapeDtypeStruct((b, t, d), out_dtype),
        compiler_params=_params("arbitrary", "arbitrary"),
        name="modulate",
    )(x, g.reshape(1, d), shift.reshape(b, 1, d), scale.reshape(b, 1, d))


def _residual_kernel(x_ref, y_ref, g_ref, gate_ref, o_ref):
    y = y_ref[0].astype(F32)
    ms = jnp.mean(y * y, axis=-1, keepdims=True)
    o_ref[0] = x_ref[0] + gate_ref[0] * (y * lax.rsqrt(ms + NORM_EPS) * g_ref[...])


def _gated_residual(x, y, g, gate, tt=256):
    b, t, d = x.shape
    tt = min(tt, t)
    blk = pl.BlockSpec((1, tt, d), lambda bi, ti: (bi, ti, 0))
    return pl.pallas_call(
        _residual_kernel,
        grid=(b, t // tt),
        in_specs=[blk, blk,
                  pl.BlockSpec((1, d), lambda bi, ti: (0, 0)),
                  pl.BlockSpec((1, 1, d), lambda bi, ti: (bi, 0, 0))],
        out_specs=blk,
        out_shape=jax.ShapeDtypeStruct((b, t, d), F32),
        compiler_params=_params("arbitrary", "arbitrary"),
        name="gated_residual",
    )(x, y, g.reshape(1, d), gate.reshape(b, 1, d))


def _na_bias_table(rpb, rows):
    nblk = rows // NA_QROWS
    qc = np.arange(GRID_W)
    cs = np.clip(qc - NA_WIN_C // 2, 0, GRID_W - NA_WIN_C)
    kc = np.arange(GRID_W)
    in_c = (kc[None, :] >= cs[:, None]) & (kc[None, :] < cs[:, None] + NA_WIN_C)
    rel_c = kc[None, :] - qc[:, None] + NA_WIN_C - 1
    sel_c = (rel_c[:, :, None] == np.arange(2 * NA_WIN_C - 1)).astype(np.float32)
    tables = []
    for j in (0, 1, nblk - 1):
        qr = NA_QROWS * j + np.arange(NA_QROWS)
        ws = int(np.clip(NA_QROWS * j - NA_WIN_R // 2, 0, rows - NA_KROWS))
        kr = ws + np.arange(NA_KROWS)
        r0 = np.clip(qr - NA_WIN_R // 2, 0, rows - NA_WIN_R)
        in_r = (kr[None, :] >= r0[:, None]) & (kr[None, :] < r0[:, None] + NA_WIN_R)
        rel_r = kr[None, :] - qr[:, None] + NA_WIN_R - 1
        sel_r = (rel_r[:, :, None] == np.arange(2 * NA_WIN_R - 1)).astype(np.float32)
        valid = in_r[:, None, :, None] & in_c[None, :, None, :]
        by_row = jnp.einsum('hrc,qkr->hqkc', rpb.astype(F32), sel_r, precision=HIGHEST)
        bias = jnp.einsum('hqkc,xyc->hqxky', by_row, sel_c, precision=HIGHEST)
        bias = jnp.where(valid[None], bias, NEG_BIG)
        tables.append(bias.reshape(rpb.shape[0], NA_QROWS * GRID_W, NA_KROWS * GRID_W))
    return jnp.stack(tables)


def _na_kernel(q_ref, k_ref, v_ref, kc_ref, vc_ref, bias_ref, o_ref, *, rows):
    j = pl.program_id(2)
    ws = jnp.clip(NA_QROWS * j - NA_WIN_R // 2, 0, rows - NA_KROWS) * GRID_W
    ws = pl.multiple_of(ws, 256)
    nk = NA_KROWS * GRID_W
    scale = NA_HEAD_DIM ** -0.5
    q = q_ref[0]
    kw = k_ref[0, pl.ds(ws, nk), :]
    vw = v_ref[0, pl.ds(ws, nk), :]
    s_nb = lax.dot_general(q, kw, _NT, preferred_element_type=F32) * scale + bias_ref[0, 0]
    s_c = lax.dot_general(q, kc_ref[0], _NT, preferred_element_type=F32) * scale
    m = jnp.maximum(jnp.max(s_nb, axis=-1, keepdims=True), jnp.max(s_c, axis=-1, keepdims=True))
    p_nb = jnp.exp(s_nb - m)
    p_c = jnp.exp(s_c - m)
    denom = jnp.sum(p_nb, axis=-1, keepdims=True) + jnp.sum(p_c, axis=-1, keepdims=True)
    o = (jnp.dot(p_nb.astype(BF16), vw, preferred_element_type=F32)
         + jnp.dot(p_c.astype(BF16), vc_ref[0], preferred_element_type=F32))
    o_ref[0] = (o / denom).astype(o_ref.dtype)


def _neighbourhood_attention(qkv, qkv_c, rpb):
    b, t, _ = qkv.shape
    tc = qkv_c.shape[1]
    rows = t // GRID_W
    nblk = rows // NA_QROWS
    tq = NA_QROWS * GRID_W
    bias = _na_bias_table(rpb, rows)
    pat = lambda j: jnp.where(j == 0, 0, jnp.where(j == nblk - 1, 2, 1))
    return pl.pallas_call(
        functools.partial(_na_kernel, rows=rows),
        grid=(b, NA_HEADS, nblk),
        in_specs=[
            pl.BlockSpec((1, tq, NA_HEAD_DIM), lambda bi, h, j: (bi, j, h)),
            pl.BlockSpec((1, t, NA_HEAD_DIM), lambda bi, h, j: (bi, 0, NA_HEADS + h)),
            pl.BlockSpec((1, t, NA_HEAD_DIM), lambda bi, h, j: (bi, 0, 2 * NA_HEADS + h)),
            pl.BlockSpec((1, tc, NA_HEAD_DIM), lambda bi, h, j: (bi, 0, NA_HEADS + h)),
            pl.BlockSpec((1, tc, NA_HEAD_DIM), lambda bi, h, j: (bi, 0, 2 * NA_HEADS + h)),
            pl.BlockSpec((1, 1, tq, NA_KROWS * GRID_W), lambda bi, h, j: (pat(j), h, 0, 0)),
        ],
        out_specs=pl.BlockSpec((1, tq, NA_HEAD_DIM), lambda bi, h, j: (bi, j, h)),
        out_shape=jax.ShapeDtypeStruct((b, t, NA_W), BF16),
        compiler_params=_params("arbitrary", "arbitrary", "arbitrary"),
        name="neighbourhood_attention",
    )(qkv, qkv, qkv, qkv_c, qkv_c, bias)


def _ctx_attn_kernel(q_ref, k_ref, v_ref, o_ref):
    scale = NA_HEAD_DIM ** -0.5
    s = lax.dot_general(q_ref[0], k_ref[0], _NT, preferred_element_type=F32) * scale
    p = jnp.exp(s - jnp.max(s, axis=-1, keepdims=True))
    denom = jnp.sum(p, axis=-1, keepdims=True)
    o = jnp.dot(p.astype(BF16), v_ref[0], preferred_element_type=F32)
    o_ref[0] = (o / denom).astype(o_ref.dtype)


def _context_attention(qkv_c):
    b, tc, _ = qkv_c.shape
    blk = lambda off: pl.BlockSpec((1, tc, NA_HEAD_DIM), lambda bi, h: (bi, 0, off + h))
    return pl.pallas_call(
        _ctx_attn_kernel,
        grid=(b, NA_HEADS),
        in_specs=[blk(0), blk(NA_HEADS), blk(2 * NA_HEADS)],
        out_specs=blk(0),
        out_shape=jax.ShapeDtypeStruct((b, tc, NA_W), BF16),
        compiler_params=_params("arbitrary", "arbitrary"),
        name="context_attention",
    )(qkv_c, qkv_c, qkv_c)


def _s5_tables(lam_re, lam_im, log_dt, b_re, b_im, c_re, c_im, d_skip):
    ll, gi, p, g = S5_CHUNK, S5_GROUP, S5_STATE, S5_GROUPS
    f32 = F32
    lam = lax.complex(lam_re.astype(f32), lam_im.astype(f32))
    dt = jnp.exp(log_dt.astype(f32))[..., None]
    lam_dt = lam * dt
    lam_bar = jnp.exp(lam_dt)
    b_bar = ((lam_bar - 1.0) / lam)[..., None] * lax.complex(b_re.astype(f32), b_im.astype(f32))[None]
    c_mat = lax.complex(c_re.astype(f32), c_im.astype(f32))
    taus = jnp.arange(ll + 1, dtype=f32)
    pw = jnp.exp(lam_dt[None] * taus[:, None, None, None])

    def cmul_sum(a, b, spec):
        ar, ai, br, bi = jnp.real(a), jnp.imag(a), jnp.real(b), jnp.imag(b)
        e = lambda x, y: jnp.einsum(spec, x, y, precision=HIGHEST)
        return e(ar, br) - e(ai, bi), e(ar, bi) + e(ai, br)

    cp = c_mat[:, None] * jnp.moveaxis(pw[:ll], 0, 1)[:, :, :, None, :]
    k_re, _ = cmul_sum(cp, b_bar, 'dtgip,dgpj->dtgij')
    idx = np.arange(ll)
    tau_f = idx[None, :] - idx[:, None]
    kf = jnp.where((tau_f >= 0)[:, :, None, None, None], k_re[0][np.clip(tau_f, 0, ll - 1)], 0.0)
    kb = jnp.where((tau_f <= 0)[:, :, None, None, None], k_re[1][np.clip(-tau_f, 0, ll - 1)], 0.0)
    skip = (jnp.eye(ll, dtype=f32)[:, :, None, None, None]
            * (d_skip.astype(f32)[:, :, None] * jnp.eye(gi, dtype=f32))[None, None])
    t_sum = kf + kb + skip
    t_sum = jnp.transpose(t_sum, (2, 0, 4, 1, 3)).reshape(g, ll * gi, ll * gi)

    def pad_state(x):
        return jnp.pad(x, [(0, 0)] * (x.ndim - 1) + [(0, 128 - p)])

    wf = pw[ll - 1 - idx, 0][:, :, :, None] * b_bar[0][None]
    wb = pw[idx, 1][:, :, :, None] * b_bar[1][None]
    to_rows = lambda w: jnp.transpose(w, (1, 0, 3, 2)).reshape(g, ll * gi, p)
    w_blocks = [pad_state(to_rows(f(w))) for w in (wf, wb) for f in (jnp.real, jnp.imag)]
    m_in = jnp.concatenate([t_sum] + w_blocks, axis=-1)

    zf = c_mat[0][None] * pw[idx + 1, 0][:, :, None, :]
    zb = c_mat[1][None] * pw[ll - idx, 1][:, :, None, :]
    to_cols = lambda z: pad_state(jnp.transpose(z, (1, 0, 2, 3)).reshape(g, ll * gi, p))
    z_blocks = [jnp.swapaxes(to_cols(f(z)), 1, 2)
                for z in (zf, zb) for f in (jnp.real, lambda v: -jnp.imag(v))]
    z_out = jnp.concatenate(z_blocks, axis=1)

    lam_l = jnp.stack([pad_state(f(pw[ll, d])) for d in (0, 1) for f in (jnp.real, jnp.imag)], axis=1)
    return m_in, z_out, lam_l


def _s5_kernel(u_ref, m_ref, z_ref, lam_ref, y_ref, r_ref, s_ref, *, nbatch, nchunk, nctx):
    width = S5_CHUNK * S5_GROUP
    r_ref[...] = jnp.dot(u_ref[0], m_ref[0], precision=HIGHEST, preferred_element_type=F32)
    lam = lam_ref[0]
    lfr, lfi, lbr, lbi = lam[0:1], lam[1:2], lam[2:3], lam[3:4]
    sl = lambda i: slice(128 * i, 128 * (i + 1))
    vsl = lambda i: slice(width + 128 * i, width + 128 * (i + 1))

    sub = 8
    ngrp, nctx_g = nchunk // sub, nctx // sub

    def body(kg, carry):
        gb = jnp.where(kg < nctx_g, nctx_g - 1 - kg, ngrp - 1 - (kg - nctx_g))
        out = []
        for bi in range(nbatch):
            fr, fi, br, bim = carry[bi]
            rf = pl.ds(pl.multiple_of(bi * nchunk + kg * sub, sub), sub)
            rb = pl.ds(pl.multiple_of(bi * nchunk + gb * sub, sub), sub)
            vfr, vfi = r_ref[rf, vsl(0)], r_ref[rf, vsl(1)]
            vbr, vbi = r_ref[rb, vsl(2)], r_ref[rb, vsl(3)]
            sfr, sfi, sbr, sbi = [], [], [None] * sub, [None] * sub
            for i in range(sub):
                sfr.append(fr)
                sfi.append(fi)
                fr, fi = (lfr * fr - lfi * fi + vfr[i:i + 1], lfr * fi + lfi * fr + vfi[i:i + 1])
                j = sub - 1 - i
                sbr[j], sbi[j] = br, bim
                br, bim = (lbr * br - lbi * bim + vbr[j:j + 1], lbr * bim + lbi * br + vbi[j:j + 1])
            s_ref[rf, sl(0)] = jnp.concatenate(sfr, axis=0)
            s_ref[rf, sl(1)] = jnp.concatenate(sfi, axis=0)
            s_ref[rb, sl(2)] = jnp.concatenate(sbr, axis=0)
            s_ref[rb, sl(3)] = jnp.concatenate(sbi, axis=0)
            out.append((fr, fi, br, bim))
        return tuple(out)

    zero = jnp.zeros((1, 128), F32)
    lax.fori_loop(0, ngrp, body, tuple((zero, zero, zero, zero) for _ in range(nbatch)))
    y_ref[0] = r_ref[:, 0:width] + jnp.dot(s_ref[...], z_ref[0], precision=HIGHEST,
                                           preferred_element_type=F32)


def _s5_scan(u_cat, tables, nctx_tokens):
    m_in, z_out, lam_l = tables
    b, tt, _ = u_cat.shape
    ll, gi, g = S5_CHUNK, S5_GROUP, S5_GROUPS
    n = tt // ll
    width = ll * gi
    ug = u_cat.reshape(b, n, ll, g, gi).transpose(3, 0, 1, 2, 4).reshape(g, b * n, width)
    y = pl.pallas_call(
        functools.partial(_s5_kernel, nbatch=b, nchunk=n, nctx=nctx_tokens // ll),
        grid=(g,),
        in_specs=[pl.BlockSpec((1, b * n, width), lambda i: (i, 0, 0)),
                  pl.BlockSpec((1, width, width + 512), lambda i: (i, 0, 0)),
                  pl.BlockSpec((1, 512, width), lambda i: (i, 0, 0)),
                  pl.BlockSpec((1, 4, 128), lambda i: (i, 0, 0))],
        out_specs=pl.BlockSpec((1, b * n, width), lambda i: (i, 0, 0)),
        out_shape=jax.ShapeDtypeStruct((g, b * n, width), F32),
        scratch_shapes=[pltpu.VMEM((b * n, width + 512), F32), pltpu.VMEM((b * n, 512), F32)],
        compiler_params=_params("arbitrary"),
        name="s5_scan",
    )(ug, m_in, z_out, lam_l)
    return y.reshape(g, b, n, ll, gi).transpose(1, 2, 3, 0, 4).reshape(b, tt, g * gi)


def _glu_kernel(y_ref, w_ref, b_ref, o_ref, wb_ref):
    @pl.when(pl.program_id(0) == 0)
    def _():
        wb_ref[...] = w_ref[...].astype(BF16)

    z = jax.nn.gelu(y_ref[...])
    a = jnp.dot(z.astype(BF16), wb_ref[...], preferred_element_type=F32) + b_ref[...]
    o_ref[...] = (z * jax.nn.sigmoid(a)).astype(o_ref.dtype)


def _s5_glu(y, w, bias, tm=256):
    m, n = y.shape
    tm = min(tm, m)
    return pl.pallas_call(
        _glu_kernel,
        grid=(m // tm,),
        in_specs=[pl.BlockSpec((tm, n), lambda i: (i, 0)),
                  pl.BlockSpec((n, n), lambda i: (0, 0)),
                  pl.BlockSpec((1, n), lambda i: (0, 0))],
        out_specs=pl.BlockSpec((tm, n), lambda i: (i, 0)),
        out_shape=jax.ShapeDtypeStruct((m, n), BF16),
        scratch_shapes=[pltpu.VMEM((n, n), BF16)],
        compiler_params=_params("arbitrary"),
        name="s5_glu",
    )(y, w, bias.reshape(1, n))


def _gla_state_kernel(f_ref, v_ref, lb_ref, tri_ref, s_ref, *, nchunk):
    d = pl.program_id(2)
    lb = lb_ref[0]
    st = jnp.zeros((HG_DIM, HG_DIM), F32)
    for c in range(nchunk):
        cc = jnp.where(d == 0, c, nchunk - 1 - c)
        sl = pl.ds(pl.multiple_of(cc * HG_CHUNK, HG_CHUNK), HG_CHUNK)
        f = lb + (1.0 - lb) * jax.nn.sigmoid(f_ref[0, sl, :])
        logf = jnp.log(f)
        b = jnp.dot(tri_ref[0], logf, precision=HIGHEST, preferred_element_type=F32)
        tot = jnp.sum(logf, axis=0, keepdims=True)
        kd = ((1.0 - f) * jnp.exp(tot - b)).astype(BF16)
        st = st * jnp.exp(tot) + jnp.dot(v_ref[0, sl, :].T.astype(BF16), kd, preferred_element_type=F32)
    s_ref[0, 0, 0] = st


def _gla_kernel(q_ref, f_ref, v_ref, cos_ref, sin_ref, lb_ref, tri_ref, s0_ref, o_ref, st_ref,
                *, nchunk, reverse):
    @pl.when(pl.program_id(2) == 0)
    def _():
        st_ref[...] = s0_ref[0, 0, 0]

    pair = 2 * HG_CHUNK
    lb = lb_ref[0]
    tri = tri_ref[...]
    keep = tri > 0
    f = lb + (1.0 - lb) * jax.nn.sigmoid(f_ref[0])
    logf = jnp.log(f)
    k = 1.0 - f
    v = v_ref[0]
    sg = jax.nn.sigmoid(q_ref[0])
    lane = lax.broadcasted_iota(jnp.int32, sg.shape, 1)
    quarter = HG_DIM // 4
    partner = jnp.where((lane & quarter) == 0, pltpu.roll(sg, HG_DIM - quarter, 1),
                        pltpu.roll(sg, quarter, 1))
    q = sg * cos_ref[...] + partner * sin_ref[...]
    hi = logf.astype(BF16)
    rem = logf - hi.astype(F32)
    mid = rem.astype(BF16)
    lo = (rem - mid.astype(F32)).astype(BF16)
    st = st_ref[...]
    pairs = range(nchunk // 2)
    for p in (reversed(pairs) if reverse else pairs):
        sl = slice(pair * p, pair * (p + 1))
        b = (jnp.dot(tri, hi[sl], preferred_element_type=F32)
             + jnp.dot(tri, mid[sl], preferred_element_type=F32)
             + jnp.dot(tri, lo[sl], preferred_element_type=F32))
        qe = (q[sl] * jnp.exp(b)).astype(BF16)
        ke = (k[sl] * jnp.exp(-b)).astype(BF16)
        att = jnp.where(keep, lax.dot_general(qe, ke, _NT, preferred_element_type=F32), 0.0)
        o_intra = jnp.dot(att.astype(BF16), v[sl].astype(BF16), preferred_element_type=F32)
        o_inter = [None, None]
        for half in ((1, 0) if reverse else (0, 1)):
            hs = slice(HG_CHUNK * half, HG_CHUNK * (half + 1))
            rows = slice(pair * p + HG_CHUNK * half, pair * p + HG_CHUNK * (half + 1))
            tot = jnp.sum(logf[rows], axis=0, keepdims=True)
            kd = (k[rows] * jnp.exp(tot - b[hs])).astype(BF16)
            o_inter[half] = lax.dot_general(qe[hs], st.astype(BF16), _NT, preferred_element_type=F32)
            st = st * jnp.exp(tot) + jnp.dot(v[rows].T.astype(BF16), kd, preferred_element_type=F32)
        o_ref[0, sl, :] = o_intra + jnp.concatenate(o_inter, axis=0)
    st_ref[...] = st


def _rope_tables(t):
    pos = np.arange(t)
    quarter = HG_DIM // 4
    inv_freq = ROPE_THETA ** (-jnp.arange(quarter, dtype=F32) / quarter)

    def tab(p):
        ang = jnp.asarray(p, F32)[:, None] * inv_freq[None, :]
        c, s = jnp.cos(ang), jnp.sin(ang)
        return jnp.concatenate([c, c], axis=-1), jnp.concatenate([-s, s], axis=-1)

    c_r, s_r = tab(pos // GRID_W)
    c_c, s_c = tab(pos % GRID_W)
    return jnp.concatenate([c_r, c_c], axis=-1), jnp.concatenate([s_r, s_c], axis=-1)


def _gla_masks():
    i = np.arange(HG_CHUNK)
    lower = (i[:, None] >= i[None, :]).astype(np.float32)
    return jnp.asarray(np.stack([lower, lower.T]))


def _hgrn2_gla(proj, proj_c, lb, tblock=1024):
    b, t, _ = proj.shape
    tc = proj_c.shape[1]
    tblock = min(tblock, t)
    nt = t // tblock
    tri = _gla_masks()
    lb3 = lb.astype(F32).reshape(HG_HEADS, 1, HG_DIM)

    s0 = pl.pallas_call(
        functools.partial(_gla_state_kernel, nchunk=tc // HG_CHUNK),
        grid=(b, HG_HEADS, 2),
        in_specs=[pl.BlockSpec((1, tc, HG_DIM), lambda bi, h, d: (bi, 0, HG_HEADS * (1 + d) + h)),
                  pl.BlockSpec((1, tc, HG_DIM), lambda bi, h, d: (bi, 0, 3 * HG_HEADS + h)),
                  pl.BlockSpec((1, 1, HG_DIM), lambda bi, h, d: (h, 0, 0)),
                  pl.BlockSpec((1, HG_CHUNK, HG_CHUNK), lambda bi, h, d: (d, 0, 0))],
        out_specs=pl.BlockSpec((1, 1, 1, HG_DIM, HG_DIM), lambda bi, h, d: (bi, h, d, 0, 0)),
        out_shape=jax.ShapeDtypeStruct((b, HG_HEADS, 2, HG_DIM, HG_DIM), F32),
        compiler_params=_params("arbitrary", "arbitrary", "arbitrary"),
        name="hgrn2_ctx_state",
    )(proj_c, proj_c, lb3, tri)

    cos, sin = _rope_tables(t)
    eye2 = jnp.eye(2, dtype=F32)
    outs = []
    for d in (0, 1):
        tb = (lambda ti: ti) if d == 0 else (lambda ti: nt - 1 - ti)
        tri2 = jnp.kron(eye2, tri[d]).astype(BF16)
        outs.append(pl.pallas_call(
            functools.partial(_gla_kernel, nchunk=tblock // HG_CHUNK, reverse=bool(d)),
            grid=(b, HG_HEADS, nt),
            in_specs=[pl.BlockSpec((1, tblock, HG_DIM), lambda bi, h, ti, tb=tb: (bi, tb(ti), h)),
                      pl.BlockSpec((1, tblock, HG_DIM),
                                   lambda bi, h, ti, tb=tb, d=d: (bi, tb(ti), HG_HEADS * (1 + d) + h)),
                      pl.BlockSpec((1, tblock, HG_DIM),
                                   lambda bi, h, ti, tb=tb: (bi, tb(ti), 3 * HG_HEADS + h)),
                      pl.BlockSpec((tblock, HG_DIM), lambda bi, h, ti, tb=tb: (tb(ti), 0)),
                      pl.BlockSpec((tblock, HG_DIM), lambda bi, h, ti, tb=tb: (tb(ti), 0)),
                      pl.BlockSpec((1, 1, HG_DIM), lambda bi, h, ti: (h, 0, 0)),
                      pl.BlockSpec((2 * HG_CHUNK, 2 * HG_CHUNK), lambda bi, h, ti: (0, 0)),
                      pl.BlockSpec((1, 1, 1, HG_DIM, HG_DIM), lambda bi, h, ti, d=d: (bi, h, d, 0, 0))],
            out_specs=pl.BlockSpec((1, tblock, HG_DIM), lambda bi, h, ti, tb=tb: (bi, tb(ti), h)),
            out_shape=jax.ShapeDtypeStruct((b, t, HG_Q), F32),
            scratch_shapes=[pltpu.VMEM((HG_DIM, HG_DIM), F32)],
            compiler_params=_params("arbitrary", "arbitrary", "arbitrary"),
            name="hgrn2_gla_bwd" if d else "hgrn2_gla_fwd",
        )(proj, proj, proj, cos, sin, lb3, tri2, s0))
    return outs


def _readout_kernel(of_ref, ob_ref, gate_ref, g_ref, o_ref):
    o = of_ref[0] + ob_ref[0]
    ms = jnp.mean(o * o, axis=-1, keepdims=True)
    y = o * lax.rsqrt(ms + NORM_EPS) * g_ref[0]
    gate = gate_ref[0]
    o_ref[0] = (y * (gate * jax.nn.sigmoid(gate))).astype(o_ref.dtype)


def _hgrn2_readout(o_fwd, o_bwd, proj, norm_g, tt=1024):
    b, t, _ = o_fwd.shape
    tt = min(tt, t)
    ospec = lambda: pl.BlockSpec((1, tt, HG_DIM), lambda bi, ti, h: (bi, ti, h))
    return pl.pallas_call(
        _readout_kernel,
        grid=(b, t // tt, HG_HEADS),
        in_specs=[ospec(), ospec(),
                  pl.BlockSpec((1, tt, HG_DIM), lambda bi, ti, h: (bi, ti, 4 * HG_HEADS + h)),
                  pl.BlockSpec((1, 1, HG_DIM), lambda bi, ti, h: (h, 0, 0))],
        out_specs=pl.BlockSpec((1, tt, HG_DIM), lambda bi, ti, h: (bi, ti, h)),
        out_shape=jax.ShapeDtypeStruct((b, t, HG_Q), BF16),
        compiler_params=_params("arbitrary", "arbitrary", "arbitrary"),
        name="hgrn2_readout",
    )(o_fwd, o_bwd, proj, norm_g.astype(F32).reshape(HG_HEADS, 1, HG_DIM))


def _router_kernel(h_ref, r_ref, o_ref):
    logits = lax.dot_general(r_ref[...], h_ref[0].astype(BF16), _NT, preferred_element_type=F32)
    p = jnp.exp(logits - jnp.max(logits, axis=0, keepdims=True))
    o_ref[0] = p / jnp.sum(p, axis=0, keepdims=True)


def _router_affinity(h, router_t, tt=512):
    b, t, d = h.shape
    tt = min(tt, t)
    return pl.pallas_call(
        _router_kernel,
        grid=(b, t // tt),
        in_specs=[pl.BlockSpec((1, tt, d), lambda bi, ti: (bi, ti, 0)),
                  pl.BlockSpec((N_EXPERTS, d), lambda bi, ti: (0, 0))],
        out_specs=pl.BlockSpec((1, N_EXPERTS, tt), lambda bi, ti: (bi, 0, ti)),
        out_shape=jax.ShapeDtypeStruct((b, N_EXPERTS, t), F32),
        compiler_params=_params("arbitrary", "arbitrary"),
        name="moe_router",
    )(h, router_t)


def _select_kernel(aff_ref, idx_ref, gate_ref, pos_ref, *, cap):
    e, r, _ = aff_ref.shape[1:]
    x = aff_ref[0]
    xb = pltpu.bitcast(x, jnp.int32)

    def count(mask):
        ones = jnp.where(mask, 1.0, 0.0)
        return jnp.sum(jnp.sum(ones, axis=1, keepdims=True), axis=2, keepdims=True)

    def search(_, bounds):
        lo, hi = bounds
        mid = lo + lax.shift_right_logical(hi - lo + 1, 1)
        ok = count(xb >= mid) >= cap
        return jnp.where(ok, mid, lo), jnp.where(ok, hi, mid - 1)

    inf_bits = 0x7F800000
    tau, _ = lax.fori_loop(0, 31, search, (jnp.zeros((e, 1, 1), jnp.int32),
                                           jnp.full((e, 1, 1), inf_bits, jnp.int32)))

    ki = lax.broadcasted_iota(jnp.int32, (128, 128), 0)
    li = lax.broadcasted_iota(jnp.int32, (128, 128), 1)
    upper = jnp.where(ki <= li, 1.0, 0.0).astype(BF16)
    ones = jnp.ones((128, 128), BF16)
    ri = lax.broadcasted_iota(jnp.int32, (r, r), 0)
    ci = lax.broadcasted_iota(jnp.int32, (r, r), 1)
    earlier_rows = jnp.where(ci < ri, 1.0, 0.0).astype(BF16)

    def running_count(mask):
        m2 = jnp.where(mask, 1.0, 0.0).reshape(e * r, 128).astype(BF16)
        within = jnp.dot(m2, upper, preferred_element_type=F32)
        row_tot = jnp.dot(m2, ones, preferred_element_type=F32).astype(BF16)
        before = [jnp.dot(earlier_rows, row_tot[i * r:(i + 1) * r], preferred_element_type=F32)
                  for i in range(e)]
        return (within + jnp.concatenate(before, axis=0)).reshape(e, r, 128)

    above = xb > tau
    tied = xb == tau
    need = cap - count(above)
    chosen = jnp.logical_or(above, jnp.logical_and(tied, running_count(tied) <= need))
    pos_ref[...] = jnp.where(chosen, running_count(chosen) - 1.0, -1.0)

    slot = lax.broadcasted_iota(jnp.int32, (cap, 128), 0).astype(F32)
    lane = lax.broadcasted_iota(jnp.int32, (1, 128), 1)
    for ei in range(e):
        def row_group(g, acc, ei=ei):
            rows = pl.ds(pl.multiple_of(g * 8, 8), 8)
            pos8 = pos_ref[ei, rows, :]
            aff8 = aff_ref[0, ei, rows, :]
            for i in range(8):
                onehot = jnp.where(slot == pos8[i:i + 1], 1.0, 0.0).astype(BF16)
                tok = (g * 8 + i) * 128 + lane
                gv = aff8[i:i + 1]
                g_hi = gv.astype(BF16).astype(F32)
                g_mid = (gv - g_hi).astype(BF16).astype(F32)
                g_lo = gv - g_hi - g_mid
                feats = jnp.concatenate(
                    [lax.shift_right_logical(tok, 6).astype(F32), (tok & 63).astype(F32),
                     g_hi, g_mid, g_lo, jnp.zeros((3, 128), F32)], axis=0).astype(BF16)
                acc = acc + lax.dot_general(feats, onehot, _NT, preferred_element_type=F32)
            return acc

        acc = lax.fori_loop(0, r // 8, row_group, jnp.zeros((8, cap), F32))
        idx_ref[0, ei:ei + 1, :] = (acc[0:1] * 64.0 + acc[1:2]).astype(jnp.int32)
        gate_ref[0, ei:ei + 1, :] = acc[2:3] + acc[3:4] + acc[4:5]


def _moe_select(aff, cap):
    b, e, t = aff.shape
    tpad = max(t, 1024)
    if tpad != t:
        aff = jnp.pad(aff, ((0, 0), (0, 0), (0, tpad - t)), constant_values=-1.0)
    r = tpad // 128
    return pl.pallas_call(
        functools.partial(_select_kernel, cap=cap),
        grid=(b,),
        in_specs=[pl.BlockSpec((1, e, r, 128), lambda bi: (bi, 0, 0, 0))],
        out_specs=[pl.BlockSpec((1, e, cap), lambda bi: (bi, 0, 0)),
                   pl.BlockSpec((1, e, cap), lambda bi: (bi, 0, 0))],
        out_shape=[jax.ShapeDtypeStruct((b, e, cap), jnp.int32),
                   jax.ShapeDtypeStruct((b, e, cap), F32)],
        scratch_shapes=[pltpu.VMEM((e, r, 128), F32)],
        compiler_params=_params("arbitrary"),
        name="moe_select",
    )(aff.reshape(b, e, r, 128))


def _dispatch_kernel(idx_ref, h_hbm, o_ref, buf_ref, sem, *, tm, steps_per_batch):
    i = pl.program_id(0)
    nsteps = pl.num_programs(0)

    def row_copy(step, r, slot):
        tok = idx_ref[step * tm + r]
        return pltpu.make_async_copy(h_hbm.at[lax.div(step, steps_per_batch), pl.ds(tok, 1), :],
                                     buf_ref.at[slot, pl.ds(r, 1), :], sem.at[slot])

    def start_rows(step):
        slot = lax.rem(step, 2)

        def body(r, carry):
            row_copy(step, r, slot).start()
            return carry

        lax.fori_loop(0, tm, body, 0, unroll=8)

    @pl.when(i == 0)
    def _():
        start_rows(i)

    @pl.when(i + 1 < nsteps)
    def _():
        start_rows(i + 1)

    slot = lax.rem(i, 2)

    def wait_row(r, carry):
        row_copy(i, r, slot).wait()
        return carry

    lax.fori_loop(0, tm, wait_row, 0, unroll=8)
    o_ref[...] = buf_ref[slot].astype(o_ref.dtype)


def _moe_dispatch(h, idx, tm=256):
    b, t, d = h.shape
    _, e, cap = idx.shape
    tm = min(tm, cap)
    nsteps = b * e * cap // tm
    out = pl.pallas_call(
        functools.partial(_dispatch_kernel, tm=tm, steps_per_batch=e * cap // tm),
        grid_spec=pltpu.PrefetchScalarGridSpec(
            num_scalar_prefetch=1,
            grid=(nsteps,),
            in_specs=[pl.BlockSpec(memory_space=pl.ANY)],
            out_specs=pl.BlockSpec((tm, d), lambda i, idx_ref: (i, 0)),
            scratch_shapes=[pltpu.VMEM((2, tm, d), F32), pltpu.SemaphoreType.DMA((2,))],
        ),
        out_shape=jax.ShapeDtypeStruct((b * e * cap, d), BF16),
        compiler_params=_params("arbitrary"),
        name="moe_dispatch",
    )(idx.reshape(-1), h)
    return out.reshape(b, e, cap, d)


def _first_row_tile():
    return jnp.logical_and(pl.program_id(2) == 0, pl.program_id(3) == 0)


def _expert_up_kernel(x_ref, w1_ref, w3_ref, o_ref, w1b_ref, w3b_ref):
    @pl.when(_first_row_tile())
    def _():
        w1b_ref[...] = w1_ref[...].astype(BF16)
        w3b_ref[...] = w3_ref[...].astype(BF16)

    x = x_ref[0, 0]
    a = jnp.dot(x, w1b_ref[...], preferred_element_type=F32)
    g = jnp.dot(x, w3b_ref[...], preferred_element_type=F32)
    o_ref[0, 0] = (a * jax.nn.sigmoid(a) * g).astype(o_ref.dtype)


def _expert_down_kernel(idx_ref, h_ref, w2_ref, gate_ref, acc_hbm, out_hbm, w2b_ref, rows_ref,
                        gather_sem, scatter_sem, *, tm, ot, cap):
    del acc_hbm
    ei, ni, bi, mi = (pl.program_id(a) for a in range(4))
    nn, nb, nm = (pl.num_programs(a) for a in (1, 2, 3))
    q = (ni * nb + bi) * nm + mi
    slot = lax.rem(q, 2)
    base = (bi * pl.num_programs(0) + ei) * cap + mi * tm
    col = pl.multiple_of(ni * ot, ot)

    @pl.when(_first_row_tile())
    def _():
        w2b_ref[...] = w2_ref[...].astype(BF16)

    def gather(r, s):
        return pltpu.make_async_copy(out_hbm.at[bi, pl.ds(idx_ref[base + r], 1), pl.ds(col, ot)],
                                     rows_ref.at[s, pl.ds(r, 1), :], gather_sem.at[s])

    def scatter(r, s):
        return pltpu.make_async_copy(rows_ref.at[s, pl.ds(r, 1), :],
                                     out_hbm.at[bi, pl.ds(idx_ref[base + r], 1), pl.ds(col, ot)],
                                     scatter_sem.at[s])

    def for_rows(fn):
        def body(r, carry):
            fn(r)
            return carry
        lax.fori_loop(0, tm, body, 0, unroll=8)

    @pl.when(q >= 2)
    def _():
        for_rows(lambda r: scatter(r, slot).wait())

    for_rows(lambda r: gather(r, slot).start())
    y = jnp.dot(h_ref[0, 0], w2b_ref[...], preferred_element_type=F32) * gate_ref[0, 0]
    for_rows(lambda r: gather(r, slot).wait())
    rows_ref[slot] = rows_ref[slot] + y
    for_rows(lambda r: scatter(r, slot).start())

    @pl.when(q == nn * nb * nm - 1)
    def _():
        for_rows(lambda r: scatter(r, 1 - slot).wait())
        for_rows(lambda r: scatter(r, slot).wait())


def _expert_ffn(xin, gate, idx, w1, w3, w2, layer, t, tm=512):
    b, e, cap, d = xin.shape
    ff = w1.shape[-1]
    tm = min(tm, cap)
    ft, ot = min(MOE_FF_TILE, ff), min(MOE_OUT_TILE, d)
    assert ((d // ot) * b * (cap // tm)) % 2 == 0
    hid = pl.pallas_call(
        _expert_up_kernel,
        grid=(e, ff // ft, b, cap // tm),
        in_specs=[pl.BlockSpec((1, 1, tm, d), lambda ei, fi, bi, mi: (bi, ei, mi, 0)),
                  pl.BlockSpec((None, None, d, ft), lambda ei, fi, bi, mi: (layer, ei, 0, fi)),
                  pl.BlockSpec((None, None, d, ft), lambda ei, fi, bi, mi: (layer, ei, 0, fi))],
        out_specs=pl.BlockSpec((1, 1, tm, ft), lambda ei, fi, bi, mi: (bi, ei, mi, fi)),
        out_shape=jax.ShapeDtypeStruct((b, e, cap, ff), BF16),
        scratch_shapes=[pltpu.VMEM((d, ft), BF16), pltpu.VMEM((d, ft), BF16)],
        compiler_params=_params("arbitrary", "arbitrary", "arbitrary", "arbitrary"),
        name="moe_expert_up",
    )(xin, w1, w3)
    return pl.pallas_call(
        functools.partial(_expert_down_kernel, tm=tm, ot=ot, cap=cap),
        grid_spec=pltpu.PrefetchScalarGridSpec(
            num_scalar_prefetch=1,
            grid=(e, d // ot, b, cap // tm),
            in_specs=[pl.BlockSpec((1, 1, tm, ff), lambda ei, ni, bi, mi, idx_ref: (bi, ei, mi, 0)),
                      pl.BlockSpec((None, None, ff, ot), lambda ei, ni, bi, mi, idx_ref: (layer, ei, 0, ni)),
                      pl.BlockSpec((1, 1, tm, 1), lambda ei, ni, bi, mi, idx_ref: (bi, ei, mi, 0)),
                      pl.BlockSpec(memory_space=pl.ANY)],
            out_specs=pl.BlockSpec(memory_space=pl.ANY),
            scratch_shapes=[pltpu.VMEM((ff, ot), BF16), pltpu.VMEM((2, tm, ot), F32),
                            pltpu.SemaphoreType.DMA((2,)), pltpu.SemaphoreType.DMA((2,))],
        ),
        out_shape=jax.ShapeDtypeStruct((b, t, d), F32),
        input_output_aliases={4: 0},
        compiler_params=_params("arbitrary", "arbitrary", "arbitrary", "arbitrary"),
        name="moe_expert_down",
    )(idx.reshape(-1), hid, w2, gate[..., None], jnp.zeros((b, t, d), F32))


def _expert_choice_moe(h, router_t, w1, w3, w2, layer):
    b, t, d = h.shape
    cap = EC_CAPACITY_FACTOR * t // N_EXPERTS
    aff = _router_affinity(h, router_t)
    idx, gate = _moe_select(aff, cap)
    idx = jnp.clip(idx, 0, t - 1)
    xin = _moe_dispatch(h, idx)
    return _expert_ffn(xin, gate, idx, w1, w3, w2, layer, t)


def _project(h, w, out_dtype, name, col0=0, n=None):
    b, t, d = h.shape
    out = _matmul(h.reshape(b * t, d), w, 0, out_dtype, name, col0=col0, n=n)
    return out.reshape(b, t, out.shape[1])


def _mixer_na_s5(h, hc, w_in, w_out, rpb, s5_tables, glu_w, glu_b):
    b, t, _ = h.shape
    tc = hc.shape[1]
    qkv = _project(h, w_in, BF16, "ab_in_qkv", 0, 3 * NA_W)
    qkv_c = _project(hc, w_in, BF16, "ab_in_qkv_ctx", 0, 3 * NA_W)
    u = _project(h, w_in, F32, "ab_in_u", 3 * NA_W, S5_WIDTH)
    u_c = _project(hc, w_in, F32, "ab_in_u_ctx", 3 * NA_W, S5_WIDTH)
    o_na = _neighbourhood_attention(qkv, qkv_c, rpb)
    oc_na = _context_attention(qkv_c)
    y_s5 = _s5_scan(jnp.concatenate([u_c, u], axis=1), s5_tables, tc)
    z = _s5_glu(y_s5.reshape(b * (tc + t), S5_WIDTH), glu_w, glu_b).reshape(b, tc + t, S5_WIDTH)
    mixed = jnp.concatenate([o_na, z[:, tc:]], axis=-1)
    mixed_c = jnp.concatenate([oc_na, z[:, :tc]], axis=-1)
    return _project(mixed, w_out, F32, "ab_out"), _project(mixed_c, w_out, F32, "ab_out_ctx")


def _mixer_hgrn2(h, hc, w_in, w_out, lb, norm_g):
    proj = _project(h, w_in, F32, "hg_in")
    proj_c = _project(hc, w_in, F32, "hg_in_ctx")
    o_fwd, o_bwd = _hgrn2_gla(proj, proj_c, lb)
    return _project(_hgrn2_readout(o_fwd, o_bwd, proj, norm_g), w_out, F32, "hg_out")


def kernel(x, c, ctx, c_ctx, ada_w, ada_b, norm_g, ab_w_in, ab_w_out, na_rpb, s5_lam_re, s5_lam_im,
           s5_log_dt, s5_b_re, s5_b_im, s5_c_re, s5_c_im, s5_d, s5_glu_w, s5_glu_b, hg_w_in, hg_w_out,
           hg_lb_logits, hg_norm_g, moe_router, moe_w1, moe_w3, moe_w2):
    bsz, _, d = x.shape
    depth = ada_w.shape[0]
    assert depth == 2 and bsz + 1 <= 8
    assert ab_w_in.shape[0] == 1 and hg_w_in.shape[0] == 1
    xc = ctx
    cond = jax.nn.silu(c)
    cond_ctx = jax.nn.silu(c_ctx)[None]
    cond3 = jnp.concatenate([cond, cond_ctx, jnp.zeros((8 - bsz - 1, d), F32)], axis=0)
    hg_lb = jnp.cumsum(jax.nn.softmax(hg_lb_logits.astype(F32), axis=0), axis=0)
    moe_w = (moe_w1, moe_w3, moe_w2)
    for layer in range(depth):
        need_ctx = layer < depth - 1
        ada = _matmul(cond3, ada_w, layer, F32, "adaln") + ada_b[layer][None]
        sh1, sc1, g1, sh2, sc2, g2 = jnp.split(ada[:bsz], 6, axis=-1)
        csh1, csc1, cg1, csh2, csc2, cg2 = [jnp.broadcast_to(a, (bsz, d))
                                            for a in jnp.split(ada[bsz:bsz + 1], 6, axis=-1)]
        ng = norm_g[layer].astype(F32)
        h = _modulate(x, ng[0], sh1, sc1)
        hc = _modulate(xc, ng[0], csh1, csc1)
        if layer == 0:
            tables = _s5_tables(s5_lam_re[0], s5_lam_im[0], s5_log_dt[0], s5_b_re[0], s5_b_im[0],
                                s5_c_re[0], s5_c_im[0], s5_d[0])
            y, yc = _mixer_na_s5(h, hc, ab_w_in, ab_w_out, na_rpb[0], tables, s5_glu_w[0],
                                 s5_glu_b[0].astype(F32))
        else:
            y = _mixer_hgrn2(h, hc, hg_w_in, hg_w_out, hg_lb[0], hg_norm_g[0])
            yc = None
        x = _gated_residual(x, y, ng[1], g1)
        router_t = moe_router[layer].T.astype(BF16)
        h2 = _modulate(x, ng[2], sh2, sc2, F32)
        x = _gated_residual(x, _expert_choice_moe(h2, router_t, *moe_w, layer), ng[3], g2)
        if need_ctx:
            xc = _gated_residual(xc, yc, ng[1], cg1)
            hc2 = _modulate(xc, ng[2], csh2, csc2, F32)
            xc = _gated_residual(xc, _expert_choice_moe(hc2, router_t, *moe_w, layer), ng[3], cg2)
    return x
```

```python
import functools
import math

import numpy as np
import jax
import jax.numpy as jnp
from jax import lax
from jax.experimental import pallas as pl
from jax.experimental.pallas import tpu as pltpu

F32 = jnp.float32
BF16 = jnp.bfloat16
HIGHEST = lax.Precision.HIGHEST

D_MODEL = 4096
GRID_W = 64
NORM_EPS = 1e-6
NA_HEADS = 16
NA_HEAD_DIM = 128
NA_WIN_R = 8
NA_WIN_C = 16
NA_W = NA_HEADS * NA_HEAD_DIM
S5_WIDTH = 2048
S5_GROUP = 16
S5_GROUPS = S5_WIDTH // S5_GROUP
S5_STATE = 64
S5_CHUNK = 16
HG_HEADS = 32
HG_DIM = 128
HG_CHUNK = 64
HG_Q = HG_HEADS * HG_DIM
ROPE_THETA = 10000.0
N_EXPERTS = 16
EXPERT_FF = 1024
EC_CAPACITY_FACTOR = 2

VMEM_LIMIT_BYTES = 56 * 1024 * 1024
MOE_FF_TILE = 512
MOE_OUT_TILE = 4096
NA_QROWS = 8
NA_KROWS = 16
NEG_BIG = -1e30

_NT = (((1,), (1,)), ((), ()))


def _params(*sem):
    return pltpu.CompilerParams(dimension_semantics=sem, vmem_limit_bytes=VMEM_LIMIT_BYTES)


def _mm_kernel(a_ref, w_ref, o_ref, wb_ref):
    @pl.when(pl.program_id(1) == 0)
    def _():
        wb_ref[...] = w_ref[...].astype(BF16)

    o_ref[...] = jnp.dot(a_ref[...].astype(BF16), wb_ref[...],
                         preferred_element_type=F32).astype(o_ref.dtype)


def _matmul(a, w, layer, out_dtype, name, col0=0, n=None, tm=1024, tn=512):
    m, k = a.shape
    n = w.shape[2] - col0 if n is None else n
    tm = min(tm, m)
    tn = min(tn, n)
    assert m % tm == 0 and n % tn == 0 and col0 % tn == 0, (m, n, col0, tm, tn)
    cb = col0 // tn
    return pl.pallas_call(
        _mm_kernel,
        grid=(n // tn, m // tm),
        in_specs=[pl.BlockSpec((tm, k), lambda j, i: (i, 0)),
                  pl.BlockSpec((None, k, tn), lambda j, i: (layer, 0, cb + j))],
        out_specs=pl.BlockSpec((tm, tn), lambda j, i: (i, j)),
        out_shape=jax.ShapeDtypeStruct((m, n), out_dtype),
        scratch_shapes=[pltpu.VMEM((k, tn), BF16)],
        compiler_params=_params("arbitrary", "arbitrary"),
        name=name,
    )(a, w)


def _modulate_kernel(x_ref, g_ref, sh_ref, sc_ref, o_ref):
    x = x_ref[0]
    ms = jnp.mean(x * x, axis=-1, keepdims=True)
    y = x * lax.rsqrt(ms + NORM_EPS) * g_ref[...]
    o_ref[0] = (y * (1.0 + sc_ref[0]) + sh_ref[0]).astype(o_ref.dtype)


def _modulate(x, g, shift, scale, out_dtype=BF16, tt=256):
    b, t, d = x.shape
    tt = min(tt, t)
    row = lambda bi, ti: (bi, 0, 0)
    return pl.pallas_call(
        _modulate_kernel,
        grid=(b, t // tt),
        in_specs=[pl.BlockSpec((1, tt, d), lambda bi, ti: (bi, ti, 0)),
                  pl.BlockSpec((1, d), lambda bi, ti: (0, 0)),
                  pl.BlockSpec((1, 1, d), row),
                  pl.BlockSpec((1, 1, d), row)],
        out_specs=pl.BlockSpec((1, tt, d), lambda bi, ti: (bi, ti, 0)),
        out_shape=jax.ShapeDtypeStruct((b, t, d), out_dtype),
        compiler_params=_params("arbitrary", "arbitrary"),
        name="modulate",
    )(x, g.reshape(1, d), shift.reshape(b, 1, d), scale.reshape(b, 1, d))


def _residual_kernel(x_ref, y_ref, g_ref, gate_ref, o_ref):
    y = y_ref[0].astype(F32)
    ms = jnp.mean(y * y, axis=-1, keepdims=True)
    o_ref[0] = x_ref[0] + gate_ref[0] * (y * lax.rsqrt(ms + NORM_EPS) * g_ref[...])


def _gated_residual(x, y, g, gate, tt=256):
    b, t, d = x.shape
    tt = min(tt, t)
    blk = pl.BlockSpec((1, tt, d), lambda bi, ti: (bi, ti, 0))
    return pl.pallas_call(
        _residual_kernel,
        grid=(b, t // tt),
        in_specs=[blk, blk,
                  pl.BlockSpec((1, d), lambda bi, ti: (0, 0)),
                  pl.BlockSpec((1, 1, d), lambda bi, ti: (bi, 0, 0))],
        out_specs=blk,
        out_shape=jax.ShapeDtypeStruct((b, t, d), F32),
        compiler_params=_params("arbitrary", "arbitrary"),
        name="gated_residual",
    )(x, y, g.reshape(1, d), gate.reshape(b, 1, d))


def _residual_modulate_kernel(x_ref, y_ref, g_ref, gate_ref, g2_ref, sh_ref, sc_ref, xo_ref, ho_ref):
    y = y_ref[0].astype(F32)
    ms = jnp.mean(y * y, axis=-1, keepdims=True)
    x = x_ref[0] + gate_ref[0] * (y * lax.rsqrt(ms + NORM_EPS) * g_ref[...])
    xo_ref[0] = x
    ms2 = jnp.mean(x * x, axis=-1, keepdims=True)
    h = x * lax.rsqrt(ms2 + NORM_EPS) * g2_ref[...]
    ho_ref[0] = (h * (1.0 + sc_ref[0]) + sh_ref[0]).astype(ho_ref.dtype)


def _gated_residual_modulate(x, y, g, gate, g2, shift, scale, out_dtype, tt=256):
    b, t, d = x.shape
    tt = min(tt, t)
    blk = pl.BlockSpec((1, tt, d), lambda bi, ti: (bi, ti, 0))
    vec = pl.BlockSpec((1, d), lambda bi, ti: (0, 0))
    row = pl.BlockSpec((1, 1, d), lambda bi, ti: (bi, 0, 0))
    per_row = lambda a: a.reshape(b, 1, d)
    return pl.pallas_call(
        _residual_modulate_kernel,
        grid=(b, t // tt),
        in_specs=[blk, blk, vec, row, vec, row, row],
        out_specs=[blk, blk],
        out_shape=[jax.ShapeDtypeStruct((b, t, d), F32), jax.ShapeDtypeStruct((b, t, d), out_dtype)],
        compiler_params=_params("arbitrary", "arbitrary"),
        name="gated_residual_modulate",
    )(x, y, g.reshape(1, d), per_row(gate), g2.reshape(1, d), per_row(shift), per_row(scale))


def _na_bias_table(rpb, rows):
    nblk = rows // NA_QROWS
    qc = np.arange(GRID_W)
    cs = np.clip(qc - NA_WIN_C // 2, 0, GRID_W - NA_WIN_C)
    kc = np.arange(GRID_W)
    in_c = (kc[None, :] >= cs[:, None]) & (kc[None, :] < cs[:, None] + NA_WIN_C)
    rel_c = kc[None, :] - qc[:, None] + NA_WIN_C - 1
    sel_c = (rel_c[:, :, None] == np.arange(2 * NA_WIN_C - 1)).astype(np.float32)
    tables = []
    for j in (0, 1, nblk - 1):
        qr = NA_QROWS * j + np.arange(NA_QROWS)
        ws = int(np.clip(NA_QROWS * j - NA_WIN_R // 2, 0, rows - NA_KROWS))
        kr = ws + np.arange(NA_KROWS)
        r0 = np.clip(qr - NA_WIN_R // 2, 0, rows - NA_WIN_R)
        in_r = (kr[None, :] >= r0[:, None]) & (kr[None, :] < r0[:, None] + NA_WIN_R)
        rel_r = kr[None, :] - qr[:, None] + NA_WIN_R - 1
        sel_r = (rel_r[:, :, None] == np.arange(2 * NA_WIN_R - 1)).astype(np.float32)
        valid = in_r[:, None, :, None] & in_c[None, :, None, :]
        by_row = jnp.einsum('hrc,qkr->hqkc', rpb.astype(F32), sel_r, precision=HIGHEST)
        bias = jnp.einsum('hqkc,xyc->hqxky', by_row, sel_c, precision=HIGHEST)
        bias = jnp.where(valid[None], bias, NEG_BIG)
        tables.append(bias.reshape(rpb.shape[0], NA_QROWS * GRID_W, NA_KROWS * GRID_W))
    return jnp.stack(tables)


def _na_kernel(q_ref, k_ref, v_ref, kc_ref, vc_ref, bias_ref, o_ref, *, rows):
    j = pl.program_id(2)
    ws = jnp.clip(NA_QROWS * j - NA_WIN_R // 2, 0, rows - NA_KROWS) * GRID_W
    ws = pl.multiple_of(ws, 256)
    nk = NA_KROWS * GRID_W
    scale = NA_HEAD_DIM ** -0.5
    q = q_ref[0]
    kw = k_ref[0, pl.ds(ws, nk), :]
    vw = v_ref[0, pl.ds(ws, nk), :]
    s_nb = lax.dot_general(q, kw, _NT, preferred_element_type=F32) * scale + bias_ref[0, 0]
    s_c = lax.dot_general(q, kc_ref[0], _NT, preferred_element_type=F32) * scale
    m = jnp.maximum(jnp.max(s_nb, axis=-1, keepdims=True), jnp.max(s_c, axis=-1, keepdims=True))
    p_nb = jnp.exp(s_nb - m)
    p_c = jnp.exp(s_c - m)
    denom = jnp.sum(p_nb, axis=-1, keepdims=True) + jnp.sum(p_c, axis=-1, keepdims=True)
    o = (jnp.dot(p_nb.astype(BF16), vw, preferred_element_type=F32)
         + jnp.dot(p_c.astype(BF16), vc_ref[0], preferred_element_type=F32))
    o_ref[0] = (o / denom).astype(o_ref.dtype)


def _neighbourhood_attention(qkv, qkv_c, rpb):
    b, t, _ = qkv.shape
    tc = qkv_c.shape[1]
    rows = t // GRID_W
    nblk = rows // NA_QROWS
    tq = NA_QROWS * GRID_W
    bias = _na_bias_table(rpb, rows)
    pat = lambda j: jnp.where(j == 0, 0, jnp.where(j == nblk - 1, 2, 1))
    return pl.pallas_call(
        functools.partial(_na_kernel, rows=rows),
        grid=(b, NA_HEADS, nblk),
        in_specs=[
            pl.BlockSpec((1, tq, NA_HEAD_DIM), lambda bi, h, j: (bi, j, h)),
            pl.BlockSpec((1, t, NA_HEAD_DIM), lambda bi, h, j: (bi, 0, NA_HEADS + h)),
            pl.BlockSpec((1, t, NA_HEAD_DIM), lambda bi, h, j: (bi, 0, 2 * NA_HEADS + h)),
            pl.BlockSpec((1, tc, NA_HEAD_DIM), lambda bi, h, j: (bi, 0, NA_HEADS + h)),
            pl.BlockSpec((1, tc, NA_HEAD_DIM), lambda bi, h, j: (bi, 0, 2 * NA_HEADS + h)),
            pl.BlockSpec((1, 1, tq, NA_KROWS * GRID_W), lambda bi, h, j: (pat(j), h, 0, 0)),
        ],
        out_specs=pl.BlockSpec((1, tq, NA_HEAD_DIM), lambda bi, h, j: (bi, j, h)),
        out_shape=jax.ShapeDtypeStruct((b, t, NA_W), BF16),
        compiler_params=_params("arbitrary", "arbitrary", "arbitrary"),
        name="neighbourhood_attention",
    )(qkv, qkv, qkv, qkv_c, qkv_c, bias)


def _ctx_attn_kernel(q_ref, k_ref, v_ref, o_ref):
    scale = NA_HEAD_DIM ** -0.5
    s = lax.dot_general(q_ref[0], k_ref[0], _NT, preferred_element_type=F32) * scale
    p = jnp.exp(s - jnp.max(s, axis=-1, keepdims=True))
    denom = jnp.sum(p, axis=-1, keepdims=True)
    o = jnp.dot(p.astype(BF16), v_ref[0], preferred_element_type=F32)
    o_ref[0] = (o / denom).astype(o_ref.dtype)


def _context_attention(qkv_c):
    b, tc, _ = qkv_c.shape
    blk = lambda off: pl.BlockSpec((1, tc, NA_HEAD_DIM), lambda bi, h: (bi, 0, off + h))
    return pl.pallas_call(
        _ctx_attn_kernel,
        grid=(b, NA_HEADS),
        in_specs=[blk(0), blk(NA_HEADS), blk(2 * NA_HEADS)],
        out_specs=blk(0),
        out_shape=jax.ShapeDtypeStruct((b, tc, NA_W), BF16),
        compiler_params=_params("arbitrary", "arbitrary"),
        name="context_attention",
    )(qkv_c, qkv_c, qkv_c)


def _s5_tables(lam_re, lam_im, log_dt, b_re, b_im, c_re, c_im, d_skip):
    ll, gi, p, g = S5_CHUNK, S5_GROUP, S5_STATE, S5_GROUPS
    f32 = F32
    lam = lax.complex(lam_re.astype(f32), lam_im.astype(f32))
    dt = jnp.exp(log_dt.astype(f32))[..., None]
    lam_dt = lam * dt
    lam_bar = jnp.exp(lam_dt)
    b_bar = ((lam_bar - 1.0) / lam)[..., None] * lax.complex(b_re.astype(f32), b_im.astype(f32))[None]
    c_mat = lax.complex(c_re.astype(f32), c_im.astype(f32))
    taus = jnp.arange(ll + 1, dtype=f32)
    pw = jnp.exp(lam_dt[None] * taus[:, None, None, None])

    def cmul_sum(a, b, spec):
        ar, ai, br, bi = jnp.real(a), jnp.imag(a), jnp.real(b), jnp.imag(b)
        e = lambda x, y: jnp.einsum(spec, x, y, precision=HIGHEST)
        return e(ar, br) - e(ai, bi), e(ar, bi) + e(ai, br)

    cp = c_mat[:, None] * jnp.moveaxis(pw[:ll], 0, 1)[:, :, :, None, :]
    k_re, _ = cmul_sum(cp, b_bar, 'dtgip,dgpj->dtgij')
    idx = np.arange(ll)
    tau_f = idx[None, :] - idx[:, None]
    kf = jnp.where((tau_f >= 0)[:, :, None, None, None], k_re[0][np.clip(tau_f, 0, ll - 1)], 0.0)
    kb = jnp.where((tau_f <= 0)[:, :, None, None, None], k_re[1][np.clip(-tau_f, 0, ll - 1)], 0.0)
    skip = (jnp.eye(ll, dtype=f32)[:, :, None, None, None]
            * (d_skip.astype(f32)[:, :, None] * jnp.eye(gi, dtype=f32))[None, None])
    t_sum = kf + kb + skip
    t_sum = jnp.transpose(t_sum, (2, 0, 4, 1, 3)).reshape(g, ll * gi, ll * gi)

    def pad_state(x):
        return jnp.pad(x, [(0, 0)] * (x.ndim - 1) + [(0, 128 - p)])

    wf = pw[ll - 1 - idx, 0][:, :, :, None] * b_bar[0][None]
    wb = pw[idx, 1][:, :, :, None] * b_bar[1][None]
    to_rows = lambda w: jnp.transpose(w, (1, 0, 3, 2)).reshape(g, ll * gi, p)
    w_blocks = [pad_state(to_rows(f(w))) for w in (wf, wb) for f in (jnp.real, jnp.imag)]
    m_in = jnp.concatenate([t_sum] + w_blocks, axis=-1)

    zf = c_mat[0][None] * pw[idx + 1, 0][:, :, None, :]
    zb = c_mat[1][None] * pw[ll - idx, 1][:, :, None, :]
    to_cols = lambda z: pad_state(jnp.transpose(z, (1, 0, 2, 3)).reshape(g, ll * gi, p))
    z_blocks = [jnp.swapaxes(to_cols(f(z)), 1, 2)
                for z in (zf, zb) for f in (jnp.real, lambda v: -jnp.imag(v))]
    z_out = jnp.concatenate(z_blocks, axis=1)

    lam_l = jnp.stack([pad_state(f(pw[ll, d])) for d in (0, 1) for f in (jnp.real, jnp.imag)], axis=1)
    return m_in, z_out, lam_l


def _split_bf16(x):
    hi = x.astype(BF16)
    return hi, (x - hi.astype(F32)).astype(BF16)


def _dot3(a, b_hi, b_lo):
    a_hi, a_lo = _split_bf16(a)
    d = lambda p, q: jnp.dot(p, q, preferred_element_type=F32)
    return d(a_hi, b_hi) + d(a_lo, b_hi) + d(a_hi, b_lo)


def _s5_kernel(u_ref, mh_ref, ml_ref, zh_ref, zl_ref, lam_ref, y_ref, ug_ref, r_ref, s_ref,
               *, nbatch, nchunk, nctx):
    ll, gi = S5_CHUNK, S5_GROUP
    width = ll * gi
    per_vreg = 128 // gi
    g = pl.program_id(1)
    lane = lax.broadcasted_iota(jnp.int32, (nchunk, 128), 1)

    for bi in range(nbatch):
        cols = []
        for half in range(ll // per_vreg):
            acc = None
            for tt in range(per_vreg):
                ut = u_ref[bi, pl.ds(half * per_vreg + tt, nchunk, stride=ll), :]
                rot = pltpu.roll(ut, jnp.mod(gi * (tt - g), 128), 1)
                keep = jnp.logical_and(lane >= gi * tt, lane < gi * (tt + 1))
                acc = rot if acc is None else jnp.where(keep, rot, acc)
            cols.append(acc)
        ug_ref[bi * nchunk:(bi + 1) * nchunk, :] = jnp.concatenate(cols, axis=1)

    r_ref[...] = _dot3(ug_ref[...], mh_ref[0], ml_ref[0])
    lam = lam_ref[0]
    lfr, lfi, lbr, lbi = lam[0:1], lam[1:2], lam[2:3], lam[3:4]
    sl = lambda i: slice(128 * i, 128 * (i + 1))
    vsl = lambda i: slice(width + 128 * i, width + 128 * (i + 1))

    sub = 8
    ngrp, nctx_g = nchunk // sub, nctx // sub

    def body(kg, carry):
        gb = jnp.where(kg < nctx_g, nctx_g - 1 - kg, ngrp - 1 - (kg - nctx_g))
        out = []
        for bi in range(nbatch):
            fr, fi, br, bim = carry[bi]
            rf = pl.ds(pl.multiple_of(bi * nchunk + kg * sub, sub), sub)
            rb = pl.ds(pl.multiple_of(bi * nchunk + gb * sub, sub), sub)
            vfr, vfi = r_ref[rf, vsl(0)], r_ref[rf, vsl(1)]
            vbr, vbi = r_ref[rb, vsl(2)], r_ref[rb, vsl(3)]
            sfr, sfi, sbr, sbi = [], [], [None] * sub, [None] * sub
            for i in range(sub):
                sfr.append(fr)
                sfi.append(fi)
                fr, fi = (lfr * fr - lfi * fi + vfr[i:i + 1], lfr * fi + lfi * fr + vfi[i:i + 1])
                j = sub - 1 - i
                sbr[j], sbi[j] = br, bim
                br, bim = (lbr * br - lbi * bim + vbr[j:j + 1], lbr * bim + lbi * br + vbi[j:j + 1])
            s_ref[rf, sl(0)] = jnp.concatenate(sfr, axis=0)
            s_ref[rf, sl(1)] = jnp.concatenate(sfi, axis=0)
            s_ref[rb, sl(2)] = jnp.concatenate(sbr, axis=0)
            s_ref[rb, sl(3)] = jnp.concatenate(sbi, axis=0)
            out.append((fr, fi, br, bim))
        return tuple(out)

    zero = jnp.zeros((1, 128), F32)
    lax.fori_loop(0, ngrp, body, tuple((zero, zero, zero, zero) for _ in range(nbatch)))
    yg = r_ref[:, 0:width] + _dot3(s_ref[...], zh_ref[0], zl_ref[0])

    @pl.when(g == 0)
    def _():
        y_ref[...] = jnp.zeros(y_ref.shape, F32)

    mine = jnp.logical_and(lane >= gi * g, lane < gi * (g + 1))
    for bi in range(nbatch):
        for t in range(ll):
            half, tt = divmod(t, per_vreg)
            piece = yg[bi * nchunk:(bi + 1) * nchunk, 128 * half:128 * (half + 1)]
            rot = pltpu.roll(piece, jnp.mod(gi * (g - tt), 128), 1)
            rows = pl.ds(t, nchunk, stride=ll)
            y_ref[bi, rows, :] = jnp.where(mine, rot, y_ref[bi, rows, :])


def _s5_scan(u_cat, tables, nctx_tokens):
    m_in, z_out, lam_l = tables
    b, tt, _ = u_cat.shape
    ll, gi, g = S5_CHUNK, S5_GROUP, S5_GROUPS
    n = tt // ll
    width = ll * gi
    per_slab = 128 // gi
    m_hi, m_lo = _split_bf16(m_in)
    z_hi, z_lo = _split_bf16(z_out)
    wspec = lambda shape: pl.BlockSpec((1,) + shape, lambda s, j: (s * per_slab + j, 0, 0))
    return pl.pallas_call(
        functools.partial(_s5_kernel, nbatch=b, nchunk=n, nctx=nctx_tokens // ll),
        grid=(g // per_slab, per_slab),
        in_specs=[pl.BlockSpec((b, tt, 128), lambda s, j: (0, 0, s)),
                  wspec((width, width + 512)), wspec((width, width + 512)),
                  wspec((512, width)), wspec((512, width)), wspec((4, 128))],
        out_specs=pl.BlockSpec((b, tt, 128), lambda s, j: (0, 0, s)),
        out_shape=jax.ShapeDtypeStruct((b, tt, g * gi), F32),
        scratch_shapes=[pltpu.VMEM((b * n, width), F32), pltpu.VMEM((b * n, width + 512), F32),
                        pltpu.VMEM((b * n, 512), F32)],
        compiler_params=_params("arbitrary", "arbitrary"),
        name="s5_scan",
    )(u_cat, m_hi, m_lo, z_hi, z_lo, lam_l)


def _glu_kernel(y_ref, w_ref, b_ref, o_ref, wb_ref):
    @pl.when(pl.program_id(0) == 0)
    def _():
        wb_ref[...] = w_ref[...].astype(BF16)

    z = jax.nn.gelu(y_ref[...])
    a = jnp.dot(z.astype(BF16), wb_ref[...], preferred_element_type=F32) + b_ref[...]
    o_ref[...] = (z * jax.nn.sigmoid(a)).astype(o_ref.dtype)


def _s5_glu(y, w, bias, tm=256):
    m, n = y.shape
    tm = min(tm, m)
    return pl.pallas_call(
        _glu_kernel,
        grid=(m // tm,),
        in_specs=[pl.BlockSpec((tm, n), lambda i: (i, 0)),
                  pl.BlockSpec((n, n), lambda i: (0, 0)),
                  pl.BlockSpec((1, n), lambda i: (0, 0))],
        out_specs=pl.BlockSpec((tm, n), lambda i: (i, 0)),
        out_shape=jax.ShapeDtypeStruct((m, n), BF16),
        scratch_shapes=[pltpu.VMEM((n, n), BF16)],
        compiler_params=_params("arbitrary"),
        name="s5_glu",
    )(y, w, bias.reshape(1, n))


def _gla_state_kernel(f_ref, v_ref, lb_ref, tri_ref, s_ref, *, nchunk):
    d = pl.program_id(2)
    lb = lb_ref[0]
    st = jnp.zeros((HG_DIM, HG_DIM), F32)
    for c in range(nchunk):
        cc = jnp.where(d == 0, c, nchunk - 1 - c)
        sl = pl.ds(pl.multiple_of(cc * HG_CHUNK, HG_CHUNK), HG_CHUNK)
        f = lb + (1.0 - lb) * jax.nn.sigmoid(f_ref[0, sl, :])
        logf = jnp.log(f)
        b = jnp.dot(tri_ref[0], logf, precision=HIGHEST, preferred_element_type=F32)
        tot = jnp.sum(logf, axis=0, keepdims=True)
        kd = ((1.0 - f) * jnp.exp(tot - b)).astype(BF16)
        st = st * jnp.exp(tot) + jnp.dot(v_ref[0, sl, :].T.astype(BF16), kd, preferred_element_type=F32)
    s_ref[0, 0, 0] = st


def _gla_kernel(q_ref, f_ref, v_ref, cos_ref, sin_ref, lb_ref, tri_ref, s0_ref, *rest, nchunk, reverse):
    if reverse:
        ofwd_ref, gate_ref, ng_ref, o_ref, st_ref = rest
    else:
        o_ref, st_ref = rest

    @pl.when(pl.program_id(2) == 0)
    def _():
        st_ref[...] = s0_ref[0, 0, 0]

    pair = 2 * HG_CHUNK
    lb = lb_ref[0]
    tri = tri_ref[...]
    keep = tri > 0
    f = lb + (1.0 - lb) * jax.nn.sigmoid(f_ref[0])
    logf = jnp.log(f)
    k = 1.0 - f
    v = v_ref[0]
    sg = jax.nn.sigmoid(q_ref[0])
    lane = lax.broadcasted_iota(jnp.int32, sg.shape, 1)
    quarter = HG_DIM // 4
    partner = jnp.where((lane & quarter) == 0, pltpu.roll(sg, HG_DIM - quarter, 1),
                        pltpu.roll(sg, quarter, 1))
    q = sg * cos_ref[...] + partner * sin_ref[...]
    hi = logf.astype(BF16)
    rem = logf - hi.astype(F32)
    mid = rem.astype(BF16)
    lo = (rem - mid.astype(F32)).astype(BF16)
    st = st_ref[...]
    pairs = range(nchunk // 2)
    for p in (reversed(pairs) if reverse else pairs):
        sl = slice(pair * p, pair * (p + 1))
        b = (jnp.dot(tri, hi[sl], preferred_element_type=F32)
             + jnp.dot(tri, mid[sl], preferred_element_type=F32)
             + jnp.dot(tri, lo[sl], preferred_element_type=F32))
        qe = (q[sl] * jnp.exp(b)).astype(BF16)
        ke = (k[sl] * jnp.exp(-b)).astype(BF16)
        att = jnp.where(keep, lax.dot_general(qe, ke, _NT, preferred_element_type=F32), 0.0)
        o_intra = jnp.dot(att.astype(BF16), v[sl].astype(BF16), preferred_element_type=F32)
        o_inter = [None, None]
        for half in ((1, 0) if reverse else (0, 1)):
            hs = slice(HG_CHUNK * half, HG_CHUNK * (half + 1))
            rows = slice(pair * p + HG_CHUNK * half, pair * p + HG_CHUNK * (half + 1))
            tot = jnp.sum(logf[rows], axis=0, keepdims=True)
            kd = (k[rows] * jnp.exp(tot - b[hs])).astype(BF16)
            o_inter[half] = lax.dot_general(qe[hs], st.astype(BF16), _NT, preferred_element_type=F32)
            st = st * jnp.exp(tot) + jnp.dot(v[rows].T.astype(BF16), kd, preferred_element_type=F32)
        o = o_intra + jnp.concatenate(o_inter, axis=0)
        if reverse:
            o = o + ofwd_ref[0, sl, :]
            ms = jnp.mean(o * o, axis=-1, keepdims=True)
            gate = gate_ref[0, sl, :]
            o = o * lax.rsqrt(ms + NORM_EPS) * ng_ref[0] * (gate * jax.nn.sigmoid(gate))
        o_ref[0, sl, :] = o.astype(o_ref.dtype)
    st_ref[...] = st


def _rope_tables(t):
    pos = np.arange(t)
    quarter = HG_DIM // 4
    inv_freq = np.float32(ROPE_THETA) ** (-np.arange(quarter, dtype=np.float32) / np.float32(quarter))

    def tab(p):
        ang = p.astype(np.float32)[:, None] * inv_freq[None, :]
        c, s = np.cos(ang, dtype=np.float32), np.sin(ang, dtype=np.float32)
        return np.concatenate([c, c], axis=-1), np.concatenate([-s, s], axis=-1)

    c_r, s_r = tab(pos // GRID_W)
    c_c, s_c = tab(pos % GRID_W)
    return (jnp.asarray(np.concatenate([c_r, c_c], axis=-1)),
            jnp.asarray(np.concatenate([s_r, s_c], axis=-1)))


def _gla_masks():
    i = np.arange(HG_CHUNK)
    lower = (i[:, None] >= i[None, :]).astype(np.float32)
    return jnp.asarray(np.stack([lower, lower.T]))


def _hgrn2_gla(proj, proj_c, lb, norm_g, tblock=1024):
    b, t, _ = proj.shape
    tc = proj_c.shape[1]
    tblock = min(tblock, t)
    nt = t // tblock
    tri = _gla_masks()
    lb3 = lb.astype(F32).reshape(HG_HEADS, 1, HG_DIM)

    s0 = pl.pallas_call(
        functools.partial(_gla_state_kernel, nchunk=tc // HG_CHUNK),
        grid=(b, HG_HEADS, 2),
        in_specs=[pl.BlockSpec((1, tc, HG_DIM), lambda bi, h, d: (bi, 0, HG_HEADS * (1 + d) + h)),
                  pl.BlockSpec((1, tc, HG_DIM), lambda bi, h, d: (bi, 0, 3 * HG_HEADS + h)),
                  pl.BlockSpec((1, 1, HG_DIM), lambda bi, h, d: (h, 0, 0)),
                  pl.BlockSpec((1, HG_CHUNK, HG_CHUNK), lambda bi, h, d: (d, 0, 0))],
        out_specs=pl.BlockSpec((1, 1, 1, HG_DIM, HG_DIM), lambda bi, h, d: (bi, h, d, 0, 0)),
        out_shape=jax.ShapeDtypeStruct((b, HG_HEADS, 2, HG_DIM, HG_DIM), F32),
        compiler_params=_params("arbitrary", "arbitrary", "arbitrary"),
        name="hgrn2_ctx_state",
    )(proj_c, proj_c, lb3, tri)

    cos, sin = _rope_tables(t)
    eye2 = jnp.eye(2, dtype=F32)
    ng3 = norm_g.astype(F32).reshape(HG_HEADS, 1, HG_DIM)
    out = None
    for d in (0, 1):
        tb = (lambda ti: ti) if d == 0 else (lambda ti: nt - 1 - ti)
        tri2 = jnp.kron(eye2, tri[d]).astype(BF16)
        col = lambda off, tb=tb: pl.BlockSpec((1, tblock, HG_DIM), lambda bi, h, ti: (bi, tb(ti), off + h))
        per_head = lambda: pl.BlockSpec((1, 1, HG_DIM), lambda bi, h, ti: (h, 0, 0))
        in_specs = [col(0), col(HG_HEADS * (1 + d)), col(3 * HG_HEADS),
                    pl.BlockSpec((tblock, HG_DIM), lambda bi, h, ti, tb=tb: (tb(ti), 0)),
                    pl.BlockSpec((tblock, HG_DIM), lambda bi, h, ti, tb=tb: (tb(ti), 0)),
                    per_head(),
                    pl.BlockSpec((2 * HG_CHUNK, 2 * HG_CHUNK), lambda bi, h, ti: (0, 0)),
                    pl.BlockSpec((1, 1, 1, HG_DIM, HG_DIM), lambda bi, h, ti, d=d: (bi, h, d, 0, 0))]
        args = [proj, proj, proj, cos, sin, lb3, tri2, s0]
        if d:
            in_specs += [col(0), col(4 * HG_HEADS), per_head()]
            args += [out, proj, ng3]
        out = pl.pallas_call(
            functools.partial(_gla_kernel, nchunk=tblock // HG_CHUNK, reverse=bool(d)),
            grid=(b, HG_HEADS, nt),
            in_specs=in_specs,
            out_specs=col(0),
            out_shape=jax.ShapeDtypeStruct((b, t, HG_Q), BF16 if d else F32),
            scratch_shapes=[pltpu.VMEM((HG_DIM, HG_DIM), F32)],
            compiler_params=_params("arbitrary", "arbitrary", "arbitrary"),
            name="hgrn2_gla_bwd" if d else "hgrn2_gla_fwd",
        )(*args)
    return out


def _router_kernel(h_ref, r_ref, o_ref):
    logits = lax.dot_general(r_ref[...], h_ref[0].astype(BF16), _NT, preferred_element_type=F32)
    p = jnp.exp(logits - jnp.max(logits, axis=0, keepdims=True))
    o_ref[0] = p / jnp.sum(p, axis=0, keepdims=True)


def _router_affinity(h, router_t, tt=512):
    b, t, d = h.shape
    tt = min(tt, t)
    return pl.pallas_call(
        _router_kernel,
        grid=(b, t // tt),
        in_specs=[pl.BlockSpec((1, tt, d), lambda bi, ti: (bi, ti, 0)),
                  pl.BlockSpec((N_EXPERTS, d), lambda bi, ti: (0, 0))],
        out_specs=pl.BlockSpec((1, N_EXPERTS, tt), lambda bi, ti: (bi, 0, ti)),
        out_shape=jax.ShapeDtypeStruct((b, N_EXPERTS, t), F32),
        compiler_params=_params("arbitrary", "arbitrary"),
        name="moe_router",
    )(h, router_t)


def _select_kernel(aff_ref, idx_ref, gate_ref, pos_ref, *, cap):
    e, r, _ = aff_ref.shape[1:]
    x = aff_ref[0]
    xb = pltpu.bitcast(x, jnp.int32)

    def count(mask):
        ones = jnp.where(mask, 1.0, 0.0)
        return jnp.sum(jnp.sum(ones, axis=1, keepdims=True), axis=2, keepdims=True)

    def search(_, bounds):
        lo, hi = bounds
        mid = lo + lax.shift_right_logical(hi - lo + 1, 1)
        ok = count(xb >= mid) >= cap
        return jnp.where(ok, mid, lo), jnp.where(ok, hi, mid - 1)

    inf_bits = 0x7F800000
    tau, _ = lax.fori_loop(0, 31, search, (jnp.zeros((e, 1, 1), jnp.int32),
                                           jnp.full((e, 1, 1), inf_bits, jnp.int32)))

    ki = lax.broadcasted_iota(jnp.int32, (128, 128), 0)
    li = lax.broadcasted_iota(jnp.int32, (128, 128), 1)
    upper = jnp.where(ki <= li, 1.0, 0.0).astype(BF16)
    ones = jnp.ones((128, 128), BF16)
    ri = lax.broadcasted_iota(jnp.int32, (r, r), 0)
    ci = lax.broadcasted_iota(jnp.int32, (r, r), 1)
    earlier_rows = jnp.where(ci < ri, 1.0, 0.0).astype(BF16)

    def running_count(mask):
        m2 = jnp.where(mask, 1.0, 0.0).reshape(e * r, 128).astype(BF16)
        within = jnp.dot(m2, upper, preferred_element_type=F32)
        row_tot = jnp.dot(m2, ones, preferred_element_type=F32).astype(BF16)
        before = [jnp.dot(earlier_rows, row_tot[i * r:(i + 1) * r], preferred_element_type=F32)
                  for i in range(e)]
        return (within + jnp.concatenate(before, axis=0)).reshape(e, r, 128)

    above = xb > tau
    tied = xb == tau
    need = cap - count(above)
    chosen = jnp.logical_or(above, jnp.logical_and(tied, running_count(tied) <= need))
    pos_ref[...] = jnp.where(chosen, running_count(chosen) - 1.0, -1.0)

    slot = lax.broadcasted_iota(jnp.int32, (cap, 128), 0).astype(F32)
    lane = lax.broadcasted_iota(jnp.int32, (1, 128), 1)
    for ei in range(e):
        def row_group(g, acc, ei=ei):
            rows = pl.ds(pl.multiple_of(g * 8, 8), 8)
            pos8 = pos_ref[ei, rows, :]
            aff8 = aff_ref[0, ei, rows, :]
            for i in range(8):
                onehot = jnp.where(slot == pos8[i:i + 1], 1.0, 0.0).astype(BF16)
                tok = (g * 8 + i) * 128 + lane
                gv = aff8[i:i + 1]
                g_hi = gv.astype(BF16).astype(F32)
                g_mid = (gv - g_hi).astype(BF16).astype(F32)
                g_lo = gv - g_hi - g_mid
                feats = jnp.concatenate(
                    [lax.shift_right_logical(tok, 6).astype(F32), (tok & 63).astype(F32),
                     g_hi, g_mid, g_lo, jnp.zeros((3, 128), F32)], axis=0).astype(BF16)
                acc = acc + lax.dot_general(feats, onehot, _NT, preferred_element_type=F32)
            return acc

        acc = lax.fori_loop(0, r // 8, row_group, jnp.zeros((8, cap), F32))
        idx_ref[0, ei:ei + 1, :] = (acc[0:1] * 64.0 + acc[1:2]).astype(jnp.int32)
        gate_ref[0, ei:ei + 1, :] = acc[2:3] + acc[3:4] + acc[4:5]


def _moe_select(aff, cap):
    b, e, t = aff.shape
    tpad = max(t, 1024)
    if tpad != t:
        aff = jnp.pad(aff, ((0, 0), (0, 0), (0, tpad - t)), constant_values=-1.0)
    r = tpad // 128
    return pl.pallas_call(
        functools.partial(_select_kernel, cap=cap),
        grid=(b,),
        in_specs=[pl.BlockSpec((1, e, r, 128), lambda bi: (bi, 0, 0, 0))],
        out_specs=[pl.BlockSpec((1, e, cap), lambda bi: (bi, 0, 0)),
                   pl.BlockSpec((1, e, cap), lambda bi: (bi, 0, 0))],
        out_shape=[jax.ShapeDtypeStruct((b, e, cap), jnp.int32),
                   jax.ShapeDtypeStruct((b, e, cap), F32)],
        scratch_shapes=[pltpu.VMEM((e, r, 128), F32)],
        compiler_params=_params("arbitrary"),
        name="moe_select",
    )(aff.reshape(b, e, r, 128))


def _dispatch_kernel(idx_ref, h_hbm, o_ref, buf_ref, sem, *, tm, steps_per_batch):
    i = pl.program_id(0)
    nsteps = pl.num_programs(0)

    def row_copy(step, r, slot):
        tok = idx_ref[step * tm + r]
        return pltpu.make_async_copy(h_hbm.at[lax.div(step, steps_per_batch), pl.ds(tok, 1), :],
                                     buf_ref.at[slot, pl.ds(r, 1), :], sem.at[slot])

    def start_rows(step):
        slot = lax.rem(step, 2)

        def body(r8, carry):
            for j in range(8):
                row_copy(step, r8 * 8 + j, slot).start(priority=j % 2)
            return carry

        lax.fori_loop(0, tm // 8, body, 0)

    @pl.when(i == 0)
    def _():
        start_rows(i)

    @pl.when(i + 1 < nsteps)
    def _():
        start_rows(i + 1)

    slot = lax.rem(i, 2)

    def wait_row(r, carry):
        row_copy(i, r, slot).wait()
        return carry

    lax.fori_loop(0, tm, wait_row, 0, unroll=8)
    o_ref[...] = buf_ref[slot].astype(o_ref.dtype)


def _moe_dispatch(h, idx, tm=256):
    b, t, d = h.shape
    _, e, cap = idx.shape
    tm = min(tm, cap)
    nsteps = b * e * cap // tm
    out = pl.pallas_call(
        functools.partial(_dispatch_kernel, tm=tm, steps_per_batch=e * cap // tm),
        grid_spec=pltpu.PrefetchScalarGridSpec(
            num_scalar_prefetch=1,
            grid=(nsteps,),
            in_specs=[pl.BlockSpec(memory_space=pl.ANY)],
            out_specs=pl.BlockSpec((tm, d), lambda i, idx_ref: (i, 0)),
            scratch_shapes=[pltpu.VMEM((2, tm, d), F32), pltpu.SemaphoreType.DMA((2,))],
        ),
        out_shape=jax.ShapeDtypeStruct((b * e * cap, d), BF16),
        compiler_params=_params("arbitrary"),
        name="moe_dispatch",
    )(idx.reshape(-1), h)
    return out.reshape(b, e, cap, d)


def _first_row_tile():
    return jnp.logical_and(pl.program_id(2) == 0, pl.program_id(3) == 0)


def _expert_up_kernel(x_ref, w1_ref, w3_ref, o_ref, w1b_ref, w3b_ref):
    @pl.when(_first_row_tile())
    def _():
        w1b_ref[...] = w1_ref[...].astype(BF16)
        w3b_ref[...] = w3_ref[...].astype(BF16)

    x = x_ref[0, 0]
    a = jnp.dot(x, w1b_ref[...], preferred_element_type=F32)
    g = jnp.dot(x, w3b_ref[...], preferred_element_type=F32)
    o_ref[0, 0] = (a * jax.nn.sigmoid(a) * g).astype(o_ref.dtype)


def _expert_down_kernel(idx_ref, h_ref, w2_ref, gate_ref, acc_hbm, out_hbm, w2b_ref, rows_ref,
                        gather_sem, scatter_sem, *, tm, ot, cap):
    del acc_hbm
    ei, ni, bi, mi = (pl.program_id(a) for a in range(4))
    nn, nb, nm = (pl.num_programs(a) for a in (1, 2, 3))
    q = (ni * nb + bi) * nm + mi
    slot = lax.rem(q, 2)
    base = (bi * pl.num_programs(0) + ei) * cap + mi * tm
    col = pl.multiple_of(ni * ot, ot)

    @pl.when(_first_row_tile())
    def _():
        w2b_ref[...] = w2_ref[...].astype(BF16)

    def gather(r, s):
        return pltpu.make_async_copy(out_hbm.at[bi, pl.ds(idx_ref[base + r], 1), pl.ds(col, ot)],
                                     rows_ref.at[s, pl.ds(r, 1), :], gather_sem.at[s])

    def scatter(r, s):
        return pltpu.make_async_copy(rows_ref.at[s, pl.ds(r, 1), :],
                                     out_hbm.at[bi, pl.ds(idx_ref[base + r], 1), pl.ds(col, ot)],
                                     scatter_sem.at[s])

    def for_rows(fn):
        def body(r8, carry):
            for j in range(8):
                fn(r8 * 8 + j, j % 2)
            return carry
        lax.fori_loop(0, tm // 8, body, 0)

    @pl.when(q >= 2)
    def _():
        for_rows(lambda r, p: scatter(r, slot).wait())

    for_rows(lambda r, p: gather(r, slot).start(priority=p))
    y = jnp.dot(h_ref[0, 0], w2b_ref[...], preferred_element_type=F32) * gate_ref[0, 0]
    for_rows(lambda r, p: gather(r, slot).wait())
    rows_ref[slot] = rows_ref[slot] + y
    for_rows(lambda r, p: scatter(r, slot).start(priority=p))

    @pl.when(q == nn * nb * nm - 1)
    def _():
        for_rows(lambda r, p: scatter(r, 1 - slot).wait())
        for_rows(lambda r, p: scatter(r, slot).wait())


def _expert_ffn(xin, gate, idx, w1, w3, w2, layer, t, tm_up=512, tm=256):
    b, e, cap, d = xin.shape
    ff = w1.shape[-1]
    tm_up, tm = min(tm_up, cap), min(tm, cap)
    ft, ot = min(MOE_FF_TILE, ff), min(MOE_OUT_TILE, d)
    assert ((d // ot) * b * (cap // tm)) % 2 == 0
    hid = pl.pallas_call(
        _expert_up_kernel,
        grid=(e, ff // ft, b, cap // tm_up),
        in_specs=[pl.BlockSpec((1, 1, tm_up, d), lambda ei, fi, bi, mi: (bi, ei, mi, 0)),
                  pl.BlockSpec((None, None, d, ft), lambda ei, fi, bi, mi: (layer, ei, 0, fi)),
                  pl.BlockSpec((None, None, d, ft), lambda ei, fi, bi, mi: (layer, ei, 0, fi))],
        out_specs=pl.BlockSpec((1, 1, tm_up, ft), lambda ei, fi, bi, mi: (bi, ei, mi, fi)),
        out_shape=jax.ShapeDtypeStruct((b, e, cap, ff), BF16),
        scratch_shapes=[pltpu.VMEM((d, ft), BF16), pltpu.VMEM((d, ft), BF16)],
        compiler_params=_params("arbitrary", "arbitrary", "arbitrary", "arbitrary"),
        name="moe_expert_up",
    )(xin, w1, w3)
    return pl.pallas_call(
        functools.partial(_expert_down_kernel, tm=tm, ot=ot, cap=cap),
        grid_spec=pltpu.PrefetchScalarGridSpec(
            num_scalar_prefetch=1,
            grid=(e, d // ot, b, cap // tm),
            in_specs=[pl.BlockSpec((1, 1, tm, ff), lambda ei, ni, bi, mi, idx_ref: (bi, ei, mi, 0)),
                      pl.BlockSpec((None, None, ff, ot), lambda ei, ni, bi, mi, idx_ref: (layer, ei, 0, ni)),
                      pl.BlockSpec((1, 1, tm, 1), lambda ei, ni, bi, mi, idx_ref: (bi, ei, mi, 0)),
                      pl.BlockSpec(memory_space=pl.ANY)],
            out_specs=pl.BlockSpec(memory_space=pl.ANY),
            scratch_shapes=[pltpu.VMEM((ff, ot), BF16), pltpu.VMEM((2, tm, ot), F32),
                            pltpu.SemaphoreType.DMA((2,)), pltpu.SemaphoreType.DMA((2,))],
        ),
        out_shape=jax.ShapeDtypeStruct((b, t, d), F32),
        input_output_aliases={4: 0},
        compiler_params=_params("arbitrary", "arbitrary", "arbitrary", "arbitrary"),
        name="moe_expert_down",
    )(idx.reshape(-1), hid, w2, gate[..., None], jnp.zeros((b, t, d), F32))


def _expert_choice_moe(h, router_t, w1, w3, w2, layer):
    b, t, d = h.shape
    cap = EC_CAPACITY_FACTOR * t // N_EXPERTS
    aff = _router_affinity(h, router_t)
    idx, gate = _moe_select(aff, cap)
    idx = jnp.clip(idx, 0, t - 1)
    xin = _moe_dispatch(h, idx)
    return _expert_ffn(xin, gate, idx, w1, w3, w2, layer, t)


def _project(h, w, out_dtype, name, col0=0, n=None):
    b, t, d = h.shape
    out = _matmul(h.reshape(b * t, d), w, 0, out_dtype, name, col0=col0, n=n)
    return out.reshape(b, t, out.shape[1])


def _mixer_na_s5(h, hc, w_in, w_out, rpb, s5_tables, glu_w, glu_b):
    b, t, _ = h.shape
    tc = hc.shape[1]
    qkv = _project(h, w_in, BF16, "ab_in_qkv", 0, 3 * NA_W)
    qkv_c = _project(hc, w_in, BF16, "ab_in_qkv_ctx", 0, 3 * NA_W)
    u = _project(h, w_in, F32, "ab_in_u", 3 * NA_W, S5_WIDTH)
    u_c = _project(hc, w_in, F32, "ab_in_u_ctx", 3 * NA_W, S5_WIDTH)
    o_na = _neighbourhood_attention(qkv, qkv_c, rpb)
    oc_na = _context_attention(qkv_c)
    y_s5 = _s5_scan(jnp.concatenate([u_c, u], axis=1), s5_tables, tc)
    z = _s5_glu(y_s5.reshape(b * (tc + t), S5_WIDTH), glu_w, glu_b).reshape(b, tc + t, S5_WIDTH)
    mixed = jnp.concatenate([o_na, z[:, tc:]], axis=-1)
    mixed_c = jnp.concatenate([oc_na, z[:, :tc]], axis=-1)
    return _project(mixed, w_out, F32, "ab_out"), _project(mixed_c, w_out, F32, "ab_out_ctx")


def _mixer_hgrn2(h, hc, w_in, w_out, lb, norm_g):
    proj = _project(h, w_in, F32, "hg_in")
    proj_c = _project(hc, w_in, F32, "hg_in_ctx")
    return _project(_hgrn2_gla(proj, proj_c, lb, norm_g), w_out, F32, "hg_out")


def kernel(x, c, ctx, c_ctx, ada_w, ada_b, norm_g, ab_w_in, ab_w_out, na_rpb, s5_lam_re, s5_lam_im,
           s5_log_dt, s5_b_re, s5_b_im, s5_c_re, s5_c_im, s5_d, s5_glu_w, s5_glu_b, hg_w_in, hg_w_out,
           hg_lb_logits, hg_norm_g, moe_router, moe_w1, moe_w3, moe_w2):
    bsz, _, d = x.shape
    depth = ada_w.shape[0]
    assert depth == 2 and bsz + 1 <= 8
    assert ab_w_in.shape[0] == 1 and hg_w_in.shape[0] == 1
    xc = ctx
    cond = jax.nn.silu(c)
    cond_ctx = jax.nn.silu(c_ctx)[None]
    cond3 = jnp.concatenate([cond, cond_ctx, jnp.zeros((8 - bsz - 1, d), F32)], axis=0)
    hg_lb = jnp.cumsum(jax.nn.softmax(hg_lb_logits.astype(F32), axis=0), axis=0)
    moe_w = (moe_w1, moe_w3, moe_w2)
    ada_x, ada_c = [], []
    for layer in range(depth):
        ada = _matmul(cond3, ada_w, layer, F32, "adaln") + ada_b[layer][None]
        ada_x.append(jnp.split(ada[:bsz], 6, axis=-1))
        ada_c.append([jnp.broadcast_to(a, (bsz, d)) for a in jnp.split(ada[bsz:bsz + 1], 6, axis=-1)])
    ng0 = norm_g[0].astype(F32)
    h = _modulate(x, ng0[0], ada_x[0][0], ada_x[0][1])
    hc = _modulate(xc, ng0[0], ada_c[0][0], ada_c[0][1])
    for layer in range(depth):
        need_ctx = layer < depth - 1
        _, _, g1, sh2, sc2, g2 = ada_x[layer]
        _, _, cg1, csh2, csc2, cg2 = ada_c[layer]
        ng = norm_g[layer].astype(F32)
        if layer == 0:
            tables = _s5_tables(s5_lam_re[0], s5_lam_im[0], s5_log_dt[0], s5_b_re[0], s5_b_im[0],
                                s5_c_re[0], s5_c_im[0], s5_d[0])
            y, yc = _mixer_na_s5(h, hc, ab_w_in, ab_w_out, na_rpb[0], tables, s5_glu_w[0],
                                 s5_glu_b[0].astype(F32))
        else:
            y = _mixer_hgrn2(h, hc, hg_w_in, hg_w_out, hg_lb[0], hg_norm_g[0])
            yc = None
        router_t = moe_router[layer].T.astype(BF16)
        x, h2 = _gated_residual_modulate(x, y, ng[1], g1, ng[2], sh2, sc2, F32)
        moe = _expert_choice_moe(h2, router_t, *moe_w, layer)
        if not need_ctx:
            x = _gated_residual(x, moe, ng[3], g2)
            continue
        ng_next = norm_g[layer + 1].astype(F32)
        x, h = _gated_residual_modulate(x, moe, ng[3], g2, ng_next[0], ada_x[layer + 1][0],
                                        ada_x[layer + 1][1], BF16)
        xc, hc2 = _gated_residual_modulate(xc, yc, ng[1], cg1, ng[2], csh2, csc2, F32)
        moe_c = _expert_choice_moe(hc2, router_t, *moe_w, layer)
        xc, hc = _gated_residual_modulate(xc, moe_c, ng[3], cg2, ng_next[0], ada_c[layer + 1][0],
                                          ada_c[layer + 1][1], BF16)
    return x
```

```python
import functools
import math

import numpy as np
import jax
import jax.numpy as jnp
from jax import lax
from jax.experimental import pallas as pl
from jax.experimental.pallas import tpu as pltpu

F32 = jnp.float32
BF16 = jnp.bfloat16
HIGHEST = lax.Precision.HIGHEST

D_MODEL = 4096
GRID_W = 64
NORM_EPS = 1e-6
NA_HEADS = 16
NA_HEAD_DIM = 128
NA_WIN_R = 8
NA_WIN_C = 16
NA_W = NA_HEADS * NA_HEAD_DIM
S5_WIDTH = 2048
S5_GROUP = 16
S5_GROUPS = S5_WIDTH // S5_GROUP
S5_STATE = 64
S5_CHUNK = 16
HG_HEADS = 32
HG_DIM = 128
HG_CHUNK = 64
HG_Q = HG_HEADS * HG_DIM
ROPE_THETA = 10000.0
N_EXPERTS = 16
EXPERT_FF = 1024
EC_CAPACITY_FACTOR = 2

VMEM_LIMIT_BYTES = 56 * 1024 * 1024
MOE_FF_TILE = 512
MOE_OUT_TILE = 4096
NA_QROWS = 8
NA_KROWS = 16
NEG_BIG = -1e30

_NT = (((1,), (1,)), ((), ()))


def _params(*sem):
    return pltpu.CompilerParams(dimension_semantics=sem, vmem_limit_bytes=VMEM_LIMIT_BYTES)


def _mm_kernel(*refs, splits):
    *a_refs, w_ref, o_ref, wb_ref = refs

    @pl.when(pl.program_id(1) == 0)
    def _():
        wb_ref[...] = w_ref[...].astype(BF16)

    acc = None
    for a_ref, (k0, k1) in zip(a_refs, splits):
        part = jnp.dot(a_ref[...].astype(BF16), wb_ref[k0:k1, :], preferred_element_type=F32)
        acc = part if acc is None else acc + part
    o_ref[...] = acc.astype(o_ref.dtype)


def _matmul(a, w, layer, out_dtype, name, col0=0, n=None, tm=1024, tn=512):
    pieces = list(a) if isinstance(a, (list, tuple)) else [a]
    m = pieces[0].shape[0]
    k = w.shape[1]
    bounds = np.cumsum([0] + [piece.shape[1] for piece in pieces])
    assert bounds[-1] == k and all(piece.shape[0] == m for piece in pieces)
    n = w.shape[2] - col0 if n is None else n
    tm = min(tm, m)
    tn = min(tn, n)
    assert m % tm == 0 and n % tn == 0 and col0 % tn == 0, (m, n, col0, tm, tn)
    cb = col0 // tn
    return pl.pallas_call(
        functools.partial(_mm_kernel, splits=tuple(zip(bounds[:-1].tolist(), bounds[1:].tolist()))),
        grid=(n // tn, m // tm),
        in_specs=[pl.BlockSpec((tm, piece.shape[1]), lambda j, i: (i, 0)) for piece in pieces]
        + [pl.BlockSpec((None, k, tn), lambda j, i: (layer, 0, cb + j))],
        out_specs=pl.BlockSpec((tm, tn), lambda j, i: (i, j)),
        out_shape=jax.ShapeDtypeStruct((m, n), out_dtype),
        scratch_shapes=[pltpu.VMEM((k, tn), BF16)],
        compiler_params=_params("arbitrary", "arbitrary"),
        name=name,
    )(*pieces, w)


def _modulate_kernel(x_ref, g_ref, sh_ref, sc_ref, o_ref):
    x = x_ref[0]
    ms = jnp.mean(x * x, axis=-1, keepdims=True)
    y = x * lax.rsqrt(ms + NORM_EPS) * g_ref[...]
    o_ref[0] = (y * (1.0 + sc_ref[0]) + sh_ref[0]).astype(o_ref.dtype)


def _modulate(x, g, shift, scale, out_dtype=BF16, tt=256):
    b, t, d = x.shape
    tt = min(tt, t)
    row = lambda bi, ti: (bi, 0, 0)
    return pl.pallas_call(
        _modulate_kernel,
        grid=(b, t // tt),
        in_specs=[pl.BlockSpec((1, tt, d), lambda bi, ti: (bi, ti, 0)),
                  pl.BlockSpec((1, d), lambda bi, ti: (0, 0)),
                  pl.BlockSpec((1, 1, d), row),
                  pl.BlockSpec((1, 1, d), row)],
        out_specs=pl.BlockSpec((1, tt, d), lambda bi, ti: (bi, ti, 0)),
        out_shape=jax.ShapeDtypeStruct((b, t, d), out_dtype),
        compiler_params=_params("arbitrary", "arbitrary"),
        name="modulate",
    )(x, g.reshape(1, d), shift.reshape(b, 1, d), scale.reshape(b, 1, d))


def _residual_kernel(x_ref, y_ref, g_ref, gate_ref, o_ref):
    y = y_ref[0].astype(F32)
    ms = jnp.mean(y * y, axis=-1, keepdims=True)
    o_ref[0] = x_ref[0] + gate_ref[0] * (y * lax.rsqrt(ms + NORM_EPS) * g_ref[...])


def _gated_residual(x, y, g, gate, tt=256):
    b, t, d = x.shape
    tt = min(tt, t)
    blk = pl.BlockSpec((1, tt, d), lambda bi, ti: (bi, ti, 0))
    return pl.pallas_call(
        _residual_kernel,
        grid=(b, t // tt),
        in_specs=[blk, blk,
                  pl.BlockSpec((1, d), lambda bi, ti: (0, 0)),
                  pl.BlockSpec((1, 1, d), lambda bi, ti: (bi, 0, 0))],
        out_specs=blk,
        out_shape=jax.ShapeDtypeStruct((b, t, d), F32),
        compiler_params=_params("arbitrary", "arbitrary"),
        name="gated_residual",
    )(x, y, g.reshape(1, d), gate.reshape(b, 1, d))


def _residual_modulate_kernel(x_ref, y_ref, g_ref, gate_ref, g2_ref, sh_ref, sc_ref, xo_ref, ho_ref):
    y = y_ref[0].astype(F32)
    ms = jnp.mean(y * y, axis=-1, keepdims=True)
    x = x_ref[0] + gate_ref[0] * (y * lax.rsqrt(ms + NORM_EPS) * g_ref[...])
    xo_ref[0] = x
    ms2 = jnp.mean(x * x, axis=-1, keepdims=True)
    h = x * lax.rsqrt(ms2 + NORM_EPS) * g2_ref[...]
    ho_ref[0] = (h * (1.0 + sc_ref[0]) + sh_ref[0]).astype(ho_ref.dtype)


def _gated_residual_modulate(x, y, g, gate, g2, shift, scale, out_dtype, tt=256):
    b, t, d = x.shape
    tt = min(tt, t)
    blk = pl.BlockSpec((1, tt, d), lambda bi, ti: (bi, ti, 0))
    vec = pl.BlockSpec((1, d), lambda bi, ti: (0, 0))
    row = pl.BlockSpec((1, 1, d), lambda bi, ti: (bi, 0, 0))
    per_row = lambda a: a.reshape(b, 1, d)
    return pl.pallas_call(
        _residual_modulate_kernel,
        grid=(b, t // tt),
        in_specs=[blk, blk, vec, row, vec, row, row],
        out_specs=[blk, blk],
        out_shape=[jax.ShapeDtypeStruct((b, t, d), F32), jax.ShapeDtypeStruct((b, t, d), out_dtype)],
        compiler_params=_params("arbitrary", "arbitrary"),
        name="gated_residual_modulate",
    )(x, y, g.reshape(1, d), per_row(gate), g2.reshape(1, d), per_row(shift), per_row(scale))


def _na_bias_table(rpb, rows):
    nblk = rows // NA_QROWS
    qc = np.arange(GRID_W)
    cs = np.clip(qc - NA_WIN_C // 2, 0, GRID_W - NA_WIN_C)
    kc = np.arange(GRID_W)
    in_c = (kc[None, :] >= cs[:, None]) & (kc[None, :] < cs[:, None] + NA_WIN_C)
    rel_c = kc[None, :] - qc[:, None] + NA_WIN_C - 1
    sel_c = (rel_c[:, :, None] == np.arange(2 * NA_WIN_C - 1)).astype(np.float32)
    tables = []
    for j in (0, 1, nblk - 1):
        qr = NA_QROWS * j + np.arange(NA_QROWS)
        ws = int(np.clip(NA_QROWS * j - NA_WIN_R // 2, 0, rows - NA_KROWS))
        kr = ws + np.arange(NA_KROWS)
        r0 = np.clip(qr - NA_WIN_R // 2, 0, rows - NA_WIN_R)
        in_r = (kr[None, :] >= r0[:, None]) & (kr[None, :] < r0[:, None] + NA_WIN_R)
        rel_r = kr[None, :] - qr[:, None] + NA_WIN_R - 1
        sel_r = (rel_r[:, :, None] == np.arange(2 * NA_WIN_R - 1)).astype(np.float32)
        valid = in_r[:, None, :, None] & in_c[None, :, None, :]
        by_row = jnp.einsum('hrc,qkr->hqkc', rpb.astype(F32), sel_r, precision=HIGHEST)
        bias = jnp.einsum('hqkc,xyc->hqxky', by_row, sel_c, precision=HIGHEST)
        bias = jnp.where(valid[None], bias, NEG_BIG)
        tables.append(bias.reshape(rpb.shape[0], NA_QROWS * GRID_W, NA_KROWS * GRID_W))
    return jnp.stack(tables)


def _na_kernel(q_ref, k_ref, v_ref, kc_ref, vc_ref, bias_ref, o_ref, *, rows):
    j = pl.program_id(2)
    ws = jnp.clip(NA_QROWS * j - NA_WIN_R // 2, 0, rows - NA_KROWS) * GRID_W
    ws = pl.multiple_of(ws, 256)
    nk = NA_KROWS * GRID_W
    scale = NA_HEAD_DIM ** -0.5
    q = q_ref[0]
    kw = k_ref[0, pl.ds(ws, nk), :]
    vw = v_ref[0, pl.ds(ws, nk), :]
    s_nb = lax.dot_general(q, kw, _NT, preferred_element_type=F32) * scale + bias_ref[0, 0]
    s_c = lax.dot_general(q, kc_ref[0], _NT, preferred_element_type=F32) * scale
    m = jnp.maximum(jnp.max(s_nb, axis=-1, keepdims=True), jnp.max(s_c, axis=-1, keepdims=True))
    p_nb = jnp.exp(s_nb - m)
    p_c = jnp.exp(s_c - m)
    denom = jnp.sum(p_nb, axis=-1, keepdims=True) + jnp.sum(p_c, axis=-1, keepdims=True)
    o = (jnp.dot(p_nb.astype(BF16), vw, preferred_element_type=F32)
         + jnp.dot(p_c.astype(BF16), vc_ref[0], preferred_element_type=F32))
    o_ref[0] = (o / denom).astype(o_ref.dtype)


def _neighbourhood_attention(qkv, qkv_c, rpb):
    b, t, _ = qkv.shape
    tc = qkv_c.shape[1]
    rows = t // GRID_W
    nblk = rows // NA_QROWS
    tq = NA_QROWS * GRID_W
    bias = _na_bias_table(rpb, rows)
    pat = lambda j: jnp.where(j == 0, 0, jnp.where(j == nblk - 1, 2, 1))
    return pl.pallas_call(
        functools.partial(_na_kernel, rows=rows),
        grid=(b, NA_HEADS, nblk),
        in_specs=[
            pl.BlockSpec((1, tq, NA_HEAD_DIM), lambda bi, h, j: (bi, j, h)),
            pl.BlockSpec((1, t, NA_HEAD_DIM), lambda bi, h, j: (bi, 0, NA_HEADS + h)),
            pl.BlockSpec((1, t, NA_HEAD_DIM), lambda bi, h, j: (bi, 0, 2 * NA_HEADS + h)),
            pl.BlockSpec((1, tc, NA_HEAD_DIM), lambda bi, h, j: (bi, 0, NA_HEADS + h)),
            pl.BlockSpec((1, tc, NA_HEAD_DIM), lambda bi, h, j: (bi, 0, 2 * NA_HEADS + h)),
            pl.BlockSpec((1, 1, tq, NA_KROWS * GRID_W), lambda bi, h, j: (pat(j), h, 0, 0)),
        ],
        out_specs=pl.BlockSpec((1, tq, NA_HEAD_DIM), lambda bi, h, j: (bi, j, h)),
        out_shape=jax.ShapeDtypeStruct((b, t, NA_W), BF16),
        compiler_params=_params("arbitrary", "arbitrary", "arbitrary"),
        name="neighbourhood_attention",
    )(qkv, qkv, qkv, qkv_c, qkv_c, bias)


def _ctx_attn_kernel(q_ref, k_ref, v_ref, o_ref):
    scale = NA_HEAD_DIM ** -0.5
    s = lax.dot_general(q_ref[0], k_ref[0], _NT, preferred_element_type=F32) * scale
    p = jnp.exp(s - jnp.max(s, axis=-1, keepdims=True))
    denom = jnp.sum(p, axis=-1, keepdims=True)
    o = jnp.dot(p.astype(BF16), v_ref[0], preferred_element_type=F32)
    o_ref[0] = (o / denom).astype(o_ref.dtype)


def _context_attention(qkv_c):
    b, tc, _ = qkv_c.shape
    blk = lambda off: pl.BlockSpec((1, tc, NA_HEAD_DIM), lambda bi, h: (bi, 0, off + h))
    return pl.pallas_call(
        _ctx_attn_kernel,
        grid=(b, NA_HEADS),
        in_specs=[blk(0), blk(NA_HEADS), blk(2 * NA_HEADS)],
        out_specs=blk(0),
        out_shape=jax.ShapeDtypeStruct((b, tc, NA_W), BF16),
        compiler_params=_params("arbitrary", "arbitrary"),
        name="context_attention",
    )(qkv_c, qkv_c, qkv_c)


def _s5_tables(lam_re, lam_im, log_dt, b_re, b_im, c_re, c_im, d_skip):
    ll, gi, p, g = S5_CHUNK, S5_GROUP, S5_STATE, S5_GROUPS
    f32 = F32
    lam = lax.complex(lam_re.astype(f32), lam_im.astype(f32))
    dt = jnp.exp(log_dt.astype(f32))[..., None]
    lam_dt = lam * dt
    lam_bar = jnp.exp(lam_dt)
    b_bar = ((lam_bar - 1.0) / lam)[..., None] * lax.complex(b_re.astype(f32), b_im.astype(f32))[None]
    c_mat = lax.complex(c_re.astype(f32), c_im.astype(f32))
    taus = jnp.arange(ll + 1, dtype=f32)
    pw = jnp.exp(lam_dt[None] * taus[:, None, None, None])

    def cmul_sum(a, b, spec):
        ar, ai, br, bi = jnp.real(a), jnp.imag(a), jnp.real(b), jnp.imag(b)
        e = lambda x, y: jnp.einsum(spec, x, y, precision=HIGHEST)
        return e(ar, br) - e(ai, bi), e(ar, bi) + e(ai, br)

    cp = c_mat[:, None] * jnp.moveaxis(pw[:ll], 0, 1)[:, :, :, None, :]
    k_re, _ = cmul_sum(cp, b_bar, 'dtgip,dgpj->dtgij')
    idx = np.arange(ll)
    tau_f = idx[None, :] - idx[:, None]
    kf = jnp.where((tau_f >= 0)[:, :, None, None, None], k_re[0][np.clip(tau_f, 0, ll - 1)], 0.0)
    kb = jnp.where((tau_f <= 0)[:, :, None, None, None], k_re[1][np.clip(-tau_f, 0, ll - 1)], 0.0)
    skip = (jnp.eye(ll, dtype=f32)[:, :, None, None, None]
            * (d_skip.astype(f32)[:, :, None] * jnp.eye(gi, dtype=f32))[None, None])
    t_sum = kf + kb + skip
    t_sum = jnp.transpose(t_sum, (2, 0, 4, 1, 3)).reshape(g, ll * gi, ll * gi)

    wf = pw[ll - 1 - idx, 0][:, :, :, None] * b_bar[0][None]
    wb = pw[idx, 1][:, :, :, None] * b_bar[1][None]
    to_rows = lambda w: jnp.transpose(w, (1, 0, 3, 2)).reshape(g, ll * gi, p)
    w_blocks = [to_rows(f(w)) for f in (jnp.real, jnp.imag) for w in (wf, wb)]
    m_in = jnp.concatenate([t_sum] + w_blocks, axis=-1)

    zf = c_mat[0][None] * pw[idx + 1, 0][:, :, None, :]
    zb = c_mat[1][None] * pw[ll - idx, 1][:, :, None, :]
    to_cols = lambda z: jnp.swapaxes(jnp.transpose(z, (1, 0, 2, 3)).reshape(g, ll * gi, p), 1, 2)
    z_blocks = [to_cols(f(z)) for f in (jnp.real, lambda v: -jnp.imag(v)) for z in (zf, zb)]
    z_out = jnp.concatenate(z_blocks, axis=1)

    lam_l = jnp.stack([jnp.concatenate([f(pw[ll, 0]), f(pw[ll, 1])], axis=-1)
                       for f in (jnp.real, jnp.imag)], axis=1)
    return m_in, z_out, lam_l


def _split_bf16(x):
    hi = x.astype(BF16)
    return hi, (x - hi.astype(F32)).astype(BF16)


def _dot3(a, b_hi, b_lo):
    a_hi, a_lo = _split_bf16(a)
    d = lambda p, q: jnp.dot(p, q, preferred_element_type=F32)
    return d(a_hi, b_hi) + d(a_lo, b_hi) + d(a_hi, b_lo)


def _s5_kernel(u_ref, uc_ref, mh_ref, ml_ref, zh_ref, zl_ref, lam_ref, y_ref, yc_ref,
               ug_ref, r_ref, s_ref, *, nbatch, nlat, nctx):
    ll, gi, p = S5_CHUNK, S5_GROUP, S5_STATE
    width = ll * gi
    per_vreg = 128 // gi
    nchunk = nctx + nlat
    g = pl.program_id(1)

    def group_rows(src_ref, bi, n):
        lane = lax.broadcasted_iota(jnp.int32, (n, 128), 1)
        cols = []
        for half in range(ll // per_vreg):
            acc = None
            for tt in range(per_vreg):
                ut = src_ref[bi, pl.ds(half * per_vreg + tt, n, stride=ll), :]
                rot = pltpu.roll(ut, jnp.mod(gi * (tt - g), 128), 1)
                keep = jnp.logical_and(lane >= gi * tt, lane < gi * (tt + 1))
                acc = rot if acc is None else jnp.where(keep, rot, acc)
            cols.append(acc)
        return jnp.concatenate(cols, axis=1)

    for bi in range(nbatch):
        ug_ref[bi * nchunk:bi * nchunk + nctx, :] = group_rows(uc_ref, bi, nctx)
        ug_ref[bi * nchunk + nctx:(bi + 1) * nchunk, :] = group_rows(u_ref, bi, nlat)

    r_ref[...] = _dot3(ug_ref[...], mh_ref[0], ml_ref[0])
    lam = lam_ref[0]
    lam_r, lam_i = lam[0:1], lam[1:2]
    fwd_lane = lax.broadcasted_iota(jnp.int32, (1, 2 * p), 1) < p
    v_re, v_im = slice(width, width + 2 * p), slice(width + 2 * p, width + 4 * p)

    sub = 8
    ngrp, nctx_g = nchunk // sub, nctx // sub

    def body(kg, carry):
        gb = jnp.where(kg < nctx_g, nctx_g - 1 - kg, ngrp - 1 - (kg - nctx_g))
        out = []
        for bi in range(nbatch):
            sr, si = carry[bi]
            rf = pl.ds(pl.multiple_of(bi * nchunk + kg * sub, sub), sub)
            rb = pl.ds(pl.multiple_of(bi * nchunk + gb * sub, sub), sub)
            vfr, vfi = r_ref[rf, v_re], r_ref[rf, v_im]
            vbr, vbi = r_ref[rb, v_re], r_ref[rb, v_im]
            before_r, before_i = [], []
            for i in range(sub):
                before_r.append(sr)
                before_i.append(si)
                j = sub - 1 - i
                vr = jnp.where(fwd_lane, vfr[i:i + 1], vbr[j:j + 1])
                vi = jnp.where(fwd_lane, vfi[i:i + 1], vbi[j:j + 1])
                sr, si = lam_r * sr - lam_i * si + vr, lam_r * si + lam_i * sr + vi
            s_ref[rf, 0:p] = jnp.concatenate(before_r, axis=0)[:, 0:p]
            s_ref[rb, p:2 * p] = jnp.concatenate(before_r[::-1], axis=0)[:, p:2 * p]
            s_ref[rf, 2 * p:3 * p] = jnp.concatenate(before_i, axis=0)[:, 0:p]
            s_ref[rb, 3 * p:4 * p] = jnp.concatenate(before_i[::-1], axis=0)[:, p:2 * p]
            out.append((sr, si))
        return tuple(out)

    zero = jnp.zeros((1, 2 * p), F32)
    lax.fori_loop(0, ngrp, body, tuple((zero, zero) for _ in range(nbatch)))
    yg = r_ref[:, 0:width] + _dot3(s_ref[...], zh_ref[0], zl_ref[0])

    @pl.when(g == 0)
    def _():
        y_ref[...] = jnp.zeros(y_ref.shape, F32)
        yc_ref[...] = jnp.zeros(yc_ref.shape, F32)

    def scatter_rows(dst_ref, bi, row0, n):
        lane = lax.broadcasted_iota(jnp.int32, (n, 128), 1)
        mine = jnp.logical_and(lane >= gi * g, lane < gi * (g + 1))
        for t in range(ll):
            half, tt = divmod(t, per_vreg)
            piece = yg[row0:row0 + n, 128 * half:128 * (half + 1)]
            rot = pltpu.roll(piece, jnp.mod(gi * (g - tt), 128), 1)
            rows = pl.ds(t, n, stride=ll)
            dst_ref[bi, rows, :] = jnp.where(mine, rot, dst_ref[bi, rows, :])

    for bi in range(nbatch):
        scatter_rows(yc_ref, bi, bi * nchunk, nctx)
        scatter_rows(y_ref, bi, bi * nchunk + nctx, nlat)


def _s5_scan(u, u_c, tables):
    m_in, z_out, lam_l = tables
    b, t, _ = u.shape
    tc = u_c.shape[1]
    ll, gi, g, p = S5_CHUNK, S5_GROUP, S5_GROUPS, S5_STATE
    nlat, nctx = t // ll, tc // ll
    rows = b * (nlat + nctx)
    width = ll * gi
    per_slab = 128 // gi
    m_hi, m_lo = _split_bf16(m_in)
    z_hi, z_lo = _split_bf16(z_out)
    wspec = lambda shape: pl.BlockSpec((1,) + shape, lambda s, j: (s * per_slab + j, 0, 0))
    slab = lambda n: pl.BlockSpec((b, n, 128), lambda s, j: (0, 0, s))
    return pl.pallas_call(
        functools.partial(_s5_kernel, nbatch=b, nlat=nlat, nctx=nctx),
        grid=(g // per_slab, per_slab),
        in_specs=[slab(t), slab(tc),
                  wspec((width, width + 4 * p)), wspec((width, width + 4 * p)),
                  wspec((4 * p, width)), wspec((4 * p, width)), wspec((2, 2 * p))],
        out_specs=[slab(t), slab(tc)],
        out_shape=[jax.ShapeDtypeStruct((b, t, g * gi), F32), jax.ShapeDtypeStruct((b, tc, g * gi), F32)],
        scratch_shapes=[pltpu.VMEM((rows, width), F32), pltpu.VMEM((rows, width + 4 * p), F32),
                        pltpu.VMEM((rows, 4 * p), F32)],
        compiler_params=_params("arbitrary", "arbitrary"),
        name="s5_scan",
    )(u, u_c, m_hi, m_lo, z_hi, z_lo, lam_l)


def _glu_kernel(y_ref, w_ref, b_ref, o_ref, wb_ref):
    @pl.when(pl.program_id(0) == 0)
    def _():
        wb_ref[...] = w_ref[...].astype(BF16)

    z = jax.nn.gelu(y_ref[...])
    a = jnp.dot(z.astype(BF16), wb_ref[...], preferred_element_type=F32) + b_ref[...]
    o_ref[...] = (z * jax.nn.sigmoid(a)).astype(o_ref.dtype)


def _s5_glu(y, w, bias, tm=256):
    m, n = y.shape
    tm = min(tm, m)
    return pl.pallas_call(
        _glu_kernel,
        grid=(m // tm,),
        in_specs=[pl.BlockSpec((tm, n), lambda i: (i, 0)),
                  pl.BlockSpec((n, n), lambda i: (0, 0)),
                  pl.BlockSpec((1, n), lambda i: (0, 0))],
        out_specs=pl.BlockSpec((tm, n), lambda i: (i, 0)),
        out_shape=jax.ShapeDtypeStruct((m, n), BF16),
        scratch_shapes=[pltpu.VMEM((n, n), BF16)],
        compiler_params=_params("arbitrary"),
        name="s5_glu",
    )(y, w, bias.reshape(1, n))


def _gla_state_kernel(f_ref, v_ref, lb_ref, tri_ref, s_ref, *, nchunk):
    d = pl.program_id(2)
    lb = lb_ref[0]
    st = jnp.zeros((HG_DIM, HG_DIM), F32)
    for c in range(nchunk):
        cc = jnp.where(d == 0, c, nchunk - 1 - c)
        sl = pl.ds(pl.multiple_of(cc * HG_CHUNK, HG_CHUNK), HG_CHUNK)
        f = lb + (1.0 - lb) * jax.nn.sigmoid(f_ref[0, sl, :])
        logf = jnp.log(f)
        b = jnp.dot(tri_ref[0], logf, precision=HIGHEST, preferred_element_type=F32)
        tot = jnp.sum(logf, axis=0, keepdims=True)
        kd = ((1.0 - f) * jnp.exp(tot - b)).astype(BF16)
        st = st * jnp.exp(tot) + jnp.dot(v_ref[0, sl, :].T.astype(BF16), kd, preferred_element_type=F32)
    s_ref[0, 0, 0] = st


def _gla_kernel(q_ref, f_ref, v_ref, cos_ref, sin_ref, lb_ref, tri_ref, s0_ref, *rest, nchunk, reverse):
    if reverse:
        ofwd_ref, gate_ref, ng_ref, o_ref, st_ref = rest
    else:
        o_ref, st_ref = rest

    @pl.when(pl.program_id(2) == 0)
    def _():
        st_ref[...] = s0_ref[0, 0, 0]

    pair = 2 * HG_CHUNK
    lb = lb_ref[0]
    tri = tri_ref[...]
    keep = tri > 0
    f = lb + (1.0 - lb) * jax.nn.sigmoid(f_ref[0])
    logf = jnp.log(f)
    k = 1.0 - f
    v = v_ref[0]
    sg = jax.nn.sigmoid(q_ref[0])
    lane = lax.broadcasted_iota(jnp.int32, sg.shape, 1)
    quarter = HG_DIM // 4
    partner = jnp.where((lane & quarter) == 0, pltpu.roll(sg, HG_DIM - quarter, 1),
                        pltpu.roll(sg, quarter, 1))
    q = sg * cos_ref[...] + partner * sin_ref[...]
    hi = logf.astype(BF16)
    rem = logf - hi.astype(F32)
    mid = rem.astype(BF16)
    lo = (rem - mid.astype(F32)).astype(BF16)
    st = st_ref[...]
    pairs = range(nchunk // 2)
    for p in (reversed(pairs) if reverse else pairs):
        sl = slice(pair * p, pair * (p + 1))
        b = (jnp.dot(tri, hi[sl], preferred_element_type=F32)
             + jnp.dot(tri, mid[sl], preferred_element_type=F32)
             + jnp.dot(tri, lo[sl], preferred_element_type=F32))
        qe = (q[sl] * jnp.exp(b)).astype(BF16)
        ke = (k[sl] * jnp.exp(-b)).astype(BF16)
        att = jnp.where(keep, lax.dot_general(qe, ke, _NT, preferred_element_type=F32), 0.0)
        o_intra = jnp.dot(att.astype(BF16), v[sl].astype(BF16), preferred_element_type=F32)
        o_inter = [None, None]
        for half in ((1, 0) if reverse else (0, 1)):
            hs = slice(HG_CHUNK * half, HG_CHUNK * (half + 1))
            rows = slice(pair * p + HG_CHUNK * half, pair * p + HG_CHUNK * (half + 1))
            tot = jnp.sum(logf[rows], axis=0, keepdims=True)
            kd = (k[rows] * jnp.exp(tot - b[hs])).astype(BF16)
            o_inter[half] = lax.dot_general(qe[hs], st.astype(BF16), _NT, preferred_element_type=F32)
            st = st * jnp.exp(tot) + jnp.dot(v[rows].T.astype(BF16), kd, preferred_element_type=F32)
        o = o_intra + jnp.concatenate(o_inter, axis=0)
        if reverse:
            o = o + ofwd_ref[0, sl, :]
            ms = jnp.mean(o * o, axis=-1, keepdims=True)
            gate = gate_ref[0, sl, :]
            o = o * lax.rsqrt(ms + NORM_EPS) * ng_ref[0] * (gate * jax.nn.sigmoid(gate))
        o_ref[0, sl, :] = o.astype(o_ref.dtype)
    st_ref[...] = st


def _rope_tables(t):
    pos = np.arange(t)
    quarter = HG_DIM // 4
    inv_freq = np.float32(ROPE_THETA) ** (-np.arange(quarter, dtype=np.float32) / np.float32(quarter))

    def tab(p):
        ang = p.astype(np.float32)[:, None] * inv_freq[None, :]
        c, s = np.cos(ang, dtype=np.float32), np.sin(ang, dtype=np.float32)
        return np.concatenate([c, c], axis=-1), np.concatenate([-s, s], axis=-1)

    c_r, s_r = tab(pos // GRID_W)
    c_c, s_c = tab(pos % GRID_W)
    return (jnp.asarray(np.concatenate([c_r, c_c], axis=-1)),
            jnp.asarray(np.concatenate([s_r, s_c], axis=-1)))


def _gla_masks():
    i = np.arange(HG_CHUNK)
    lower = (i[:, None] >= i[None, :]).astype(np.float32)
    return jnp.asarray(np.stack([lower, lower.T]))


def _hgrn2_gla(proj, proj_c, lb, norm_g, tblock=2048):
    b, t, _ = proj.shape
    tc = proj_c.shape[1]
    tblock = min(tblock, t)
    nt = t // tblock
    tri = _gla_masks()
    lb3 = lb.astype(F32).reshape(HG_HEADS, 1, HG_DIM)

    s0 = pl.pallas_call(
        functools.partial(_gla_state_kernel, nchunk=tc // HG_CHUNK),
        grid=(b, HG_HEADS, 2),
        in_specs=[pl.BlockSpec((1, tc, HG_DIM), lambda bi, h, d: (bi, 0, HG_HEADS * (1 + d) + h)),
                  pl.BlockSpec((1, tc, HG_DIM), lambda bi, h, d: (bi, 0, 3 * HG_HEADS + h)),
                  pl.BlockSpec((1, 1, HG_DIM), lambda bi, h, d: (h, 0, 0)),
                  pl.BlockSpec((1, HG_CHUNK, HG_CHUNK), lambda bi, h, d: (d, 0, 0))],
        out_specs=pl.BlockSpec((1, 1, 1, HG_DIM, HG_DIM), lambda bi, h, d: (bi, h, d, 0, 0)),
        out_shape=jax.ShapeDtypeStruct((b, HG_HEADS, 2, HG_DIM, HG_DIM), F32),
        compiler_params=_params("arbitrary", "arbitrary", "arbitrary"),
        name="hgrn2_ctx_state",
    )(proj_c, proj_c, lb3, tri)

    cos, sin = _rope_tables(t)
    eye2 = jnp.eye(2, dtype=F32)
    ng3 = norm_g.astype(F32).reshape(HG_HEADS, 1, HG_DIM)
    out = None
    for d in (0, 1):
        tb = (lambda ti: ti) if d == 0 else (lambda ti: nt - 1 - ti)
        tri2 = jnp.kron(eye2, tri[d]).astype(BF16)
        col = lambda off, tb=tb: pl.BlockSpec((1, tblock, HG_DIM), lambda bi, h, ti: (bi, tb(ti), off + h))
        per_head = lambda: pl.BlockSpec((1, 1, HG_DIM), lambda bi, h, ti: (h, 0, 0))
        in_specs = [col(0), col(HG_HEADS * (1 + d)), col(3 * HG_HEADS),
                    pl.BlockSpec((tblock, HG_DIM), lambda bi, h, ti, tb=tb: (tb(ti), 0)),
                    pl.BlockSpec((tblock, HG_DIM), lambda bi, h, ti, tb=tb: (tb(ti), 0)),
                    per_head(),
                    pl.BlockSpec((2 * HG_CHUNK, 2 * HG_CHUNK), lambda bi, h, ti: (0, 0)),
                    pl.BlockSpec((1, 1, 1, HG_DIM, HG_DIM), lambda bi, h, ti, d=d: (bi, h, d, 0, 0))]
        args = [proj, proj, proj, cos, sin, lb3, tri2, s0]
        if d:
            in_specs += [col(0), col(4 * HG_HEADS), per_head()]
            args += [out, proj, ng3]
        out = pl.pallas_call(
            functools.partial(_gla_kernel, nchunk=tblock // HG_CHUNK, reverse=bool(d)),
            grid=(b, HG_HEADS, nt),
            in_specs=in_specs,
            out_specs=col(0),
            out_shape=jax.ShapeDtypeStruct((b, t, HG_Q), BF16 if d else F32),
            scratch_shapes=[pltpu.VMEM((HG_DIM, HG_DIM), F32)],
            compiler_params=_params("arbitrary", "arbitrary", "arbitrary"),
            name="hgrn2_gla_bwd" if d else "hgrn2_gla_fwd",
        )(*args)
    return out


def _router_kernel(h_ref, r_ref, o_ref):
    logits = lax.dot_general(r_ref[...], h_ref[0].astype(BF16), _NT, preferred_element_type=F32)
    p = jnp.exp(logits - jnp.max(logits, axis=0, keepdims=True))
    o_ref[0] = p / jnp.sum(p, axis=0, keepdims=True)


def _router_affinity(h, router_t, tt=512):
    b, t, d = h.shape
    tt = min(tt, t)
    return pl.pallas_call(
        _router_kernel,
        grid=(b, t // tt),
        in_specs=[pl.BlockSpec((1, tt, d), lambda bi, ti: (bi, ti, 0)),
                  pl.BlockSpec((N_EXPERTS, d), lambda bi, ti: (0, 0))],
        out_specs=pl.BlockSpec((1, N_EXPERTS, tt), lambda bi, ti: (bi, 0, ti)),
        out_shape=jax.ShapeDtypeStruct((b, N_EXPERTS, t), F32),
        compiler_params=_params("arbitrary", "arbitrary"),
        name="moe_router",
    )(h, router_t)


def _select_kernel(aff_ref, idx_ref, gate_ref, pos_ref, *, cap):
    e, r, _ = aff_ref.shape[1:]
    x = aff_ref[0]
    xb = pltpu.bitcast(x, jnp.int32)

    def count(mask):
        ones = jnp.where(mask, 1.0, 0.0)
        return jnp.sum(jnp.sum(ones, axis=1, keepdims=True), axis=2, keepdims=True)

    def search(_, bounds):
        lo, hi = bounds
        mid = lo + lax.shift_right_logical(hi - lo + 1, 1)
        ok = count(xb >= mid) >= cap
        return jnp.where(ok, mid, lo), jnp.where(ok, hi, mid - 1)

    inf_bits = 0x7F800000
    tau, _ = lax.fori_loop(0, 31, search, (jnp.zeros((e, 1, 1), jnp.int32),
                                           jnp.full((e, 1, 1), inf_bits, jnp.int32)))

    ki = lax.broadcasted_iota(jnp.int32, (128, 128), 0)
    li = lax.broadcasted_iota(jnp.int32, (128, 128), 1)
    upper = jnp.where(ki <= li, 1.0, 0.0).astype(BF16)
    ones = jnp.ones((128, 128), BF16)
    ri = lax.broadcasted_iota(jnp.int32, (r, r), 0)
    ci = lax.broadcasted_iota(jnp.int32, (r, r), 1)
    earlier_rows = jnp.where(ci < ri, 1.0, 0.0).astype(BF16)

    def running_count(mask):
        m2 = jnp.where(mask, 1.0, 0.0).reshape(e * r, 128).astype(BF16)
        within = jnp.dot(m2, upper, preferred_element_type=F32)
        row_tot = jnp.dot(m2, ones, preferred_element_type=F32).astype(BF16)
        before = [jnp.dot(earlier_rows, row_tot[i * r:(i + 1) * r], preferred_element_type=F32)
                  for i in range(e)]
        return (within + jnp.concatenate(before, axis=0)).reshape(e, r, 128)

    above = xb > tau
    tied = xb == tau
    need = cap - count(above)
    chosen = jnp.logical_or(above, jnp.logical_and(tied, running_count(tied) <= need))
    pos_ref[...] = jnp.where(chosen, running_count(chosen) - 1.0, -1.0)

    slot = lax.broadcasted_iota(jnp.int32, (cap, 128), 0).astype(F32)
    lane = lax.broadcasted_iota(jnp.int32, (1, 128), 1)
    for ei in range(e):
        def row_group(g, acc, ei=ei):
            rows = pl.ds(pl.multiple_of(g * 8, 8), 8)
            pos8 = pos_ref[ei, rows, :]
            aff8 = aff_ref[0, ei, rows, :]
            for i in range(8):
                onehot = jnp.where(slot == pos8[i:i + 1], 1.0, 0.0).astype(BF16)
                tok = (g * 8 + i) * 128 + lane
                gv = aff8[i:i + 1]
                g_hi = gv.astype(BF16).astype(F32)
                g_mid = (gv - g_hi).astype(BF16).astype(F32)
                g_lo = gv - g_hi - g_mid
                feats = jnp.concatenate(
                    [lax.shift_right_logical(tok, 6).astype(F32), (tok & 63).astype(F32),
                     g_hi, g_mid, g_lo, jnp.zeros((3, 128), F32)], axis=0).astype(BF16)
                acc = acc + lax.dot_general(feats, onehot, _NT, preferred_element_type=F32)
            return acc

        acc = lax.fori_loop(0, r // 8, row_group, jnp.zeros((8, cap), F32))
        idx_ref[0, ei:ei + 1, :] = (acc[0:1] * 64.0 + acc[1:2]).astype(jnp.int32)
        gate_ref[0, ei:ei + 1, :] = acc[2:3] + acc[3:4] + acc[4:5]


def _moe_select(aff, cap):
    b, e, t = aff.shape
    tpad = max(t, 1024)
    if tpad != t:
        aff = jnp.pad(aff, ((0, 0), (0, 0), (0, tpad - t)), constant_values=-1.0)
    r = tpad // 128
    return pl.pallas_call(
        functools.partial(_select_kernel, cap=cap),
        grid=(b,),
        in_specs=[pl.BlockSpec((1, e, r, 128), lambda bi: (bi, 0, 0, 0))],
        out_specs=[pl.BlockSpec((1, e, cap), lambda bi: (bi, 0, 0)),
                   pl.BlockSpec((1, e, cap), lambda bi: (bi, 0, 0))],
        out_shape=[jax.ShapeDtypeStruct((b, e, cap), jnp.int32),
                   jax.ShapeDtypeStruct((b, e, cap), F32)],
        scratch_shapes=[pltpu.VMEM((e, r, 128), F32)],
        compiler_params=_params("arbitrary"),
        name="moe_select",
    )(aff.reshape(b, e, r, 128))


def _dispatch_kernel(idx_ref, h_hbm, o_ref, buf_ref, sem, *, tm, steps_per_batch):
    i = pl.program_id(0)
    nsteps = pl.num_programs(0)

    def row_copy(step, r, slot):
        tok = idx_ref[step * tm + r]
        return pltpu.make_async_copy(h_hbm.at[lax.div(step, steps_per_batch), pl.ds(tok, 1), :],
                                     buf_ref.at[slot, pl.ds(r, 1), :], sem.at[slot])

    def start_rows(step):
        slot = lax.rem(step, 2)

        def body(r8, carry):
            for j in range(8):
                row_copy(step, r8 * 8 + j, slot).start(priority=j % 2)
            return carry

        lax.fori_loop(0, tm // 8, body, 0)

    @pl.when(i == 0)
    def _():
        start_rows(i)

    @pl.when(i + 1 < nsteps)
    def _():
        start_rows(i + 1)

    slot = lax.rem(i, 2)

    def wait_row(r, carry):
        row_copy(i, r, slot).wait()
        return carry

    lax.fori_loop(0, tm, wait_row, 0, unroll=8)
    o_ref[...] = buf_ref[slot].astype(o_ref.dtype)


def _moe_dispatch(h, idx, tm=256):
    b, t, d = h.shape
    _, e, cap = idx.shape
    tm = min(tm, cap)
    nsteps = b * e * cap // tm
    out = pl.pallas_call(
        functools.partial(_dispatch_kernel, tm=tm, steps_per_batch=e * cap // tm),
        grid_spec=pltpu.PrefetchScalarGridSpec(
            num_scalar_prefetch=1,
            grid=(nsteps,),
            in_specs=[pl.BlockSpec(memory_space=pl.ANY)],
            out_specs=pl.BlockSpec((tm, d), lambda i, idx_ref: (i, 0)),
            scratch_shapes=[pltpu.VMEM((2, tm, d), F32), pltpu.SemaphoreType.DMA((2,))],
        ),
        out_shape=jax.ShapeDtypeStruct((b * e * cap, d), BF16),
        compiler_params=_params("arbitrary"),
        name="moe_dispatch",
    )(idx.reshape(-1), h)
    return out.reshape(b, e, cap, d)


def _first_row_tile():
    return jnp.logical_and(pl.program_id(2) == 0, pl.program_id(3) == 0)


def _expert_up_kernel(x_ref, w1_ref, w3_ref, o_ref, w1b_ref, w3b_ref):
    @pl.when(_first_row_tile())
    def _():
        w1b_ref[...] = w1_ref[...].astype(BF16)
        w3b_ref[...] = w3_ref[...].astype(BF16)

    x = x_ref[0, 0]
    a = jnp.dot(x, w1b_ref[...], preferred_element_type=F32)
    g = jnp.dot(x, w3b_ref[...], preferred_element_type=F32)
    o_ref[0, 0] = (a * jax.nn.sigmoid(a) * g).astype(o_ref.dtype)


def _expert_down_kernel(idx_ref, h_ref, w2_ref, gate_ref, acc_hbm, out_hbm, w2b_ref, rows_ref,
                        gather_sem, scatter_sem, *, tm, ot, cap):
    del acc_hbm
    ei, ni, bi, mi = (pl.program_id(a) for a in range(4))
    nn, nb, nm = (pl.num_programs(a) for a in (1, 2, 3))
    q = (ni * nb + bi) * nm + mi
    slot = lax.rem(q, 2)
    base = (bi * pl.num_programs(0) + ei) * cap + mi * tm
    col = pl.multiple_of(ni * ot, ot)

    @pl.when(_first_row_tile())
    def _():
        w2b_ref[...] = w2_ref[...].astype(BF16)

    def gather(r, s):
        return pltpu.make_async_copy(out_hbm.at[bi, pl.ds(idx_ref[base + r], 1), pl.ds(col, ot)],
                                     rows_ref.at[s, pl.ds(r, 1), :], gather_sem.at[s])

    def scatter(r, s):
        return pltpu.make_async_copy(rows_ref.at[s, pl.ds(r, 1), :],
                                     out_hbm.at[bi, pl.ds(idx_ref[base + r], 1), pl.ds(col, ot)],
                                     scatter_sem.at[s])

    def for_rows(fn):
        def body(r8, carry):
            for j in range(8):
                fn(r8 * 8 + j, j % 2)
            return carry
        lax.fori_loop(0, tm // 8, body, 0)

    @pl.when(q >= 2)
    def _():
        for_rows(lambda r, p: scatter(r, slot).wait())

    for_rows(lambda r, p: gather(r, slot).start(priority=p))
    y = jnp.dot(h_ref[0, 0], w2b_ref[...], preferred_element_type=F32) * gate_ref[0, 0]
    for_rows(lambda r, p: gather(r, slot).wait())
    rows_ref[slot] = rows_ref[slot] + y
    for_rows(lambda r, p: scatter(r, slot).start(priority=p))

    @pl.when(q == nn * nb * nm - 1)
    def _():
        for_rows(lambda r, p: scatter(r, 1 - slot).wait())
        for_rows(lambda r, p: scatter(r, slot).wait())


def _expert_ffn(xin, gate, idx, w1, w3, w2, layer, t, tm_up=512, tm=256):
    b, e, cap, d = xin.shape
    ff = w1.shape[-1]
    tm_up, tm = min(tm_up, cap), min(tm, cap)
    ft, ot = min(MOE_FF_TILE, ff), min(MOE_OUT_TILE, d)
    assert ((d // ot) * b * (cap // tm)) % 2 == 0
    hid = pl.pallas_call(
        _expert_up_kernel,
        grid=(e, ff // ft, b, cap // tm_up),
        in_specs=[pl.BlockSpec((1, 1, tm_up, d), lambda ei, fi, bi, mi: (bi, ei, mi, 0)),
                  pl.BlockSpec((None, None, d, ft), lambda ei, fi, bi, mi: (layer, ei, 0, fi)),
                  pl.BlockSpec((None, None, d, ft), lambda ei, fi, bi, mi: (layer, ei, 0, fi))],
        out_specs=pl.BlockSpec((1, 1, tm_up, ft), lambda ei, fi, bi, mi: (bi, ei, mi, fi)),
        out_shape=jax.ShapeDtypeStruct((b, e, cap, ff), BF16),
        scratch_shapes=[pltpu.VMEM((d, ft), BF16), pltpu.VMEM((d, ft), BF16)],
        compiler_params=_params("arbitrary", "arbitrary", "arbitrary", "arbitrary"),
        name="moe_expert_up",
    )(xin, w1, w3)
    return pl.pallas_call(
        functools.partial(_expert_down_kernel, tm=tm, ot=ot, cap=cap),
        grid_spec=pltpu.PrefetchScalarGridSpec(
            num_scalar_prefetch=1,
            grid=(e, d // ot, b, cap // tm),
            in_specs=[pl.BlockSpec((1, 1, tm, ff), lambda ei, ni, bi, mi, idx_ref: (bi, ei, mi, 0)),
                      pl.BlockSpec((None, None, ff, ot), lambda ei, ni, bi, mi, idx_ref: (layer, ei, 0, ni)),
                      pl.BlockSpec((1, 1, tm, 1), lambda ei, ni, bi, mi, idx_ref: (bi, ei, mi, 0)),
                      pl.BlockSpec(memory_space=pl.ANY)],
            out_specs=pl.BlockSpec(memory_space=pl.ANY),
            scratch_shapes=[pltpu.VMEM((ff, ot), BF16), pltpu.VMEM((2, tm, ot), F32),
                            pltpu.SemaphoreType.DMA((2,)), pltpu.SemaphoreType.DMA((2,))],
        ),
        out_shape=jax.ShapeDtypeStruct((b, t, d), F32),
        input_output_aliases={4: 0},
        compiler_params=_params("arbitrary", "arbitrary", "arbitrary", "arbitrary"),
        name="moe_expert_down",
    )(idx.reshape(-1), hid, w2, gate[..., None], jnp.zeros((b, t, d), F32))


def _expert_choice_moe(h, router_t, w1, w3, w2, layer):
    b, t, d = h.shape
    cap = EC_CAPACITY_FACTOR * t // N_EXPERTS
    aff = _router_affinity(h, router_t)
    idx, gate = _moe_select(aff, cap)
    idx = jnp.clip(idx, 0, t - 1)
    xin = _moe_dispatch(h, idx)
    return _expert_ffn(xin, gate, idx, w1, w3, w2, layer, t)


def _project(h, w, out_dtype, name, col0=0, n=None):
    b, t, d = h.shape
    out = _matmul(h.reshape(b * t, d), w, 0, out_dtype, name, col0=col0, n=n)
    return out.reshape(b, t, out.shape[1])


def _mixer_na_s5(h, hc, w_in, w_out, rpb, s5_tables, glu_w, glu_b):
    b, t, _ = h.shape
    tc = hc.shape[1]
    qkv = _project(h, w_in, BF16, "ab_in_qkv", 0, 3 * NA_W)
    qkv_c = _project(hc, w_in, BF16, "ab_in_qkv_ctx", 0, 3 * NA_W)
    u = _project(h, w_in, F32, "ab_in_u", 3 * NA_W, S5_WIDTH)
    u_c = _project(hc, w_in, F32, "ab_in_u_ctx", 3 * NA_W, S5_WIDTH)
    o_na = _neighbourhood_attention(qkv, qkv_c, rpb)
    oc_na = _context_attention(qkv_c)
    y_s5, yc_s5 = _s5_scan(u, u_c, s5_tables)
    z = _s5_glu(y_s5.reshape(b * t, S5_WIDTH), glu_w, glu_b)
    z_c = _s5_glu(yc_s5.reshape(b * tc, S5_WIDTH), glu_w, glu_b)
    y = _matmul([o_na.reshape(b * t, NA_W), z], w_out, 0, F32, "ab_out")
    y_c = _matmul([oc_na.reshape(b * tc, NA_W), z_c], w_out, 0, F32, "ab_out_ctx")
    return y.reshape(b, t, -1), y_c.reshape(b, tc, -1)


def _mixer_hgrn2(h, hc, w_in, w_out, lb, norm_g):
    proj = _project(h, w_in, F32, "hg_in")
    proj_c = _project(hc, w_in, F32, "hg_in_ctx")
    return _project(_hgrn2_gla(proj, proj_c, lb, norm_g), w_out, F32, "hg_out")


def kernel(x, c, ctx, c_ctx, ada_w, ada_b, norm_g, ab_w_in, ab_w_out, na_rpb, s5_lam_re, s5_lam_im,
           s5_log_dt, s5_b_re, s5_b_im, s5_c_re, s5_c_im, s5_d, s5_glu_w, s5_glu_b, hg_w_in, hg_w_out,
           hg_lb_logits, hg_norm_g, moe_router, moe_w1, moe_w3, moe_w2):
    bsz, _, d = x.shape
    depth = ada_w.shape[0]
    assert depth == 2 and bsz + 1 <= 8
    assert ab_w_in.shape[0] == 1 and hg_w_in.shape[0] == 1
    xc = ctx
    cond = jax.nn.silu(c)
    cond_ctx = jax.nn.silu(c_ctx)[None]
    cond3 = jnp.concatenate([cond, cond_ctx, jnp.zeros((8 - bsz - 1, d), F32)], axis=0)
    hg_lb = jnp.cumsum(jax.nn.softmax(hg_lb_logits.astype(F32), axis=0), axis=0)
    moe_w = (moe_w1, moe_w3, moe_w2)
    ada_x, ada_c = [], []
    for layer in range(depth):
        ada = _matmul(cond3, ada_w, layer, F32, "adaln") + ada_b[layer][None]
        ada_x.append(jnp.split(ada[:bsz], 6, axis=-1))
        ada_c.append([jnp.broadcast_to(a, (bsz, d)) for a in jnp.split(ada[bsz:bsz + 1], 6, axis=-1)])
    ng0 = norm_g[0].astype(F32)
    h = _modulate(x, ng0[0], ada_x[0][0], ada_x[0][1])
    hc = _modulate(xc, ng0[0], ada_c[0][0], ada_c[0][1])
    for layer in range(depth):
        need_ctx = layer < depth - 1
        _, _, g1, sh2, sc2, g2 = ada_x[layer]
        _, _, cg1, csh2, csc2, cg2 = ada_c[layer]
        ng = norm_g[layer].astype(F32)
        if layer == 0:
            tables = _s5_tables(s5_lam_re[0], s5_lam_im[0], s5_log_dt[0], s5_b_re[0], s5_b_im[0],
                                s5_c_re[0], s5_c_im[0], s5_d[0])
            y, yc = _mixer_na_s5(h, hc, ab_w_in, ab_w_out, na_rpb[0], tables, s5_glu_w[0],
                                 s5_glu_b[0].astype(F32))
        else:
            y = _mixer_hgrn2(h, hc, hg_w_in, hg_w_out, hg_lb[0], hg_norm_g[0])
            yc = None
        router_t = moe_router[layer].T.astype(BF16)
        x, h2 = _gated_residual_modulate(x, y, ng[1], g1, ng[2], sh2, sc2, F32)
        moe = _expert_choice_moe(h2, router_t, *moe_w, layer)
        if not need_ctx:
            x = _gated_residual(x, moe, ng[3], g2)
            continue
        ng_next = norm_g[layer + 1].astype(F32)
        x, h = _gated_residual_modulate(x, moe, ng[3], g2, ng_next[0], ada_x[layer + 1][0],
                                        ada_x[layer + 1][1], BF16)
        xc, hc2 = _gated_residual_modulate(xc, yc, ng[1], cg1, ng[2], csh2, csc2, F32)
        moe_c = _expert_choice_moe(hc2, router_t, *moe_w, layer)
        xc, hc = _gated_residual_modulate(xc, moe_c, ng[3], cg2, ng_next[0], ada_c[layer + 1][0],
                                          ada_c[layer + 1][1], BF16)
    return x
```

```python
import functools
import math

import numpy as np
import jax
import jax.numpy as jnp
from jax import lax
from jax.experimental import pallas as pl
from jax.experimental.pallas import tpu as pltpu

F32 = jnp.float32
BF16 = jnp.bfloat16
HIGHEST = lax.Precision.HIGHEST

D_MODEL = 4096
GRID_W = 64
NORM_EPS = 1e-6
NA_HEADS = 16
NA_HEAD_DIM = 128
NA_WIN_R = 8
NA_WIN_C = 16
NA_W = NA_HEADS * NA_HEAD_DIM
S5_WIDTH = 2048
S5_GROUP = 16
S5_GROUPS = S5_WIDTH // S5_GROUP
S5_STATE = 64
S5_CHUNK = 16
HG_HEADS = 32
HG_DIM = 128
HG_CHUNK = 64
HG_Q = HG_HEADS * HG_DIM
HG_HEADS_PER_STEP = 2
ROPE_THETA = 10000.0
N_EXPERTS = 16
EXPERT_FF = 1024
EC_CAPACITY_FACTOR = 2

VMEM_LIMIT_BYTES = 56 * 1024 * 1024
MOE_FF_TILE = 512
MOE_OUT_TILE = 4096
NA_QROWS = 8
NA_KROWS = 16
NEG_BIG = -1e30

_NT = (((1,), (1,)), ((), ()))


def _params(*sem):
    return pltpu.CompilerParams(dimension_semantics=sem, vmem_limit_bytes=VMEM_LIMIT_BYTES)


def _mm_kernel(*refs, splits):
    *a_refs, w_ref, o_ref, wb_ref = refs

    @pl.when(pl.program_id(1) == 0)
    def _():
        wb_ref[...] = w_ref[...].astype(BF16)

    acc = None
    for a_ref, (k0, k1) in zip(a_refs, splits):
        part = jnp.dot(a_ref[...].astype(BF16), wb_ref[k0:k1, :], preferred_element_type=F32)
        acc = part if acc is None else acc + part
    o_ref[...] = acc.astype(o_ref.dtype)


def _matmul(a, w, layer, out_dtype, name, col0=0, n=None, tm=1024, tn=512):
    pieces = list(a) if isinstance(a, (list, tuple)) else [a]
    m = pieces[0].shape[0]
    k = w.shape[1]
    bounds = np.cumsum([0] + [piece.shape[1] for piece in pieces])
    assert bounds[-1] == k and all(piece.shape[0] == m for piece in pieces)
    n = w.shape[2] - col0 if n is None else n
    tm = min(tm, m)
    tn = min(tn, n)
    assert m % tm == 0 and n % tn == 0 and col0 % tn == 0, (m, n, col0, tm, tn)
    cb = col0 // tn
    return pl.pallas_call(
        functools.partial(_mm_kernel, splits=tuple(zip(bounds[:-1].tolist(), bounds[1:].tolist()))),
        grid=(n // tn, m // tm),
        in_specs=[pl.BlockSpec((tm, piece.shape[1]), lambda j, i: (i, 0)) for piece in pieces]
        + [pl.BlockSpec((None, k, tn), lambda j, i: (layer, 0, cb + j))],
        out_specs=pl.BlockSpec((tm, tn), lambda j, i: (i, j)),
        out_shape=jax.ShapeDtypeStruct((m, n), out_dtype),
        scratch_shapes=[pltpu.VMEM((k, tn), BF16)],
        compiler_params=_params("arbitrary", "arbitrary"),
        name=name,
    )(*pieces, w)


def _modulate_kernel(x_ref, g_ref, sh_ref, sc_ref, o_ref):
    x = x_ref[0]
    ms = jnp.mean(x * x, axis=-1, keepdims=True)
    y = x * lax.rsqrt(ms + NORM_EPS) * g_ref[...]
    o_ref[0] = (y * (1.0 + sc_ref[0]) + sh_ref[0]).astype(o_ref.dtype)


def _modulate(x, g, shift, scale, out_dtype=BF16, tt=256):
    b, t, d = x.shape
    tt = min(tt, t)
    row = lambda bi, ti: (bi, 0, 0)
    return pl.pallas_call(
        _modulate_kernel,
        grid=(b, t // tt),
        in_specs=[pl.BlockSpec((1, tt, d), lambda bi, ti: (bi, ti, 0)),
                  pl.BlockSpec((1, d), lambda bi, ti: (0, 0)),
                  pl.BlockSpec((1, 1, d), row),
                  pl.BlockSpec((1, 1, d), row)],
        out_specs=pl.BlockSpec((1, tt, d), lambda bi, ti: (bi, ti, 0)),
        out_shape=jax.ShapeDtypeStruct((b, t, d), out_dtype),
        compiler_params=_params("arbitrary", "arbitrary"),
        name="modulate",
    )(x, g.reshape(1, d), shift.reshape(b, 1, d), scale.reshape(b, 1, d))


def _residual_kernel(x_ref, y_ref, g_ref, gate_ref, o_ref):
    y = y_ref[0].astype(F32)
    ms = jnp.mean(y * y, axis=-1, keepdims=True)
    o_ref[0] = x_ref[0] + gate_ref[0] * (y * lax.rsqrt(ms + NORM_EPS) * g_ref[...])


def _gated_residual(x, y, g, gate, tt=256):
    b, t, d = x.shape
    tt = min(tt, t)
    blk = pl.BlockSpec((1, tt, d), lambda bi, ti: (bi, ti, 0))
    return pl.pallas_call(
        _residual_kernel,
        grid=(b, t // tt),
        in_specs=[blk, blk,
                  pl.BlockSpec((1, d), lambda bi, ti: (0, 0)),
                  pl.BlockSpec((1, 1, d), lambda bi, ti: (bi, 0, 0))],
        out_specs=blk,
        out_shape=jax.ShapeDtypeStruct((b, t, d), F32),
        compiler_params=_params("arbitrary", "arbitrary"),
        name="gated_residual",
    )(x, y, g.reshape(1, d), gate.reshape(b, 1, d))


def _residual_modulate_kernel(x_ref, y_ref, g_ref, gate_ref, g2_ref, sh_ref, sc_ref, xo_ref, ho_ref):
    y = y_ref[0].astype(F32)
    ms = jnp.mean(y * y, axis=-1, keepdims=True)
    x = x_ref[0] + gate_ref[0] * (y * lax.rsqrt(ms + NORM_EPS) * g_ref[...])
    xo_ref[0] = x
    ms2 = jnp.mean(x * x, axis=-1, keepdims=True)
    h = x * lax.rsqrt(ms2 + NORM_EPS) * g2_ref[...]
    ho_ref[0] = (h * (1.0 + sc_ref[0]) + sh_ref[0]).astype(ho_ref.dtype)


def _gated_residual_modulate(x, y, g, gate, g2, shift, scale, out_dtype, tt=256):
    b, t, d = x.shape
    tt = min(tt, t)
    blk = pl.BlockSpec((1, tt, d), lambda bi, ti: (bi, ti, 0))
    vec = pl.BlockSpec((1, d), lambda bi, ti: (0, 0))
    row = pl.BlockSpec((1, 1, d), lambda bi, ti: (bi, 0, 0))
    per_row = lambda a: a.reshape(b, 1, d)
    return pl.pallas_call(
        _residual_modulate_kernel,
        grid=(b, t // tt),
        in_specs=[blk, blk, vec, row, vec, row, row],
        out_specs=[blk, blk],
        out_shape=[jax.ShapeDtypeStruct((b, t, d), F32), jax.ShapeDtypeStruct((b, t, d), out_dtype)],
        compiler_params=_params("arbitrary", "arbitrary"),
        name="gated_residual_modulate",
    )(x, y, g.reshape(1, d), per_row(gate), g2.reshape(1, d), per_row(shift), per_row(scale))


def _na_bias_table(rpb, rows):
    nblk = rows // NA_QROWS
    qc = np.arange(GRID_W)
    cs = np.clip(qc - NA_WIN_C // 2, 0, GRID_W - NA_WIN_C)
    kc = np.arange(GRID_W)
    in_c = (kc[None, :] >= cs[:, None]) & (kc[None, :] < cs[:, None] + NA_WIN_C)
    rel_c = kc[None, :] - qc[:, None] + NA_WIN_C - 1
    sel_c = (rel_c[:, :, None] == np.arange(2 * NA_WIN_C - 1)).astype(np.float32)
    sel_r, valid = [], []
    for j in (0, 1, nblk - 1):
        qr = NA_QROWS * j + np.arange(NA_QROWS)
        ws = int(np.clip(NA_QROWS * j - NA_WIN_R // 2, 0, rows - NA_KROWS))
        kr = ws + np.arange(NA_KROWS)
        r0 = np.clip(qr - NA_WIN_R // 2, 0, rows - NA_WIN_R)
        in_r = (kr[None, :] >= r0[:, None]) & (kr[None, :] < r0[:, None] + NA_WIN_R)
        rel_r = kr[None, :] - qr[:, None] + NA_WIN_R - 1
        sel_r.append((rel_r[:, :, None] == np.arange(2 * NA_WIN_R - 1)).astype(np.float32))
        valid.append(in_r[:, None, :, None] & in_c[None, :, None, :])
    by_row = jnp.einsum('hrc,pqkr->phqkc', rpb.astype(F32), np.stack(sel_r), precision=HIGHEST)
    bias = jnp.einsum('phqkc,xyc->phqxky', by_row, sel_c, precision=HIGHEST)
    bias = jnp.where(np.stack(valid)[:, None], bias, NEG_BIG)
    return bias.reshape(3, rpb.shape[0], NA_QROWS * GRID_W, NA_KROWS * GRID_W)


def _na_kernel(q_ref, k_ref, v_ref, kc_ref, vc_ref, bias_ref, o_ref, *, rows):
    j = pl.program_id(2)
    ws = jnp.clip(NA_QROWS * j - NA_WIN_R // 2, 0, rows - NA_KROWS) * GRID_W
    ws = pl.multiple_of(ws, 256)
    nk = NA_KROWS * GRID_W
    scale = NA_HEAD_DIM ** -0.5
    q = q_ref[0]
    kw = k_ref[0, pl.ds(ws, nk), :]
    vw = v_ref[0, pl.ds(ws, nk), :]
    s_nb = lax.dot_general(q, kw, _NT, preferred_element_type=F32) * scale + bias_ref[0, 0]
    s_c = lax.dot_general(q, kc_ref[0], _NT, preferred_element_type=F32) * scale
    m = jnp.maximum(jnp.max(s_nb, axis=-1, keepdims=True), jnp.max(s_c, axis=-1, keepdims=True))
    p_nb = jnp.exp(s_nb - m)
    p_c = jnp.exp(s_c - m)
    denom = jnp.sum(p_nb, axis=-1, keepdims=True) + jnp.sum(p_c, axis=-1, keepdims=True)
    o = (jnp.dot(p_nb.astype(BF16), vw, preferred_element_type=F32)
         + jnp.dot(p_c.astype(BF16), vc_ref[0], preferred_element_type=F32))
    o_ref[0] = (o / denom).astype(o_ref.dtype)


def _neighbourhood_attention(qkv, qkv_c, rpb):
    b, t, _ = qkv.shape
    tc = qkv_c.shape[1]
    rows = t // GRID_W
    nblk = rows // NA_QROWS
    tq = NA_QROWS * GRID_W
    bias = _na_bias_table(rpb, rows)
    pat = lambda j: jnp.where(j == 0, 0, jnp.where(j == nblk - 1, 2, 1))
    return pl.pallas_call(
        functools.partial(_na_kernel, rows=rows),
        grid=(b, NA_HEADS, nblk),
        in_specs=[
            pl.BlockSpec((1, tq, NA_HEAD_DIM), lambda bi, h, j: (bi, j, h)),
            pl.BlockSpec((1, t, NA_HEAD_DIM), lambda bi, h, j: (bi, 0, NA_HEADS + h)),
            pl.BlockSpec((1, t, NA_HEAD_DIM), lambda bi, h, j: (bi, 0, 2 * NA_HEADS + h)),
            pl.BlockSpec((1, tc, NA_HEAD_DIM), lambda bi, h, j: (bi, 0, NA_HEADS + h)),
            pl.BlockSpec((1, tc, NA_HEAD_DIM), lambda bi, h, j: (bi, 0, 2 * NA_HEADS + h)),
            pl.BlockSpec((1, 1, tq, NA_KROWS * GRID_W), lambda bi, h, j: (pat(j), h, 0, 0)),
        ],
        out_specs=pl.BlockSpec((1, tq, NA_HEAD_DIM), lambda bi, h, j: (bi, j, h)),
        out_shape=jax.ShapeDtypeStruct((b, t, NA_W), BF16),
        compiler_params=_params("arbitrary", "arbitrary", "arbitrary"),
        name="neighbourhood_attention",
    )(qkv, qkv, qkv, qkv_c, qkv_c, bias)


def _ctx_attn_kernel(q_ref, k_ref, v_ref, o_ref):
    scale = NA_HEAD_DIM ** -0.5
    s = lax.dot_general(q_ref[0], k_ref[0], _NT, preferred_element_type=F32) * scale
    p = jnp.exp(s - jnp.max(s, axis=-1, keepdims=True))
    denom = jnp.sum(p, axis=-1, keepdims=True)
    o = jnp.dot(p.astype(BF16), v_ref[0], preferred_element_type=F32)
    o_ref[0] = (o / denom).astype(o_ref.dtype)


def _context_attention(qkv_c):
    b, tc, _ = qkv_c.shape
    blk = lambda off: pl.BlockSpec((1, tc, NA_HEAD_DIM), lambda bi, h: (bi, 0, off + h))
    return pl.pallas_call(
        _ctx_attn_kernel,
        grid=(b, NA_HEADS),
        in_specs=[blk(0), blk(NA_HEADS), blk(2 * NA_HEADS)],
        out_specs=blk(0),
        out_shape=jax.ShapeDtypeStruct((b, tc, NA_W), BF16),
        compiler_params=_params("arbitrary", "arbitrary"),
        name="context_attention",
    )(qkv_c, qkv_c, qkv_c)


def _s5_tables(lam_re, lam_im, log_dt, b_re, b_im, c_re, c_im, d_skip):
    ll, gi, p, g = S5_CHUNK, S5_GROUP, S5_STATE, S5_GROUPS
    f32 = F32
    lam = lax.complex(lam_re.astype(f32), lam_im.astype(f32))
    dt = jnp.exp(log_dt.astype(f32))[..., None]
    lam_dt = lam * dt
    lam_bar = jnp.exp(lam_dt)
    b_bar = ((lam_bar - 1.0) / lam)[..., None] * lax.complex(b_re.astype(f32), b_im.astype(f32))[None]
    c_mat = lax.complex(c_re.astype(f32), c_im.astype(f32))
    taus = jnp.arange(ll + 1, dtype=f32)
    pw = jnp.exp(lam_dt[None] * taus[:, None, None, None])

    def cmul_sum(a, b, spec):
        ar, ai, br, bi = jnp.real(a), jnp.imag(a), jnp.real(b), jnp.imag(b)
        e = lambda x, y: jnp.einsum(spec, x, y, precision=HIGHEST)
        return e(ar, br) - e(ai, bi), e(ar, bi) + e(ai, br)

    cp = c_mat[:, None] * jnp.moveaxis(pw[:ll], 0, 1)[:, :, :, None, :]
    k_re, _ = cmul_sum(cp, b_bar, 'dtgip,dgpj->dtgij')
    idx = np.arange(ll)
    tau_f = idx[None, :] - idx[:, None]
    kf = jnp.where((tau_f >= 0)[:, :, None, None, None], k_re[0][np.clip(tau_f, 0, ll - 1)], 0.0)
    kb = jnp.where((tau_f <= 0)[:, :, None, None, None], k_re[1][np.clip(-tau_f, 0, ll - 1)], 0.0)
    skip = (jnp.eye(ll, dtype=f32)[:, :, None, None, None]
            * (d_skip.astype(f32)[:, :, None] * jnp.eye(gi, dtype=f32))[None, None])
    t_sum = kf + kb + skip
    t_sum = jnp.transpose(t_sum, (2, 0, 4, 1, 3)).reshape(g, ll * gi, ll * gi)

    wf = pw[ll - 1 - idx, 0][:, :, :, None] * b_bar[0][None]
    wb = pw[idx, 1][:, :, :, None] * b_bar[1][None]
    to_rows = lambda w: jnp.transpose(w, (1, 0, 3, 2)).reshape(g, ll * gi, p)
    w_blocks = [to_rows(f(w)) for f in (jnp.real, jnp.imag) for w in (wf, wb)]
    m_in = jnp.concatenate([t_sum] + w_blocks, axis=-1)

    zf = c_mat[0][None] * pw[idx + 1, 0][:, :, None, :]
    zb = c_mat[1][None] * pw[ll - idx, 1][:, :, None, :]
    to_cols = lambda z: jnp.swapaxes(jnp.transpose(z, (1, 0, 2, 3)).reshape(g, ll * gi, p), 1, 2)
    z_blocks = [to_cols(f(z)) for f in (jnp.real, lambda v: -jnp.imag(v)) for z in (zf, zb)]
    z_out = jnp.concatenate(z_blocks, axis=1)

    lam_l = jnp.stack([jnp.concatenate([f(pw[ll, 0]), f(pw[ll, 1])], axis=-1)
                       for f in (jnp.real, jnp.imag)], axis=1)
    return m_in, z_out, lam_l


def _split_bf16(x):
    hi = x.astype(BF16)
    return hi, (x - hi.astype(F32)).astype(BF16)


def _dot3(a, b_hi, b_lo):
    a_hi, a_lo = _split_bf16(a)
    d = lambda p, q: jnp.dot(p, q, preferred_element_type=F32)
    return d(a_hi, b_hi) + d(a_lo, b_hi) + d(a_hi, b_lo)


def _s5_kernel(u_ref, uc_ref, mh_ref, ml_ref, zh_ref, zl_ref, lam_ref, y_ref, yc_ref,
               ug_ref, r_ref, s_ref, *, nbatch, nlat, nctx):
    ll, gi, p = S5_CHUNK, S5_GROUP, S5_STATE
    width = ll * gi
    per_vreg = 128 // gi
    nchunk = nctx + nlat
    g = pl.program_id(1)

    def group_rows(src_ref, bi, n):
        lane = lax.broadcasted_iota(jnp.int32, (n, 128), 1)
        cols = []
        for half in range(ll // per_vreg):
            acc = None
            for tt in range(per_vreg):
                ut = src_ref[bi, pl.ds(half * per_vreg + tt, n, stride=ll), :]
                rot = pltpu.roll(ut, jnp.mod(gi * (tt - g), 128), 1)
                keep = jnp.logical_and(lane >= gi * tt, lane < gi * (tt + 1))
                acc = rot if acc is None else jnp.where(keep, rot, acc)
            cols.append(acc)
        return jnp.concatenate(cols, axis=1)

    for bi in range(nbatch):
        ug_ref[bi * nchunk:bi * nchunk + nctx, :] = group_rows(uc_ref, bi, nctx)
        ug_ref[bi * nchunk + nctx:(bi + 1) * nchunk, :] = group_rows(u_ref, bi, nlat)

    r_ref[...] = _dot3(ug_ref[...], mh_ref[0], ml_ref[0])
    lam = lam_ref[0]
    lam_r, lam_i = lam[0:1], lam[1:2]
    fwd_lane = lax.broadcasted_iota(jnp.int32, (1, 2 * p), 1) < p
    v_re, v_im = slice(width, width + 2 * p), slice(width + 2 * p, width + 4 * p)

    sub = 8
    ngrp, nctx_g = nchunk // sub, nctx // sub

    def body(kg, carry):
        gb = jnp.where(kg < nctx_g, nctx_g - 1 - kg, ngrp - 1 - (kg - nctx_g))
        out = []
        for bi in range(nbatch):
            sr, si = carry[bi]
            rf = pl.ds(pl.multiple_of(bi * nchunk + kg * sub, sub), sub)
            rb = pl.ds(pl.multiple_of(bi * nchunk + gb * sub, sub), sub)
            vfr, vfi = r_ref[rf, v_re], r_ref[rf, v_im]
            vbr, vbi = r_ref[rb, v_re], r_ref[rb, v_im]
            before_r, before_i = [], []
            for i in range(sub):
                before_r.append(sr)
                before_i.append(si)
                j = sub - 1 - i
                vr = jnp.where(fwd_lane, vfr[i:i + 1], vbr[j:j + 1])
                vi = jnp.where(fwd_lane, vfi[i:i + 1], vbi[j:j + 1])
                sr, si = lam_r * sr - lam_i * si + vr, lam_r * si + lam_i * sr + vi
            s_ref[rf, 0:p] = jnp.concatenate(before_r, axis=0)[:, 0:p]
            s_ref[rb, p:2 * p] = jnp.concatenate(before_r[::-1], axis=0)[:, p:2 * p]
            s_ref[rf, 2 * p:3 * p] = jnp.concatenate(before_i, axis=0)[:, 0:p]
            s_ref[rb, 3 * p:4 * p] = jnp.concatenate(before_i[::-1], axis=0)[:, p:2 * p]
            out.append((sr, si))
        return tuple(out)

    zero = jnp.zeros((1, 2 * p), F32)
    lax.fori_loop(0, ngrp, body, tuple((zero, zero) for _ in range(nbatch)))
    yg = r_ref[:, 0:width] + _dot3(s_ref[...], zh_ref[0], zl_ref[0])

    @pl.when(g == 0)
    def _():
        y_ref[...] = jnp.zeros(y_ref.shape, F32)
        yc_ref[...] = jnp.zeros(yc_ref.shape, F32)

    def scatter_rows(dst_ref, bi, row0, n):
        lane = lax.broadcasted_iota(jnp.int32, (n, 128), 1)
        mine = jnp.logical_and(lane >= gi * g, lane < gi * (g + 1))
        for t in range(ll):
            half, tt = divmod(t, per_vreg)
            piece = yg[row0:row0 + n, 128 * half:128 * (half + 1)]
            rot = pltpu.roll(piece, jnp.mod(gi * (g - tt), 128), 1)
            rows = pl.ds(t, n, stride=ll)
            dst_ref[bi, rows, :] = jnp.where(mine, rot, dst_ref[bi, rows, :])

    for bi in range(nbatch):
        scatter_rows(yc_ref, bi, bi * nchunk, nctx)
        scatter_rows(y_ref, bi, bi * nchunk + nctx, nlat)


def _s5_scan(u, u_c, tables):
    m_in, z_out, lam_l = tables
    b, t, _ = u.shape
    tc = u_c.shape[1]
    ll, gi, g, p = S5_CHUNK, S5_GROUP, S5_GROUPS, S5_STATE
    nlat, nctx = t // ll, tc // ll
    rows = b * (nlat + nctx)
    width = ll * gi
    per_slab = 128 // gi
    m_hi, m_lo = _split_bf16(m_in)
    z_hi, z_lo = _split_bf16(z_out)
    wspec = lambda shape: pl.BlockSpec((1,) + shape, lambda s, j: (s * per_slab + j, 0, 0))
    slab = lambda n: pl.BlockSpec((b, n, 128), lambda s, j: (0, 0, s))
    return pl.pallas_call(
        functools.partial(_s5_kernel, nbatch=b, nlat=nlat, nctx=nctx),
        grid=(g // per_slab, per_slab),
        in_specs=[slab(t), slab(tc),
                  wspec((width, width + 4 * p)), wspec((width, width + 4 * p)),
                  wspec((4 * p, width)), wspec((4 * p, width)), wspec((2, 2 * p))],
        out_specs=[slab(t), slab(tc)],
        out_shape=[jax.ShapeDtypeStruct((b, t, g * gi), F32), jax.ShapeDtypeStruct((b, tc, g * gi), F32)],
        scratch_shapes=[pltpu.VMEM((rows, width), F32), pltpu.VMEM((rows, width + 4 * p), F32),
                        pltpu.VMEM((rows, 4 * p), F32)],
        compiler_params=_params("arbitrary", "arbitrary"),
        name="s5_scan",
    )(u, u_c, m_hi, m_lo, z_hi, z_lo, lam_l)


def _glu_kernel(y_ref, w_ref, b_ref, o_ref, wb_ref):
    @pl.when(pl.program_id(0) == 0)
    def _():
        wb_ref[...] = w_ref[...].astype(BF16)

    z = jax.nn.gelu(y_ref[...])
    a = jnp.dot(z.astype(BF16), wb_ref[...], preferred_element_type=F32) + b_ref[...]
    o_ref[...] = (z * jax.nn.sigmoid(a)).astype(o_ref.dtype)


def _s5_glu(y, w, bias, tm=256):
    m, n = y.shape
    tm = min(tm, m)
    return pl.pallas_call(
        _glu_kernel,
        grid=(m // tm,),
        in_specs=[pl.BlockSpec((tm, n), lambda i: (i, 0)),
                  pl.BlockSpec((n, n), lambda i: (0, 0)),
                  pl.BlockSpec((1, n), lambda i: (0, 0))],
        out_specs=pl.BlockSpec((tm, n), lambda i: (i, 0)),
        out_shape=jax.ShapeDtypeStruct((m, n), BF16),
        scratch_shapes=[pltpu.VMEM((n, n), BF16)],
        compiler_params=_params("arbitrary"),
        name="s5_glu",
    )(y, w, bias.reshape(1, n))


def _gla_state_kernel(f_ref, v_ref, lb_ref, tri_ref, s_ref, *, nchunk):
    d = pl.program_id(2)
    lb = lb_ref[0]
    st = jnp.zeros((HG_DIM, HG_DIM), F32)
    for c in range(nchunk):
        cc = jnp.where(d == 0, c, nchunk - 1 - c)
        sl = pl.ds(pl.multiple_of(cc * HG_CHUNK, HG_CHUNK), HG_CHUNK)
        f = lb + (1.0 - lb) * jax.nn.sigmoid(f_ref[0, sl, :])
        logf = jnp.log(f)
        b = jnp.dot(tri_ref[0], logf, precision=HIGHEST, preferred_element_type=F32)
        tot = jnp.sum(logf, axis=0, keepdims=True)
        kd = ((1.0 - f) * jnp.exp(tot - b)).astype(BF16)
        st = st * jnp.exp(tot) + jnp.dot(v_ref[0, sl, :].T.astype(BF16), kd, preferred_element_type=F32)
    s_ref[0, 0, 0] = st


def _gla_kernel(q_ref, f_ref, v_ref, cos_ref, sin_ref, lb_ref, tri_ref, s0_ref, *rest, nchunk, reverse):
    if reverse:
        ofwd_ref, gate_ref, ng_ref, o_ref, st_ref = rest
    else:
        o_ref, st_ref = rest
    heads = q_ref.shape[2] // HG_DIM

    @pl.when(pl.program_id(2) == 0)
    def _():
        for h in range(heads):
            st_ref[h] = s0_ref[0, h, 0]

    pair = 2 * HG_CHUNK
    tri = tri_ref[...]
    keep = tri > 0
    quarter = HG_DIM // 4
    lane = lax.broadcasted_iota(jnp.int32, (q_ref.shape[1], HG_DIM), 1)
    logf, k, v, q, hi, lo, st = [], [], [], [], [], [], []
    for h in range(heads):
        hl = slice(HG_DIM * h, HG_DIM * (h + 1))
        lb = lb_ref[h]
        f = lb + (1.0 - lb) * jax.nn.sigmoid(f_ref[0, :, hl])
        logf.append(jnp.log(f))
        k.append(1.0 - f)
        v.append(v_ref[0, :, hl])
        sg = jax.nn.sigmoid(q_ref[0, :, hl])
        partner = jnp.where((lane & quarter) == 0, pltpu.roll(sg, HG_DIM - quarter, 1),
                            pltpu.roll(sg, quarter, 1))
        q.append(sg * cos_ref[...] + partner * sin_ref[...])
        hi_h, lo_h = _split_bf16(logf[h])
        hi.append(hi_h)
        lo.append(lo_h)
        st.append(st_ref[h])
    pairs = range(nchunk // 2)
    for p in (reversed(pairs) if reverse else pairs):
        sl = slice(pair * p, pair * (p + 1))
        for h in range(heads):
            hl = slice(HG_DIM * h, HG_DIM * (h + 1))
            b = (jnp.dot(tri, hi[h][sl], preferred_element_type=F32)
                 + jnp.dot(tri, lo[h][sl], preferred_element_type=F32))
            qe = (q[h][sl] * jnp.exp(b)).astype(BF16)
            ke = (k[h][sl] * jnp.exp(-b)).astype(BF16)
            att = jnp.where(keep, lax.dot_general(qe, ke, _NT, preferred_element_type=F32), 0.0)
            o_intra = jnp.dot(att.astype(BF16), v[h][sl].astype(BF16), preferred_element_type=F32)
            o_inter = [None, None]
            for half in ((1, 0) if reverse else (0, 1)):
                hs = slice(HG_CHUNK * half, HG_CHUNK * (half + 1))
                rows = slice(pair * p + HG_CHUNK * half, pair * p + HG_CHUNK * (half + 1))
                tot = jnp.sum(logf[h][rows], axis=0, keepdims=True)
                kd = (k[h][rows] * jnp.exp(tot - b[hs])).astype(BF16)
                o_inter[half] = lax.dot_general(qe[hs], st[h].astype(BF16), _NT,
                                                preferred_element_type=F32)
                st[h] = st[h] * jnp.exp(tot) + jnp.dot(v[h][rows].T.astype(BF16), kd,
                                                       preferred_element_type=F32)
            o = o_intra + jnp.concatenate(o_inter, axis=0)
            if reverse:
                o = o + ofwd_ref[0, sl, hl]
                ms = jnp.mean(o * o, axis=-1, keepdims=True)
                gate = gate_ref[0, sl, hl]
                o = o * lax.rsqrt(ms + NORM_EPS) * ng_ref[h] * (gate * jax.nn.sigmoid(gate))
            o_ref[0, sl, hl] = o.astype(o_ref.dtype)
    for h in range(heads):
        st_ref[h] = st[h]


def _rope_tables(t):
    pos = np.arange(t)
    quarter = HG_DIM // 4
    inv_freq = np.float32(ROPE_THETA) ** (-np.arange(quarter, dtype=np.float32) / np.float32(quarter))

    def tab(p):
        ang = p.astype(np.float32)[:, None] * inv_freq[None, :]
        c, s = np.cos(ang, dtype=np.float32), np.sin(ang, dtype=np.float32)
        return np.concatenate([c, c], axis=-1), np.concatenate([-s, s], axis=-1)

    c_r, s_r = tab(pos // GRID_W)
    c_c, s_c = tab(pos % GRID_W)
    return (jnp.asarray(np.concatenate([c_r, c_c], axis=-1)),
            jnp.asarray(np.concatenate([s_r, s_c], axis=-1)))


def _gla_masks():
    i = np.arange(HG_CHUNK)
    lower = (i[:, None] >= i[None, :]).astype(np.float32)
    return jnp.asarray(np.stack([lower, lower.T]))


def _hgrn2_gla(proj, proj_c, lb, norm_g, tblock=2048):
    b, t, _ = proj.shape
    tc = proj_c.shape[1]
    tblock = min(tblock, t)
    nt = t // tblock
    tri = _gla_masks()
    lb3 = lb.astype(F32).reshape(HG_HEADS, 1, HG_DIM)

    s0 = pl.pallas_call(
        functools.partial(_gla_state_kernel, nchunk=tc // HG_CHUNK),
        grid=(b, HG_HEADS, 2),
        in_specs=[pl.BlockSpec((1, tc, HG_DIM), lambda bi, h, d: (bi, 0, HG_HEADS * (1 + d) + h)),
                  pl.BlockSpec((1, tc, HG_DIM), lambda bi, h, d: (bi, 0, 3 * HG_HEADS + h)),
                  pl.BlockSpec((1, 1, HG_DIM), lambda bi, h, d: (h, 0, 0)),
                  pl.BlockSpec((1, HG_CHUNK, HG_CHUNK), lambda bi, h, d: (d, 0, 0))],
        out_specs=pl.BlockSpec((1, 1, 1, HG_DIM, HG_DIM), lambda bi, h, d: (bi, h, d, 0, 0)),
        out_shape=jax.ShapeDtypeStruct((b, HG_HEADS, 2, HG_DIM, HG_DIM), F32),
        compiler_params=_params("arbitrary", "arbitrary", "arbitrary"),
        name="hgrn2_ctx_state",
    )(proj_c, proj_c, lb3, tri)

    cos, sin = _rope_tables(t)
    eye2 = jnp.eye(2, dtype=F32)
    ng3 = norm_g.astype(F32).reshape(HG_HEADS, 1, HG_DIM)
    hps = HG_HEADS_PER_STEP
    width = hps * HG_DIM
    out = None
    for d in (0, 1):
        tb = (lambda ti: ti) if d == 0 else (lambda ti: nt - 1 - ti)
        tri2 = jnp.kron(eye2, tri[d]).astype(BF16)
        col = lambda off, tb=tb: pl.BlockSpec((1, tblock, width),
                                               lambda bi, hp, ti: (bi, tb(ti), off // hps + hp))
        per_head = lambda: pl.BlockSpec((hps, 1, HG_DIM), lambda bi, hp, ti: (hp, 0, 0))
        in_specs = [col(0), col(HG_HEADS * (1 + d)), col(3 * HG_HEADS),
                    pl.BlockSpec((tblock, HG_DIM), lambda bi, hp, ti, tb=tb: (tb(ti), 0)),
                    pl.BlockSpec((tblock, HG_DIM), lambda bi, hp, ti, tb=tb: (tb(ti), 0)),
                    per_head(),
                    pl.BlockSpec((2 * HG_CHUNK, 2 * HG_CHUNK), lambda bi, hp, ti: (0, 0)),
                    pl.BlockSpec((1, hps, 1, HG_DIM, HG_DIM), lambda bi, hp, ti, d=d: (bi, hp, d, 0, 0))]
        args = [proj, proj, proj, cos, sin, lb3, tri2, s0]
        if d:
            in_specs += [col(0), col(4 * HG_HEADS), per_head()]
            args += [out, proj, ng3]
        out = pl.pallas_call(
            functools.partial(_gla_kernel, nchunk=tblock // HG_CHUNK, reverse=bool(d)),
            grid=(b, HG_HEADS // hps, nt),
            in_specs=in_specs,
            out_specs=col(0),
            out_shape=jax.ShapeDtypeStruct((b, t, HG_Q), BF16 if d else F32),
            scratch_shapes=[pltpu.VMEM((hps, HG_DIM, HG_DIM), F32)],
            compiler_params=_params("arbitrary", "arbitrary", "arbitrary"),
            name="hgrn2_gla_bwd" if d else "hgrn2_gla_fwd",
        )(*args)
    return out


def _router_kernel(h_ref, r_ref, o_ref):
    logits = lax.dot_general(r_ref[...], h_ref[0].astype(BF16), _NT, preferred_element_type=F32)
    p = jnp.exp(logits - jnp.max(logits, axis=0, keepdims=True))
    o_ref[0] = p / jnp.sum(p, axis=0, keepdims=True)


def _router_affinity(h, router_t, tt=512):
    b, t, d = h.shape
    tt = min(tt, t)
    return pl.pallas_call(
        _router_kernel,
        grid=(b, t // tt),
        in_specs=[pl.BlockSpec((1, tt, d), lambda bi, ti: (bi, ti, 0)),
                  pl.BlockSpec((N_EXPERTS, d), lambda bi, ti: (0, 0))],
        out_specs=pl.BlockSpec((1, N_EXPERTS, tt), lambda bi, ti: (bi, 0, ti)),
        out_shape=jax.ShapeDtypeStruct((b, N_EXPERTS, t), F32),
        compiler_params=_params("arbitrary", "arbitrary"),
        name="moe_router",
    )(h, router_t)


def _select_kernel(aff_ref, idx_ref, gate_ref, pos_ref, *, cap):
    e, r, _ = aff_ref.shape[1:]
    x = aff_ref[0]
    xb = pltpu.bitcast(x, jnp.int32)

    def count(mask):
        ones = jnp.where(mask, 1.0, 0.0)
        return jnp.sum(jnp.sum(ones, axis=1, keepdims=True), axis=2, keepdims=True)

    def search(_, bounds):
        lo, hi = bounds
        mid = lo + lax.shift_right_logical(hi - lo + 1, 1)
        ok = count(xb >= mid) >= cap
        return jnp.where(ok, mid, lo), jnp.where(ok, hi, mid - 1)

    inf_bits = 0x7F800000
    tau, _ = lax.fori_loop(0, 31, search, (jnp.zeros((e, 1, 1), jnp.int32),
                                           jnp.full((e, 1, 1), inf_bits, jnp.int32)))

    ki = lax.broadcasted_iota(jnp.int32, (128, 128), 0)
    li = lax.broadcasted_iota(jnp.int32, (128, 128), 1)
    upper = jnp.where(ki <= li, 1.0, 0.0).astype(BF16)
    ones = jnp.ones((128, 128), BF16)
    ri = lax.broadcasted_iota(jnp.int32, (r, r), 0)
    ci = lax.broadcasted_iota(jnp.int32, (r, r), 1)
    earlier_rows = jnp.where(ci < ri, 1.0, 0.0).astype(BF16)

    def running_count(mask):
        m2 = jnp.where(mask, 1.0, 0.0).reshape(e * r, 128).astype(BF16)
        within = jnp.dot(m2, upper, preferred_element_type=F32)
        row_tot = jnp.dot(m2, ones, preferred_element_type=F32).astype(BF16)
        before = [jnp.dot(earlier_rows, row_tot[i * r:(i + 1) * r], preferred_element_type=F32)
                  for i in range(e)]
        return (within + jnp.concatenate(before, axis=0)).reshape(e, r, 128)

    above = xb > tau
    tied = xb == tau
    need = cap - count(above)
    chosen = jnp.logical_or(above, jnp.logical_and(tied, running_count(tied) <= need))
    pos_ref[...] = jnp.where(chosen, running_count(chosen) - 1.0, -1.0)

    slot = lax.broadcasted_iota(jnp.int32, (cap, 128), 0).astype(F32)
    lane = lax.broadcasted_iota(jnp.int32, (1, 128), 1)
    for ei in range(e):
        def row_group(g, acc, ei=ei):
            rows = pl.ds(pl.multiple_of(g * 8, 8), 8)
            pos8 = pos_ref[ei, rows, :]
            aff8 = aff_ref[0, ei, rows, :]
            for i in range(8):
                onehot = jnp.where(slot == pos8[i:i + 1], 1.0, 0.0).astype(BF16)
                tok = (g * 8 + i) * 128 + lane
                gv = aff8[i:i + 1]
                g_hi = gv.astype(BF16).astype(F32)
                g_mid = (gv - g_hi).astype(BF16).astype(F32)
                g_lo = gv - g_hi - g_mid
                feats = jnp.concatenate(
                    [lax.shift_right_logical(tok, 6).astype(F32), (tok & 63).astype(F32),
                     g_hi, g_mid, g_lo, jnp.zeros((3, 128), F32)], axis=0).astype(BF16)
                acc = acc + lax.dot_general(feats, onehot, _NT, preferred_element_type=F32)
            return acc

        acc = lax.fori_loop(0, r // 8, row_group, jnp.zeros((8, cap), F32))
        idx_ref[0, ei:ei + 1, :] = (acc[0:1] * 64.0 + acc[1:2]).astype(jnp.int32)
        gate_ref[0, ei:ei + 1, :] = acc[2:3] + acc[3:4] + acc[4:5]


def _moe_select(aff, cap):
    b, e, t = aff.shape
    tpad = max(t, 1024)
    if tpad != t:
        aff = jnp.pad(aff, ((0, 0), (0, 0), (0, tpad - t)), constant_values=-1.0)
    r = tpad // 128
    return pl.pallas_call(
        functools.partial(_select_kernel, cap=cap),
        grid=(b,),
        in_specs=[pl.BlockSpec((1, e, r, 128), lambda bi: (bi, 0, 0, 0))],
        out_specs=[pl.BlockSpec((1, e, cap), lambda bi: (bi, 0, 0)),
                   pl.BlockSpec((1, e, cap), lambda bi: (bi, 0, 0))],
        out_shape=[jax.ShapeDtypeStruct((b, e, cap), jnp.int32),
                   jax.ShapeDtypeStruct((b, e, cap), F32)],
        scratch_shapes=[pltpu.VMEM((e, r, 128), F32)],
        compiler_params=_params("arbitrary"),
        name="moe_select",
    )(aff.reshape(b, e, r, 128))


def _dispatch_kernel(idx_ref, h_hbm, o_ref, buf_ref, sem, *, tm, steps_per_batch):
    i = pl.program_id(0)
    nsteps = pl.num_programs(0)

    def row_copy(step, r, slot):
        tok = idx_ref[step * tm + r]
        return pltpu.make_async_copy(h_hbm.at[lax.div(step, steps_per_batch), pl.ds(tok, 1), :],
                                     buf_ref.at[slot, pl.ds(r, 1), :], sem.at[slot])

    def start_rows(step):
        slot = lax.rem(step, 2)

        def body(r8, carry):
            for j in range(8):
                row_copy(step, r8 * 8 + j, slot).start(priority=j % 2)
            return carry

        lax.fori_loop(0, tm // 8, body, 0)

    @pl.when(i == 0)
    def _():
        start_rows(i)

    @pl.when(i + 1 < nsteps)
    def _():
        start_rows(i + 1)

    slot = lax.rem(i, 2)

    def wait_row(r, carry):
        row_copy(i, r, slot).wait()
        return carry

    lax.fori_loop(0, tm, wait_row, 0, unroll=8)
    o_ref[...] = buf_ref[slot].astype(o_ref.dtype)


def _moe_dispatch(h, idx, tm=256):
    b, t, d = h.shape
    _, e, cap = idx.shape
    tm = min(tm, cap)
    nsteps = b * e * cap // tm
    out = pl.pallas_call(
        functools.partial(_dispatch_kernel, tm=tm, steps_per_batch=e * cap // tm),
        grid_spec=pltpu.PrefetchScalarGridSpec(
            num_scalar_prefetch=1,
            grid=(nsteps,),
            in_specs=[pl.BlockSpec(memory_space=pl.ANY)],
            out_specs=pl.BlockSpec((tm, d), lambda i, idx_ref: (i, 0)),
            scratch_shapes=[pltpu.VMEM((2, tm, d), F32), pltpu.SemaphoreType.DMA((2,))],
        ),
        out_shape=jax.ShapeDtypeStruct((b * e * cap, d), BF16),
        compiler_params=_params("arbitrary"),
        name="moe_dispatch",
    )(idx.reshape(-1), h)
    return out.reshape(b, e, cap, d)


def _first_row_tile():
    return jnp.logical_and(pl.program_id(2) == 0, pl.program_id(3) == 0)


def _expert_up_kernel(x_ref, w1_ref, w3_ref, o_ref, w1b_ref, w3b_ref):
    @pl.when(_first_row_tile())
    def _():
        w1b_ref[...] = w1_ref[...].astype(BF16)
        w3b_ref[...] = w3_ref[...].astype(BF16)

    x = x_ref[0, 0]
    a = jnp.dot(x, w1b_ref[...], preferred_element_type=F32)
    g = jnp.dot(x, w3b_ref[...], preferred_element_type=F32)
    o_ref[0, 0] = (a * jax.nn.sigmoid(a) * g).astype(o_ref.dtype)


def _expert_down_kernel(idx_ref, h_ref, w2_ref, gate_ref, acc_hbm, out_hbm, w2b_ref, rows_ref,
                        gather_sem, scatter_sem, *, tm, ot, cap):
    del acc_hbm
    ei, ni, bi, mi = (pl.program_id(a) for a in range(4))
    nn, nb, nm = (pl.num_programs(a) for a in (1, 2, 3))
    q = (ni * nb + bi) * nm + mi
    slot = lax.rem(q, 2)
    base = (bi * pl.num_programs(0) + ei) * cap + mi * tm
    col = pl.multiple_of(ni * ot, ot)

    @pl.when(_first_row_tile())
    def _():
        w2b_ref[...] = w2_ref[...].astype(BF16)

    def gather(r, s):
        return pltpu.make_async_copy(out_hbm.at[bi, pl.ds(idx_ref[base + r], 1), pl.ds(col, ot)],
                                     rows_ref.at[s, pl.ds(r, 1), :], gather_sem.at[s])

    def scatter(r, s):
        return pltpu.make_async_copy(rows_ref.at[s, pl.ds(r, 1), :],
                                     out_hbm.at[bi, pl.ds(idx_ref[base + r], 1), pl.ds(col, ot)],
                                     scatter_sem.at[s])

    def for_rows(fn):
        def body(r8, carry):
            for j in range(8):
                fn(r8 * 8 + j, j % 2)
            return carry
        lax.fori_loop(0, tm // 8, body, 0)

    @pl.when(q >= 2)
    def _():
        for_rows(lambda r, p: scatter(r, slot).wait())

    for_rows(lambda r, p: gather(r, slot).start(priority=p))
    y = jnp.dot(h_ref[0, 0], w2b_ref[...], preferred_element_type=F32) * gate_ref[0, 0]
    for_rows(lambda r, p: gather(r, slot).wait())
    rows_ref[slot] = rows_ref[slot] + y
    for_rows(lambda r, p: scatter(r, slot).start(priority=p))

    @pl.when(q == nn * nb * nm - 1)
    def _():
        for_rows(lambda r, p: scatter(r, 1 - slot).wait())
        for_rows(lambda r, p: scatter(r, slot).wait())


def _expert_ffn(xin, gate, idx, w1, w3, w2, layer, t, tm_up=512, tm=256):
    b, e, cap, d = xin.shape
    ff = w1.shape[-1]
    tm_up, tm = min(tm_up, cap), min(tm, cap)
    ft, ot = min(MOE_FF_TILE, ff), min(MOE_OUT_TILE, d)
    assert ((d // ot) * b * (cap // tm)) % 2 == 0
    hid = pl.pallas_call(
        _expert_up_kernel,
        grid=(e, ff // ft, b, cap // tm_up),
        in_specs=[pl.BlockSpec((1, 1, tm_up, d), lambda ei, fi, bi, mi: (bi, ei, mi, 0)),
                  pl.BlockSpec((None, None, d, ft), lambda ei, fi, bi, mi: (layer, ei, 0, fi)),
                  pl.BlockSpec((None, None, d, ft), lambda ei, fi, bi, mi: (layer, ei, 0, fi))],
        out_specs=pl.BlockSpec((1, 1, tm_up, ft), lambda ei, fi, bi, mi: (bi, ei, mi, fi)),
        out_shape=jax.ShapeDtypeStruct((b, e, cap, ff), BF16),
        scratch_shapes=[pltpu.VMEM((d, ft), BF16), pltpu.VMEM((d, ft), BF16)],
        compiler_params=_params("arbitrary", "arbitrary", "arbitrary", "arbitrary"),
        name="moe_expert_up",
    )(xin, w1, w3)
    return pl.pallas_call(
        functools.partial(_expert_down_kernel, tm=tm, ot=ot, cap=cap),
        grid_spec=pltpu.PrefetchScalarGridSpec(
            num_scalar_prefetch=1,
            grid=(e, d // ot, b, cap // tm),
            in_specs=[pl.BlockSpec((1, 1, tm, ff), lambda ei, ni, bi, mi, idx_ref: (bi, ei, mi, 0)),
                      pl.BlockSpec((None, None, ff, ot), lambda ei, ni, bi, mi, idx_ref: (layer, ei, 0, ni)),
                      pl.BlockSpec((1, 1, tm, 1), lambda ei, ni, bi, mi, idx_ref: (bi, ei, mi, 0)),
                      pl.BlockSpec(memory_space=pl.ANY)],
            out_specs=pl.BlockSpec(memory_space=pl.ANY),
            scratch_shapes=[pltpu.VMEM((ff, ot), BF16), pltpu.VMEM((2, tm, ot), F32),
                            pltpu.SemaphoreType.DMA((2,)), pltpu.SemaphoreType.DMA((2,))],
        ),
        out_shape=jax.ShapeDtypeStruct((b, t, d), F32),
        input_output_aliases={4: 0},
        compiler_params=_params("arbitrary", "arbitrary", "arbitrary", "arbitrary"),
        name="moe_expert_down",
    )(idx.reshape(-1), hid, w2, gate[..., None], jnp.zeros((b, t, d), F32))


def _expert_choice_moe(h, router_t, w1, w3, w2, layer):
    b, t, d = h.shape
    cap = EC_CAPACITY_FACTOR * t // N_EXPERTS
    aff = _router_affinity(h, router_t)
    idx, gate = _moe_select(aff, cap)
    idx = jnp.clip(idx, 0, t - 1)
    xin = _moe_dispatch(h, idx)
    return _expert_ffn(xin, gate, idx, w1, w3, w2, layer, t)


def _project(h, w, out_dtype, name, col0=0, n=None):
    b, t, d = h.shape
    out = _matmul(h.reshape(b * t, d), w, 0, out_dtype, name, col0=col0, n=n)
    return out.reshape(b, t, out.shape[1])


def _mixer_na_s5(h, hc, w_in, w_out, rpb, s5_tables, glu_w, glu_b):
    b, t, _ = h.shape
    tc = hc.shape[1]
    qkv = _project(h, w_in, BF16, "ab_in_qkv", 0, 3 * NA_W)
    qkv_c = _project(hc, w_in, BF16, "ab_in_qkv_ctx", 0, 3 * NA_W)
    u = _project(h, w_in, F32, "ab_in_u", 3 * NA_W, S5_WIDTH)
    u_c = _project(hc, w_in, F32, "ab_in_u_ctx", 3 * NA_W, S5_WIDTH)
    o_na = _neighbourhood_attention(qkv, qkv_c, rpb)
    oc_na = _context_attention(qkv_c)
    y_s5, yc_s5 = _s5_scan(u, u_c, s5_tables)
    z = _s5_glu(y_s5.reshape(b * t, S5_WIDTH), glu_w, glu_b)
    z_c = _s5_glu(yc_s5.reshape(b * tc, S5_WIDTH), glu_w, glu_b)
    y = _matmul([o_na.reshape(b * t, NA_W), z], w_out, 0, F32, "ab_out")
    y_c = _matmul([oc_na.reshape(b * tc, NA_W), z_c], w_out, 0, F32, "ab_out_ctx")
    return y.reshape(b, t, -1), y_c.reshape(b, tc, -1)


def _mixer_hgrn2(h, hc, w_in, w_out, lb, norm_g):
    proj = _project(h, w_in, F32, "hg_in")
    proj_c = _project(hc, w_in, F32, "hg_in_ctx")
    return _project(_hgrn2_gla(proj, proj_c, lb, norm_g), w_out, F32, "hg_out")


def kernel(x, c, ctx, c_ctx, ada_w, ada_b, norm_g, ab_w_in, ab_w_out, na_rpb, s5_lam_re, s5_lam_im,
           s5_log_dt, s5_b_re, s5_b_im, s5_c_re, s5_c_im, s5_d, s5_glu_w, s5_glu_b, hg_w_in, hg_w_out,
           hg_lb_logits, hg_norm_g, moe_router, moe_w1, moe_w3, moe_w2):
    bsz, _, d = x.shape
    depth = ada_w.shape[0]
    assert depth == 2 and bsz + 1 <= 8
    assert ab_w_in.shape[0] == 1 and hg_w_in.shape[0] == 1
    xc = ctx
    cond = jax.nn.silu(c)
    cond_ctx = jax.nn.silu(c_ctx)[None]
    cond3 = jnp.concatenate([cond, cond_ctx, jnp.zeros((8 - bsz - 1, d), F32)], axis=0)
    hg_lb = jnp.cumsum(jax.nn.softmax(hg_lb_logits.astype(F32), axis=0), axis=0)
    moe_w = (moe_w1, moe_w3, moe_w2)
    ada_x, ada_c = [], []
    for layer in range(depth):
        ada = _matmul(cond3, ada_w, layer, F32, "adaln") + ada_b[layer][None]
        ada_x.append(jnp.split(ada[:bsz], 6, axis=-1))
        ada_c.append([jnp.broadcast_to(a, (bsz, d)) for a in jnp.split(ada[bsz:bsz + 1], 6, axis=-1)])
    ng0 = norm_g[0].astype(F32)
    h = _modulate(x, ng0[0], ada_x[0][0], ada_x[0][1])
    hc = _modulate(xc, ng0[0], ada_c[0][0], ada_c[0][1])
    for layer in range(depth):
        need_ctx = layer < depth - 1
        _, _, g1, sh2, sc2, g2 = ada_x[layer]
        _, _, cg1, csh2, csc2, cg2 = ada_c[layer]
        ng = norm_g[layer].astype(F32)
        if layer == 0:
            tables = _s5_tables(s5_lam_re[0], s5_lam_im[0], s5_log_dt[0], s5_b_re[0], s5_b_im[0],
                                s5_c_re[0], s5_c_im[0], s5_d[0])
            y, yc = _mixer_na_s5(h, hc, ab_w_in, ab_w_out, na_rpb[0], tables, s5_glu_w[0],
                                 s5_glu_b[0].astype(F32))
        else:
            y = _mixer_hgrn2(h, hc, hg_w_in, hg_w_out, hg_lb[0], hg_norm_g[0])
            yc = None
        router_t = moe_router[layer].T.astype(BF16)
        x, h2 = _gated_residual_modulate(x, y, ng[1], g1, ng[2], sh2, sc2, F32)
        moe = _expert_choice_moe(h2, router_t, *moe_w, layer)
        if not need_ctx:
            x = _gated_residual(x, moe, ng[3], g2)
            continue
        ng_next = norm_g[layer + 1].astype(F32)
        x, h = _gated_residual_modulate(x, moe, ng[3], g2, ng_next[0], ada_x[layer + 1][0],
                                        ada_x[layer + 1][1], BF16)
        xc, hc2 = _gated_residual_modulate(xc, yc, ng[1], cg1, ng[2], csh2, csc2, F32)
        moe_c = _expert_choice_moe(hc2, router_t, *moe_w, layer)
        xc, hc = _gated_residual_modulate(xc, moe_c, ng[3], cg2, ng_next[0], ada_c[layer + 1][0],
                                          ada_c[layer + 1][1], BF16)
    return x
```

```python
import functools
import math

import numpy as np
import jax
import jax.numpy as jnp
from jax import lax
from jax.experimental import pallas as pl
from jax.experimental.pallas import tpu as pltpu

F32 = jnp.float32
BF16 = jnp.bfloat16
HIGHEST = lax.Precision.HIGHEST

D_MODEL = 4096
GRID_W = 64
NORM_EPS = 1e-6
NA_HEADS = 16
NA_HEAD_DIM = 128
NA_WIN_R = 8
NA_WIN_C = 16
NA_W = NA_HEADS * NA_HEAD_DIM
S5_WIDTH = 2048
S5_GROUP = 16
S5_GROUPS = S5_WIDTH // S5_GROUP
S5_STATE = 64
S5_CHUNK = 16
HG_HEADS = 32
HG_DIM = 128
HG_CHUNK = 64
HG_Q = HG_HEADS * HG_DIM
HG_HEADS_PER_STEP = 2
ROPE_THETA = 10000.0
N_EXPERTS = 16
EXPERT_FF = 1024
EC_CAPACITY_FACTOR = 2

VMEM_LIMIT_BYTES = 56 * 1024 * 1024
MOE_FF_TILE = 512
MOE_OUT_TILE = 4096
NA_QROWS = 8
NA_KROWS = 16
NEG_BIG = -1e30

_NT = (((1,), (1,)), ((), ()))


def _params(*sem):
    return pltpu.CompilerParams(dimension_semantics=sem, vmem_limit_bytes=VMEM_LIMIT_BYTES)


def _mm_kernel(*refs, splits):
    *a_refs, w_ref, o_ref, wb_ref = refs

    @pl.when(pl.program_id(1) == 0)
    def _():
        wb_ref[...] = w_ref[...].astype(BF16)

    acc = None
    for a_ref, (k0, k1) in zip(a_refs, splits):
        part = jnp.dot(a_ref[...].astype(BF16), wb_ref[k0:k1, :], preferred_element_type=F32)
        acc = part if acc is None else acc + part
    o_ref[...] = acc.astype(o_ref.dtype)


def _matmul(a, w, layer, out_dtype, name, col0=0, n=None, tm=1024, tn=512):
    pieces = list(a) if isinstance(a, (list, tuple)) else [a]
    m = pieces[0].shape[0]
    k = w.shape[1]
    bounds = np.cumsum([0] + [piece.shape[1] for piece in pieces])
    assert bounds[-1] == k and all(piece.shape[0] == m for piece in pieces)
    n = w.shape[2] - col0 if n is None else n
    tm = min(tm, m)
    tn = min(tn, n)
    assert m % tm == 0 and n % tn == 0 and col0 % tn == 0, (m, n, col0, tm, tn)
    cb = col0 // tn
    return pl.pallas_call(
        functools.partial(_mm_kernel, splits=tuple(zip(bounds[:-1].tolist(), bounds[1:].tolist()))),
        grid=(n // tn, m // tm),
        in_specs=[pl.BlockSpec((tm, piece.shape[1]), lambda j, i: (i, 0)) for piece in pieces]
        + [pl.BlockSpec((None, k, tn), lambda j, i: (layer, 0, cb + j))],
        out_specs=pl.BlockSpec((tm, tn), lambda j, i: (i, j)),
        out_shape=jax.ShapeDtypeStruct((m, n), out_dtype),
        scratch_shapes=[pltpu.VMEM((k, tn), BF16)],
        compiler_params=_params("arbitrary", "arbitrary"),
        name=name,
    )(*pieces, w)


def _modulate_kernel(x_ref, g_ref, sh_ref, sc_ref, o_ref):
    x = x_ref[0]
    ms = jnp.mean(x * x, axis=-1, keepdims=True)
    y = x * lax.rsqrt(ms + NORM_EPS) * g_ref[...]
    o_ref[0] = (y * (1.0 + sc_ref[0]) + sh_ref[0]).astype(o_ref.dtype)


def _modulate(x, g, shift, scale, out_dtype=BF16, tt=256):
    b, t, d = x.shape
    tt = min(tt, t)
    row = lambda bi, ti: (bi, 0, 0)
    return pl.pallas_call(
        _modulate_kernel,
        grid=(b, t // tt),
        in_specs=[pl.BlockSpec((1, tt, d), lambda bi, ti: (bi, ti, 0)),
                  pl.BlockSpec((1, d), lambda bi, ti: (0, 0)),
                  pl.BlockSpec((1, 1, d), row),
                  pl.BlockSpec((1, 1, d), row)],
        out_specs=pl.BlockSpec((1, tt, d), lambda bi, ti: (bi, ti, 0)),
        out_shape=jax.ShapeDtypeStruct((b, t, d), out_dtype),
        compiler_params=_params("arbitrary", "arbitrary"),
        name="modulate",
    )(x, g.reshape(1, d), shift.reshape(b, 1, d), scale.reshape(b, 1, d))


def _residual_kernel(x_ref, y_ref, g_ref, gate_ref, o_ref):
    y = y_ref[0].astype(F32)
    ms = jnp.mean(y * y, axis=-1, keepdims=True)
    o_ref[0] = x_ref[0] + gate_ref[0] * (y * lax.rsqrt(ms + NORM_EPS) * g_ref[...])


def _gated_residual(x, y, g, gate, tt=256):
    b, t, d = x.shape
    tt = min(tt, t)
    blk = pl.BlockSpec((1, tt, d), lambda bi, ti: (bi, ti, 0))
    return pl.pallas_call(
        _residual_kernel,
        grid=(b, t // tt),
        in_specs=[blk, blk,
                  pl.BlockSpec((1, d), lambda bi, ti: (0, 0)),
                  pl.BlockSpec((1, 1, d), lambda bi, ti: (bi, 0, 0))],
        out_specs=blk,
        out_shape=jax.ShapeDtypeStruct((b, t, d), F32),
        compiler_params=_params("arbitrary", "arbitrary"),
        name="gated_residual",
    )(x, y, g.reshape(1, d), gate.reshape(b, 1, d))


def _residual_modulate_kernel(x_ref, y_ref, g_ref, gate_ref, g2_ref, sh_ref, sc_ref, xo_ref, ho_ref):
    y = y_ref[0].astype(F32)
    ms = jnp.mean(y * y, axis=-1, keepdims=True)
    x = x_ref[0] + gate_ref[0] * (y * lax.rsqrt(ms + NORM_EPS) * g_ref[...])
    xo_ref[0] = x
    ms2 = jnp.mean(x * x, axis=-1, keepdims=True)
    h = x * lax.rsqrt(ms2 + NORM_EPS) * g2_ref[...]
    ho_ref[0] = (h * (1.0 + sc_ref[0]) + sh_ref[0]).astype(ho_ref.dtype)


def _gated_residual_modulate(x, y, g, gate, g2, shift, scale, out_dtype, tt=256):
    b, t, d = x.shape
    tt = min(tt, t)
    blk = pl.BlockSpec((1, tt, d), lambda bi, ti: (bi, ti, 0))
    vec = pl.BlockSpec((1, d), lambda bi, ti: (0, 0))
    row = pl.BlockSpec((1, 1, d), lambda bi, ti: (bi, 0, 0))
    per_row = lambda a: a.reshape(b, 1, d)
    return pl.pallas_call(
        _residual_modulate_kernel,
        grid=(b, t // tt),
        in_specs=[blk, blk, vec, row, vec, row, row],
        out_specs=[blk, blk],
        out_shape=[jax.ShapeDtypeStruct((b, t, d), F32), jax.ShapeDtypeStruct((b, t, d), out_dtype)],
        compiler_params=_params("arbitrary", "arbitrary"),
        name="gated_residual_modulate",
    )(x, y, g.reshape(1, d), per_row(gate), g2.reshape(1, d), per_row(shift), per_row(scale))


def _na_bias_blocks(rpb):
    qc = np.arange(GRID_W)
    cs = np.clip(qc - NA_WIN_C // 2, 0, GRID_W - NA_WIN_C)
    kc = np.arange(GRID_W)
    in_c = (kc[None, :] >= cs[:, None]) & (kc[None, :] < cs[:, None] + NA_WIN_C)
    rel_c = kc[None, :] - qc[:, None] + NA_WIN_C - 1
    sel_c = (rel_c[:, :, None] == np.arange(2 * NA_WIN_C - 1)).astype(np.float32)
    z = jnp.einsum('hrc,xyc->hrxy', rpb.astype(F32), sel_c, precision=HIGHEST)
    z = jnp.where(in_c[None, None], z, NEG_BIG)
    neg = jnp.full_like(z, NEG_BIG)
    z_next = jnp.concatenate([z[:, 1:], neg[:, :1]], axis=1)
    cat = lambda left, right: jnp.concatenate([left, right], axis=-1)
    return jnp.stack([cat(z, z_next), cat(z, neg), cat(neg, z)], axis=1)


def _na_fill_bias(bias_ref, tiles_ref, j, rows):
    qr = NA_QROWS * j + np.arange(NA_QROWS)
    ws = int(np.clip(NA_QROWS * j - NA_WIN_R // 2, 0, rows - NA_KROWS))
    r0 = np.clip(qr - NA_WIN_R // 2, 0, rows - NA_WIN_R)
    visible = lambda a, kr: r0[a] <= kr < r0[a] + NA_WIN_R
    for a in range(NA_QROWS):
        for m in range(NA_KROWS // 2):
            ka, kb = ws + 2 * m, ws + 2 * m + 1
            rel = lambda kr: int(kr - qr[a] + NA_WIN_R - 1)
            if visible(a, ka) and visible(a, kb):
                tile = tiles_ref[0, 0, rel(ka)]
            elif visible(a, ka):
                tile = tiles_ref[0, 1, rel(ka)]
            elif visible(a, kb):
                tile = tiles_ref[0, 2, rel(kb)]
            else:
                tile = jnp.full((GRID_W, 2 * GRID_W), NEG_BIG, F32)
            bias_ref[GRID_W * a:GRID_W * (a + 1), 2 * GRID_W * m:2 * GRID_W * (m + 1)] = tile


def _na_kernel(q_ref, k_ref, v_ref, kc_ref, vc_ref, tiles_ref, o_ref, bias_ref, *, rows):
    j = pl.program_id(2)
    nblk = rows // NA_QROWS
    for j_static in (0, 1, nblk - 1):
        @pl.when(j == j_static)
        def _(j_static=j_static):
            _na_fill_bias(bias_ref, tiles_ref, j_static, rows)

    ws = jnp.clip(NA_QROWS * j - NA_WIN_R // 2, 0, rows - NA_KROWS) * GRID_W
    ws = pl.multiple_of(ws, 256)
    nk = NA_KROWS * GRID_W
    scale = NA_HEAD_DIM ** -0.5
    q = q_ref[0]
    kw = k_ref[0, pl.ds(ws, nk), :]
    vw = v_ref[0, pl.ds(ws, nk), :]
    s_nb = lax.dot_general(q, kw, _NT, preferred_element_type=F32) * scale + bias_ref[...]
    s_c = lax.dot_general(q, kc_ref[0], _NT, preferred_element_type=F32) * scale
    m = jnp.maximum(jnp.max(s_nb, axis=-1, keepdims=True), jnp.max(s_c, axis=-1, keepdims=True))
    p_nb = jnp.exp(s_nb - m)
    p_c = jnp.exp(s_c - m)
    denom = jnp.sum(p_nb, axis=-1, keepdims=True) + jnp.sum(p_c, axis=-1, keepdims=True)
    o = (jnp.dot(p_nb.astype(BF16), vw, preferred_element_type=F32)
         + jnp.dot(p_c.astype(BF16), vc_ref[0], preferred_element_type=F32))
    o_ref[0] = (o / denom).astype(o_ref.dtype)


def _neighbourhood_attention(qkv, qkv_c, rpb):
    b, t, _ = qkv.shape
    tc = qkv_c.shape[1]
    rows = t // GRID_W
    nblk = rows // NA_QROWS
    assert nblk >= 3
    tq = NA_QROWS * GRID_W
    tiles = _na_bias_blocks(rpb)
    return pl.pallas_call(
        functools.partial(_na_kernel, rows=rows),
        grid=(b, NA_HEADS, nblk),
        in_specs=[
            pl.BlockSpec((1, tq, NA_HEAD_DIM), lambda bi, h, j: (bi, j, h)),
            pl.BlockSpec((1, t, NA_HEAD_DIM), lambda bi, h, j: (bi, 0, NA_HEADS + h)),
            pl.BlockSpec((1, t, NA_HEAD_DIM), lambda bi, h, j: (bi, 0, 2 * NA_HEADS + h)),
            pl.BlockSpec((1, tc, NA_HEAD_DIM), lambda bi, h, j: (bi, 0, NA_HEADS + h)),
            pl.BlockSpec((1, tc, NA_HEAD_DIM), lambda bi, h, j: (bi, 0, 2 * NA_HEADS + h)),
            pl.BlockSpec((1,) + tiles.shape[1:], lambda bi, h, j: (h, 0, 0, 0, 0)),
        ],
        out_specs=pl.BlockSpec((1, tq, NA_HEAD_DIM), lambda bi, h, j: (bi, j, h)),
        out_shape=jax.ShapeDtypeStruct((b, t, NA_W), BF16),
        scratch_shapes=[pltpu.VMEM((tq, NA_KROWS * GRID_W), F32)],
        compiler_params=_params("arbitrary", "arbitrary", "arbitrary"),
        name="neighbourhood_attention",
    )(qkv, qkv, qkv, qkv_c, qkv_c, tiles)


def _ctx_attn_kernel(q_ref, k_ref, v_ref, o_ref):
    scale = NA_HEAD_DIM ** -0.5
    s = lax.dot_general(q_ref[0], k_ref[0], _NT, preferred_element_type=F32) * scale
    p = jnp.exp(s - jnp.max(s, axis=-1, keepdims=True))
    denom = jnp.sum(p, axis=-1, keepdims=True)
    o = jnp.dot(p.astype(BF16), v_ref[0], preferred_element_type=F32)
    o_ref[0] = (o / denom).astype(o_ref.dtype)


def _context_attention(qkv_c):
    b, tc, _ = qkv_c.shape
    blk = lambda off: pl.BlockSpec((1, tc, NA_HEAD_DIM), lambda bi, h: (bi, 0, off + h))
    return pl.pallas_call(
        _ctx_attn_kernel,
        grid=(b, NA_HEADS),
        in_specs=[blk(0), blk(NA_HEADS), blk(2 * NA_HEADS)],
        out_specs=blk(0),
        out_shape=jax.ShapeDtypeStruct((b, tc, NA_W), BF16),
        compiler_params=_params("arbitrary", "arbitrary"),
        name="context_attention",
    )(qkv_c, qkv_c, qkv_c)


def _s5_tables(lam_re, lam_im, log_dt, b_re, b_im, c_re, c_im, d_skip):
    ll, gi, p, g = S5_CHUNK, S5_GROUP, S5_STATE, S5_GROUPS
    f32 = F32
    lam = lax.complex(lam_re.astype(f32), lam_im.astype(f32))
    dt = jnp.exp(log_dt.astype(f32))[..., None]
    lam_dt = lam * dt
    lam_bar = jnp.exp(lam_dt)
    b_bar = ((lam_bar - 1.0) / lam)[..., None] * lax.complex(b_re.astype(f32), b_im.astype(f32))[None]
    c_mat = lax.complex(c_re.astype(f32), c_im.astype(f32))
    taus = jnp.arange(ll + 1, dtype=f32)
    pw = jnp.exp(lam_dt[None] * taus[:, None, None, None])

    def cmul_sum(a, b, spec):
        ar, ai, br, bi = jnp.real(a), jnp.imag(a), jnp.real(b), jnp.imag(b)
        e = lambda x, y: jnp.einsum(spec, x, y, precision=HIGHEST)
        return e(ar, br) - e(ai, bi), e(ar, bi) + e(ai, br)

    cp = c_mat[:, None] * jnp.moveaxis(pw[:ll], 0, 1)[:, :, :, None, :]
    k_re, _ = cmul_sum(cp, b_bar, 'dtgip,dgpj->dtgij')
    idx = np.arange(ll)
    tau_f = idx[None, :] - idx[:, None]
    kf = jnp.where((tau_f >= 0)[:, :, None, None, None], k_re[0][np.clip(tau_f, 0, ll - 1)], 0.0)
    kb = jnp.where((tau_f <= 0)[:, :, None, None, None], k_re[1][np.clip(-tau_f, 0, ll - 1)], 0.0)
    skip = (jnp.eye(ll, dtype=f32)[:, :, None, None, None]
            * (d_skip.astype(f32)[:, :, None] * jnp.eye(gi, dtype=f32))[None, None])
    t_sum = kf + kb + skip
    t_sum = jnp.transpose(t_sum, (2, 0, 4, 1, 3)).reshape(g, ll * gi, ll * gi)

    wf = pw[ll - 1 - idx, 0][:, :, :, None] * b_bar[0][None]
    wb = pw[idx, 1][:, :, :, None] * b_bar[1][None]
    to_rows = lambda w: jnp.transpose(w, (1, 0, 3, 2)).reshape(g, ll * gi, p)
    w_blocks = [to_rows(f(w)) for f in (jnp.real, jnp.imag) for w in (wf, wb)]
    m_in = jnp.concatenate([t_sum] + w_blocks, axis=-1)

    zf = c_mat[0][None] * pw[idx + 1, 0][:, :, None, :]
    zb = c_mat[1][None] * pw[ll - idx, 1][:, :, None, :]
    to_cols = lambda z: jnp.swapaxes(jnp.transpose(z, (1, 0, 2, 3)).reshape(g, ll * gi, p), 1, 2)
    z_blocks = [to_cols(f(z)) for f in (jnp.real, lambda v: -jnp.imag(v)) for z in (zf, zb)]
    z_out = jnp.concatenate(z_blocks, axis=1)

    lam_l = jnp.stack([jnp.concatenate([f(pw[ll, 0]), f(pw[ll, 1])], axis=-1)
                       for f in (jnp.real, jnp.imag)], axis=1)
    return m_in, z_out, lam_l


def _split_bf16(x):
    hi = x.astype(BF16)
    return hi, (x - hi.astype(F32)).astype(BF16)


def _dot3(a, b_hi, b_lo):
    a_hi, a_lo = _split_bf16(a)
    d = lambda p, q: jnp.dot(p, q, preferred_element_type=F32)
    return d(a_hi, b_hi) + d(a_lo, b_hi) + d(a_hi, b_lo)


def _s5_kernel(u_ref, uc_ref, mh_ref, ml_ref, zh_ref, zl_ref, lam_ref, y_ref, yc_ref,
               ug_ref, r_ref, s_ref, *, nbatch, nlat, nctx):
    ll, gi, p = S5_CHUNK, S5_GROUP, S5_STATE
    width = ll * gi
    per_vreg = 128 // gi
    nchunk = nctx + nlat
    g = pl.program_id(1)

    def group_rows(src_ref, bi, n):
        lane = lax.broadcasted_iota(jnp.int32, (n, 128), 1)
        cols = []
        for half in range(ll // per_vreg):
            acc = None
            for tt in range(per_vreg):
                ut = src_ref[bi, pl.ds(half * per_vreg + tt, n, stride=ll), :]
                rot = pltpu.roll(ut, jnp.mod(gi * (tt - g), 128), 1)
                keep = jnp.logical_and(lane >= gi * tt, lane < gi * (tt + 1))
                acc = rot if acc is None else jnp.where(keep, rot, acc)
            cols.append(acc)
        return jnp.concatenate(cols, axis=1)

    for bi in range(nbatch):
        ug_ref[bi * nchunk:bi * nchunk + nctx, :] = group_rows(uc_ref, bi, nctx)
        ug_ref[bi * nchunk + nctx:(bi + 1) * nchunk, :] = group_rows(u_ref, bi, nlat)

    r_ref[...] = _dot3(ug_ref[...], mh_ref[0], ml_ref[0])
    lam = lam_ref[0]
    lam_r, lam_i = lam[0:1], lam[1:2]
    fwd_lane = lax.broadcasted_iota(jnp.int32, (1, 2 * p), 1) < p
    v_re, v_im = slice(width, width + 2 * p), slice(width + 2 * p, width + 4 * p)

    sub = 8
    ngrp, nctx_g = nchunk // sub, nctx // sub

    def body(kg, carry):
        gb = jnp.where(kg < nctx_g, nctx_g - 1 - kg, ngrp - 1 - (kg - nctx_g))
        out = []
        for bi in range(nbatch):
            sr, si = carry[bi]
            rf = pl.ds(pl.multiple_of(bi * nchunk + kg * sub, sub), sub)
            rb = pl.ds(pl.multiple_of(bi * nchunk + gb * sub, sub), sub)
            vfr, vfi = r_ref[rf, v_re], r_ref[rf, v_im]
            vbr, vbi = r_ref[rb, v_re], r_ref[rb, v_im]
            before_r, before_i = [], []
            for i in range(sub):
                before_r.append(sr)
                before_i.append(si)
                j = sub - 1 - i
                vr = jnp.where(fwd_lane, vfr[i:i + 1], vbr[j:j + 1])
                vi = jnp.where(fwd_lane, vfi[i:i + 1], vbi[j:j + 1])
                sr, si = lam_r * sr - lam_i * si + vr, lam_r * si + lam_i * sr + vi
            s_ref[rf, 0:p] = jnp.concatenate(before_r, axis=0)[:, 0:p]
            s_ref[rb, p:2 * p] = jnp.concatenate(before_r[::-1], axis=0)[:, p:2 * p]
            s_ref[rf, 2 * p:3 * p] = jnp.concatenate(before_i, axis=0)[:, 0:p]
            s_ref[rb, 3 * p:4 * p] = jnp.concatenate(before_i[::-1], axis=0)[:, p:2 * p]
            out.append((sr, si))
        return tuple(out)

    zero = jnp.zeros((1, 2 * p), F32)
    lax.fori_loop(0, ngrp, body, tuple((zero, zero) for _ in range(nbatch)))
    yg = r_ref[:, 0:width] + _dot3(s_ref[...], zh_ref[0], zl_ref[0])

    @pl.when(g == 0)
    def _():
        y_ref[...] = jnp.zeros(y_ref.shape, F32)
        yc_ref[...] = jnp.zeros(yc_ref.shape, F32)

    def scatter_rows(dst_ref, bi, row0, n):
        lane = lax.broadcasted_iota(jnp.int32, (n, 128), 1)
        mine = jnp.logical_and(lane >= gi * g, lane < gi * (g + 1))
        for t in range(ll):
            half, tt = divmod(t, per_vreg)
            piece = yg[row0:row0 + n, 128 * half:128 * (half + 1)]
            rot = pltpu.roll(piece, jnp.mod(gi * (g - tt), 128), 1)
            rows = pl.ds(t, n, stride=ll)
            dst_ref[bi, rows, :] = jnp.where(mine, rot, dst_ref[bi, rows, :])

    for bi in range(nbatch):
        scatter_rows(yc_ref, bi, bi * nchunk, nctx)
        scatter_rows(y_ref, bi, bi * nchunk + nctx, nlat)


def _s5_scan(u, u_c, tables):
    m_in, z_out, lam_l = tables
    b, t, _ = u.shape
    tc = u_c.shape[1]
    ll, gi, g, p = S5_CHUNK, S5_GROUP, S5_GROUPS, S5_STATE
    nlat, nctx = t // ll, tc // ll
    rows = b * (nlat + nctx)
    width = ll * gi
    per_slab = 128 // gi
    m_hi, m_lo = _split_bf16(m_in)
    z_hi, z_lo = _split_bf16(z_out)
    wspec = lambda shape: pl.BlockSpec((1,) + shape, lambda s, j: (s * per_slab + j, 0, 0))
    slab = lambda n: pl.BlockSpec((b, n, 128), lambda s, j: (0, 0, s))
    return pl.pallas_call(
        functools.partial(_s5_kernel, nbatch=b, nlat=nlat, nctx=nctx),
        grid=(g // per_slab, per_slab),
        in_specs=[slab(t), slab(tc),
                  wspec((width, width + 4 * p)), wspec((width, width + 4 * p)),
                  wspec((4 * p, width)), wspec((4 * p, width)), wspec((2, 2 * p))],
        out_specs=[slab(t), slab(tc)],
        out_shape=[jax.ShapeDtypeStruct((b, t, g * gi), F32), jax.ShapeDtypeStruct((b, tc, g * gi), F32)],
        scratch_shapes=[pltpu.VMEM((rows, width), F32), pltpu.VMEM((rows, width + 4 * p), F32),
                        pltpu.VMEM((rows, 4 * p), F32)],
        compiler_params=_params("arbitrary", "arbitrary"),
        name="s5_scan",
    )(u, u_c, m_hi, m_lo, z_hi, z_lo, lam_l)


def _glu_kernel(y_ref, w_ref, b_ref, o_ref, wb_ref):
    @pl.when(pl.program_id(0) == 0)
    def _():
        wb_ref[...] = w_ref[...].astype(BF16)

    z = jax.nn.gelu(y_ref[...])
    a = jnp.dot(z.astype(BF16), wb_ref[...], preferred_element_type=F32) + b_ref[...]
    o_ref[...] = (z * jax.nn.sigmoid(a)).astype(o_ref.dtype)


def _s5_glu(y, w, bias, tm=256):
    m, n = y.shape
    tm = min(tm, m)
    return pl.pallas_call(
        _glu_kernel,
        grid=(m // tm,),
        in_specs=[pl.BlockSpec((tm, n), lambda i: (i, 0)),
                  pl.BlockSpec((n, n), lambda i: (0, 0)),
                  pl.BlockSpec((1, n), lambda i: (0, 0))],
        out_specs=pl.BlockSpec((tm, n), lambda i: (i, 0)),
        out_shape=jax.ShapeDtypeStruct((m, n), BF16),
        scratch_shapes=[pltpu.VMEM((n, n), BF16)],
        compiler_params=_params("arbitrary"),
        name="s5_glu",
    )(y, w, bias.reshape(1, n))


def _gla_state_kernel(f_ref, v_ref, lb_ref, tri_ref, s_ref, *, nchunk):
    d = pl.program_id(2)
    lb = lb_ref[0]
    st = jnp.zeros((HG_DIM, HG_DIM), F32)
    for c in range(nchunk):
        cc = jnp.where(d == 0, c, nchunk - 1 - c)
        sl = pl.ds(pl.multiple_of(cc * HG_CHUNK, HG_CHUNK), HG_CHUNK)
        f = lb + (1.0 - lb) * jax.nn.sigmoid(f_ref[0, sl, :])
        logf = jnp.log(f)
        b = jnp.dot(tri_ref[0], logf, precision=HIGHEST, preferred_element_type=F32)
        tot = jnp.sum(logf, axis=0, keepdims=True)
        kd = ((1.0 - f) * jnp.exp(tot - b)).astype(BF16)
        st = st * jnp.exp(tot) + jnp.dot(v_ref[0, sl, :].T.astype(BF16), kd, preferred_element_type=F32)
    s_ref[0, 0, 0] = st


def _gla_kernel(q_ref, f_ref, v_ref, cos_ref, sin_ref, lb_ref, tri_ref, s0_ref, *rest, nchunk, reverse):
    if reverse:
        ofwd_ref, gate_ref, ng_ref, o_ref, st_ref, osum_ref = rest
    else:
        o_ref, st_ref = rest
    heads = q_ref.shape[2] // HG_DIM

    @pl.when(pl.program_id(2) == 0)
    def _():
        for h in range(heads):
            st_ref[h] = s0_ref[0, h, 0]

    pair = 2 * HG_CHUNK
    tri = tri_ref[...]
    keep = tri > 0
    quarter = HG_DIM // 4
    lane = lax.broadcasted_iota(jnp.int32, (pair, HG_DIM), 1)
    st = [st_ref[h] for h in range(heads)]
    pairs = range(nchunk // 2)
    for p in (reversed(pairs) if reverse else pairs):
        chunks = (2 * p + 1, 2 * p) if reverse else (2 * p, 2 * p + 1)
        rows = [slice(HG_CHUNK * c, HG_CHUNK * (c + 1)) for c in chunks]
        whole = slice(pair * p, pair * (p + 1))
        gather = ((lambda ref, *ix: ref[ix[:-1] + (whole,) + ix[-1:]]) if not reverse else
                  (lambda ref, *ix: jnp.concatenate([ref[ix[:-1] + (r,) + ix[-1:]] for r in rows], axis=0)))
        cos, sin = gather(cos_ref, slice(None)), gather(sin_ref, slice(None))
        for h in range(heads):
            hl = slice(HG_DIM * h, HG_DIM * (h + 1))
            load = lambda ref: gather(ref, 0, hl)
            lb = lb_ref[h]
            f = lb + (1.0 - lb) * jax.nn.sigmoid(load(f_ref))
            logf = jnp.log(f)
            k = 1.0 - f
            v = load(v_ref)
            sg = jax.nn.sigmoid(load(q_ref))
            partner = jnp.where((lane & quarter) == 0, pltpu.roll(sg, HG_DIM - quarter, 1),
                                pltpu.roll(sg, quarter, 1))
            q = sg * cos + partner * sin
            hi, lo = _split_bf16(logf)
            b = jnp.dot(tri, hi, preferred_element_type=F32) + jnp.dot(tri, lo, preferred_element_type=F32)
            qe = (q * jnp.exp(b)).astype(BF16)
            ke = (k * jnp.exp(-b)).astype(BF16)
            att = jnp.where(keep, lax.dot_general(qe, ke, _NT, preferred_element_type=F32), 0.0)
            o_intra = jnp.dot(att.astype(BF16), v.astype(BF16), preferred_element_type=F32)
            o_inter = []
            for half in range(2):
                hs = slice(HG_CHUNK * half, HG_CHUNK * (half + 1))
                tot = jnp.sum(logf[hs], axis=0, keepdims=True)
                kd = (k[hs] * jnp.exp(tot - b[hs])).astype(BF16)
                o_inter.append(lax.dot_general(qe[hs], st[h].astype(BF16), _NT,
                                               preferred_element_type=F32))
                st[h] = st[h] * jnp.exp(tot) + jnp.dot(v[hs].T.astype(BF16), kd,
                                                       preferred_element_type=F32)
            o = o_intra + jnp.concatenate(o_inter, axis=0)
            if reverse:
                for half, r in enumerate(rows):
                    hs = slice(HG_CHUNK * half, HG_CHUNK * (half + 1))
                    osum_ref[r, hl] = o[hs] + ofwd_ref[0, r, hl]
            else:
                o_ref[0, whole, hl] = o
    for h in range(heads):
        st_ref[h] = st[h]
    if reverse:
        for h in range(heads):
            hl = slice(HG_DIM * h, HG_DIM * (h + 1))
            o = osum_ref[:, hl]
            ms = jnp.mean(o * o, axis=-1, keepdims=True)
            gate = gate_ref[0, :, hl]
            o = o * lax.rsqrt(ms + NORM_EPS) * ng_ref[h] * (gate * jax.nn.sigmoid(gate))
            o_ref[0, :, hl] = o.astype(o_ref.dtype)


def _rope_tables(t):
    pos = np.arange(t)
    quarter = HG_DIM // 4
    inv_freq = np.float32(ROPE_THETA) ** (-np.arange(quarter, dtype=np.float32) / np.float32(quarter))

    def tab(p):
        ang = p.astype(np.float32)[:, None] * inv_freq[None, :]
        c, s = np.cos(ang, dtype=np.float32), np.sin(ang, dtype=np.float32)
        return np.concatenate([c, c], axis=-1), np.concatenate([-s, s], axis=-1)

    c_r, s_r = tab(pos // GRID_W)
    c_c, s_c = tab(pos % GRID_W)
    return (jnp.asarray(np.concatenate([c_r, c_c], axis=-1)),
            jnp.asarray(np.concatenate([s_r, s_c], axis=-1)))


def _gla_masks():
    i = np.arange(HG_CHUNK)
    lower = (i[:, None] >= i[None, :]).astype(np.float32)
    return jnp.asarray(np.stack([lower, lower.T]))


def _hgrn2_gla(proj, proj_c, lb, norm_g, tblock=2048):
    b, t, _ = proj.shape
    tc = proj_c.shape[1]
    tblock = min(tblock, t)
    nt = t // tblock
    tri = _gla_masks()
    lb3 = lb.astype(F32).reshape(HG_HEADS, 1, HG_DIM)

    s0 = pl.pallas_call(
        functools.partial(_gla_state_kernel, nchunk=tc // HG_CHUNK),
        grid=(b, HG_HEADS, 2),
        in_specs=[pl.BlockSpec((1, tc, HG_DIM), lambda bi, h, d: (bi, 0, HG_HEADS * (1 + d) + h)),
                  pl.BlockSpec((1, tc, HG_DIM), lambda bi, h, d: (bi, 0, 3 * HG_HEADS + h)),
                  pl.BlockSpec((1, 1, HG_DIM), lambda bi, h, d: (h, 0, 0)),
                  pl.BlockSpec((1, HG_CHUNK, HG_CHUNK), lambda bi, h, d: (d, 0, 0))],
        out_specs=pl.BlockSpec((1, 1, 1, HG_DIM, HG_DIM), lambda bi, h, d: (bi, h, d, 0, 0)),
        out_shape=jax.ShapeDtypeStruct((b, HG_HEADS, 2, HG_DIM, HG_DIM), F32),
        compiler_params=_params("arbitrary", "arbitrary", "arbitrary"),
        name="hgrn2_ctx_state",
    )(proj_c, proj_c, lb3, tri)

    cos, sin = _rope_tables(t)
    eye2 = jnp.eye(2, dtype=F32)
    ng3 = norm_g.astype(F32).reshape(HG_HEADS, 1, HG_DIM)
    hps = HG_HEADS_PER_STEP
    width = hps * HG_DIM
    out = None
    for d in (0, 1):
        tb = (lambda ti: ti) if d == 0 else (lambda ti: nt - 1 - ti)
        tri2 = jnp.kron(eye2, tri[d]).astype(BF16)
        col = lambda off, tb=tb: pl.BlockSpec((1, tblock, width),
                                               lambda bi, hp, ti: (bi, tb(ti), off // hps + hp))
        per_head = lambda: pl.BlockSpec((hps, 1, HG_DIM), lambda bi, hp, ti: (hp, 0, 0))
        in_specs = [col(0), col(HG_HEADS * (1 + d)), col(3 * HG_HEADS),
                    pl.BlockSpec((tblock, HG_DIM), lambda bi, hp, ti, tb=tb: (tb(ti), 0)),
                    pl.BlockSpec((tblock, HG_DIM), lambda bi, hp, ti, tb=tb: (tb(ti), 0)),
                    per_head(),
                    pl.BlockSpec((2 * HG_CHUNK, 2 * HG_CHUNK), lambda bi, hp, ti: (0, 0)),
                    pl.BlockSpec((1, hps, 1, HG_DIM, HG_DIM), lambda bi, hp, ti, d=d: (bi, hp, d, 0, 0))]
        args = [proj, proj, proj, cos, sin, lb3, tri2, s0]
        if d:
            in_specs += [col(0), col(4 * HG_HEADS), per_head()]
            args += [out, proj, ng3]
        out = pl.pallas_call(
            functools.partial(_gla_kernel, nchunk=tblock // HG_CHUNK, reverse=bool(d)),
            grid=(b, HG_HEADS // hps, nt),
            in_specs=in_specs,
            out_specs=col(0),
            out_shape=jax.ShapeDtypeStruct((b, t, HG_Q), BF16 if d else F32),
            scratch_shapes=[pltpu.VMEM((hps, HG_DIM, HG_DIM), F32)]
            + ([pltpu.VMEM((tblock, width), F32)] if d else []),
            compiler_params=_params("arbitrary", "arbitrary", "arbitrary"),
            name="hgrn2_gla_bwd" if d else "hgrn2_gla_fwd",
        )(*args)
    return out


def _router_kernel(h_ref, r_ref, o_ref):
    logits = lax.dot_general(r_ref[...], h_ref[0].astype(BF16), _NT, preferred_element_type=F32)
    p = jnp.exp(logits - jnp.max(logits, axis=0, keepdims=True))
    o_ref[0] = p / jnp.sum(p, axis=0, keepdims=True)


def _router_affinity(h, router_t, tt=512):
    b, t, d = h.shape
    tt = min(tt, t)
    return pl.pallas_call(
        _router_kernel,
        grid=(b, t // tt),
        in_specs=[pl.BlockSpec((1, tt, d), lambda bi, ti: (bi, ti, 0)),
                  pl.BlockSpec((N_EXPERTS, d), lambda bi, ti: (0, 0))],
        out_specs=pl.BlockSpec((1, N_EXPERTS, tt), lambda bi, ti: (bi, 0, ti)),
        out_shape=jax.ShapeDtypeStruct((b, N_EXPERTS, t), F32),
        compiler_params=_params("arbitrary", "arbitrary"),
        name="moe_router",
    )(h, router_t)


def _select_kernel(aff_ref, idx_ref, gate_ref, pos_ref, *, cap):
    e, r, _ = aff_ref.shape[1:]
    x = aff_ref[0]
    xb = pltpu.bitcast(x, jnp.int32)

    def count(mask):
        ones = jnp.where(mask, 1.0, 0.0)
        return jnp.sum(jnp.sum(ones, axis=1, keepdims=True), axis=2, keepdims=True)

    def search(_, bounds):
        lo, hi = bounds
        mid = lo + lax.shift_right_logical(hi - lo + 1, 1)
        ok = count(xb >= mid) >= cap
        return jnp.where(ok, mid, lo), jnp.where(ok, hi, mid - 1)

    inf_bits = 0x7F800000
    tau, _ = lax.fori_loop(0, 31, search, (jnp.zeros((e, 1, 1), jnp.int32),
                                           jnp.full((e, 1, 1), inf_bits, jnp.int32)))

    ki = lax.broadcasted_iota(jnp.int32, (128, 128), 0)
    li = lax.broadcasted_iota(jnp.int32, (128, 128), 1)
    upper = jnp.where(ki <= li, 1.0, 0.0).astype(BF16)
    ones = jnp.ones((128, 128), BF16)
    ri = lax.broadcasted_iota(jnp.int32, (r, r), 0)
    ci = lax.broadcasted_iota(jnp.int32, (r, r), 1)
    earlier_rows = jnp.where(ci < ri, 1.0, 0.0).astype(BF16)

    def running_count(mask):
        m2 = jnp.where(mask, 1.0, 0.0).reshape(e * r, 128).astype(BF16)
        within = jnp.dot(m2, upper, preferred_element_type=F32)
        row_tot = jnp.dot(m2, ones, preferred_element_type=F32).astype(BF16)
        before = [jnp.dot(earlier_rows, row_tot[i * r:(i + 1) * r], preferred_element_type=F32)
                  for i in range(e)]
        return (within + jnp.concatenate(before, axis=0)).reshape(e, r, 128)

    above = xb > tau
    tied = xb == tau
    need = cap - count(above)
    chosen = jnp.logical_or(above, jnp.logical_and(tied, running_count(tied) <= need))
    pos_ref[...] = jnp.where(chosen, running_count(chosen) - 1.0, -1.0)

    slot = lax.broadcasted_iota(jnp.int32, (cap, 128), 0).astype(F32)
    lane = lax.broadcasted_iota(jnp.int32, (1, 128), 1)
    for ei in range(e):
        def row_group(g, acc, ei=ei):
            rows = pl.ds(pl.multiple_of(g * 8, 8), 8)
            pos8 = pos_ref[ei, rows, :]
            aff8 = aff_ref[0, ei, rows, :]
            for i in range(8):
                onehot = jnp.where(slot == pos8[i:i + 1], 1.0, 0.0).astype(BF16)
                tok = (g * 8 + i) * 128 + lane
                gv = aff8[i:i + 1]
                g_hi = gv.astype(BF16).astype(F32)
                g_mid = (gv - g_hi).astype(BF16).astype(F32)
                g_lo = gv - g_hi - g_mid
                feats = jnp.concatenate(
                    [lax.shift_right_logical(tok, 6).astype(F32), (tok & 63).astype(F32),
                     g_hi, g_mid, g_lo, jnp.zeros((3, 128), F32)], axis=0).astype(BF16)
                acc = acc + lax.dot_general(feats, onehot, _NT, preferred_element_type=F32)
            return acc

        acc = lax.fori_loop(0, r // 8, row_group, jnp.zeros((8, cap), F32))
        idx_ref[0, ei:ei + 1, :] = (acc[0:1] * 64.0 + acc[1:2]).astype(jnp.int32)
        gate_ref[0, ei:ei + 1, :] = acc[2:3] + acc[3:4] + acc[4:5]


def _moe_select(aff, cap):
    b, e, t = aff.shape
    tpad = max(t, 1024)
    if tpad != t:
        aff = jnp.pad(aff, ((0, 0), (0, 0), (0, tpad - t)), constant_values=-1.0)
    r = tpad // 128
    return pl.pallas_call(
        functools.partial(_select_kernel, cap=cap),
        grid=(b,),
        in_specs=[pl.BlockSpec((1, e, r, 128), lambda bi: (bi, 0, 0, 0))],
        out_specs=[pl.BlockSpec((1, e, cap), lambda bi: (bi, 0, 0)),
                   pl.BlockSpec((1, e, cap), lambda bi: (bi, 0, 0))],
        out_shape=[jax.ShapeDtypeStruct((b, e, cap), jnp.int32),
                   jax.ShapeDtypeStruct((b, e, cap), F32)],
        scratch_shapes=[pltpu.VMEM((e, r, 128), F32)],
        compiler_params=_params("arbitrary"),
        name="moe_select",
    )(aff.reshape(b, e, r, 128))


def _dispatch_kernel(idx_ref, h_hbm, o_ref, buf_ref, sem, *, tm, steps_per_batch):
    i = pl.program_id(0)
    nsteps = pl.num_programs(0)

    def row_copy(step, r, slot):
        tok = idx_ref[step * tm + r]
        return pltpu.make_async_copy(h_hbm.at[lax.div(step, steps_per_batch), pl.ds(tok, 1), :],
                                     buf_ref.at[slot, pl.ds(r, 1), :], sem.at[slot])

    def start_rows(step):
        slot = lax.rem(step, 2)

        def body(r8, carry):
            for j in range(8):
                row_copy(step, r8 * 8 + j, slot).start(priority=j % 2)
            return carry

        lax.fori_loop(0, tm // 8, body, 0)

    @pl.when(i == 0)
    def _():
        start_rows(i)

    @pl.when(i + 1 < nsteps)
    def _():
        start_rows(i + 1)

    slot = lax.rem(i, 2)

    def wait_row(r, carry):
        row_copy(i, r, slot).wait()
        return carry

    lax.fori_loop(0, tm, wait_row, 0, unroll=8)
    o_ref[...] = buf_ref[slot].astype(o_ref.dtype)


def _moe_dispatch(h, idx, tm=256):
    b, t, d = h.shape
    _, e, cap = idx.shape
    tm = min(tm, cap)
    nsteps = b * e * cap // tm
    out = pl.pallas_call(
        functools.partial(_dispatch_kernel, tm=tm, steps_per_batch=e * cap // tm),
        grid_spec=pltpu.PrefetchScalarGridSpec(
            num_scalar_prefetch=1,
            grid=(nsteps,),
            in_specs=[pl.BlockSpec(memory_space=pl.ANY)],
            out_specs=pl.BlockSpec((tm, d), lambda i, idx_ref: (i, 0)),
            scratch_shapes=[pltpu.VMEM((2, tm, d), F32), pltpu.SemaphoreType.DMA((2,))],
        ),
        out_shape=jax.ShapeDtypeStruct((b * e * cap, d), BF16),
        compiler_params=_params("arbitrary"),
        name="moe_dispatch",
    )(idx.reshape(-1), h)
    return out.reshape(b, e, cap, d)


def _first_row_tile():
    return jnp.logical_and(pl.program_id(2) == 0, pl.program_id(3) == 0)


def _expert_up_kernel(x_ref, w1_ref, w3_ref, o_ref, w1b_ref, w3b_ref):
    @pl.when(_first_row_tile())
    def _():
        w1b_ref[...] = w1_ref[...].astype(BF16)
        w3b_ref[...] = w3_ref[...].astype(BF16)

    x = x_ref[0, 0]
    a = jnp.dot(x, w1b_ref[...], preferred_element_type=F32)
    g = jnp.dot(x, w3b_ref[...], preferred_element_type=F32)
    o_ref[0, 0] = (a * jax.nn.sigmoid(a) * g).astype(o_ref.dtype)


def _expert_down_kernel(idx_ref, h_ref, w2_ref, gate_ref, acc_hbm, out_hbm, w2b_ref, rows_ref,
                        gather_sem, scatter_sem, *, tm, ot, cap):
    del acc_hbm
    ei, ni, bi, mi = (pl.program_id(a) for a in range(4))
    nn, nb, nm = (pl.num_programs(a) for a in (1, 2, 3))
    q = (ni * nb + bi) * nm + mi
    slot = lax.rem(q, 2)
    base = (bi * pl.num_programs(0) + ei) * cap + mi * tm
    col = pl.multiple_of(ni * ot, ot)

    @pl.when(_first_row_tile())
    def _():
        w2b_ref[...] = w2_ref[...].astype(BF16)

    def gather(r, s):
        return pltpu.make_async_copy(out_hbm.at[bi, pl.ds(idx_ref[base + r], 1), pl.ds(col, ot)],
                                     rows_ref.at[s, pl.ds(r, 1), :], gather_sem.at[s])

    def scatter(r, s):
        return pltpu.make_async_copy(rows_ref.at[s, pl.ds(r, 1), :],
                                     out_hbm.at[bi, pl.ds(idx_ref[base + r], 1), pl.ds(col, ot)],
                                     scatter_sem.at[s])

    def for_rows(fn):
        def body(r8, carry):
            for j in range(8):
                fn(r8 * 8 + j, j % 2)
            return carry
        lax.fori_loop(0, tm // 8, body, 0)

    @pl.when(q >= 2)
    def _():
        for_rows(lambda r, p: scatter(r, slot).wait())

    for_rows(lambda r, p: gather(r, slot).start(priority=p))
    y = jnp.dot(h_ref[0, 0], w2b_ref[...], preferred_element_type=F32) * gate_ref[0, 0]
    for_rows(lambda r, p: gather(r, slot).wait())
    rows_ref[slot] = rows_ref[slot] + y
    for_rows(lambda r, p: scatter(r, slot).start(priority=p))

    @pl.when(q == nn * nb * nm - 1)
    def _():
        for_rows(lambda r, p: scatter(r, 1 - slot).wait())
        for_rows(lambda r, p: scatter(r, slot).wait())


def _expert_ffn(xin, gate, idx, w1, w3, w2, layer, t, tm_up=512, tm=256):
    b, e, cap, d = xin.shape
    ff = w1.shape[-1]
    tm_up, tm = min(tm_up, cap), min(tm, cap)
    ft, ot = min(MOE_FF_TILE, ff), min(MOE_OUT_TILE, d)
    assert ((d // ot) * b * (cap // tm)) % 2 == 0
    hid = pl.pallas_call(
        _expert_up_kernel,
        grid=(e, ff // ft, b, cap // tm_up),
        in_specs=[pl.BlockSpec((1, 1, tm_up, d), lambda ei, fi, bi, mi: (bi, ei, mi, 0)),
                  pl.BlockSpec((None, None, d, ft), lambda ei, fi, bi, mi: (layer, ei, 0, fi)),
                  pl.BlockSpec((None, None, d, ft), lambda ei, fi, bi, mi: (layer, ei, 0, fi))],
        out_specs=pl.BlockSpec((1, 1, tm_up, ft), lambda ei, fi, bi, mi: (bi, ei, mi, fi)),
        out_shape=jax.ShapeDtypeStruct((b, e, cap, ff), BF16),
        scratch_shapes=[pltpu.VMEM((d, ft), BF16), pltpu.VMEM((d, ft), BF16)],
        compiler_params=_params("arbitrary", "arbitrary", "arbitrary", "arbitrary"),
        name="moe_expert_up",
    )(xin, w1, w3)
    return pl.pallas_call(
        functools.partial(_expert_down_kernel, tm=tm, ot=ot, cap=cap),
        grid_spec=pltpu.PrefetchScalarGridSpec(
            num_scalar_prefetch=1,
            grid=(e, d // ot, b, cap // tm),
            in_specs=[pl.BlockSpec((1, 1, tm, ff), lambda ei, ni, bi, mi, idx_ref: (bi, ei, mi, 0)),
                      pl.BlockSpec((None, None, ff, ot), lambda ei, ni, bi, mi, idx_ref: (layer, ei, 0, ni)),
                      pl.BlockSpec((1, 1, tm, 1), lambda ei, ni, bi, mi, idx_ref: (bi, ei, mi, 0)),
                      pl.BlockSpec(memory_space=pl.ANY)],
            out_specs=pl.BlockSpec(memory_space=pl.ANY),
            scratch_shapes=[pltpu.VMEM((ff, ot), BF16), pltpu.VMEM((2, tm, ot), F32),
                            pltpu.SemaphoreType.DMA((2,)), pltpu.SemaphoreType.DMA((2,))],
        ),
        out_shape=jax.ShapeDtypeStruct((b, t, d), F32),
        input_output_aliases={4: 0},
        compiler_params=_params("arbitrary", "arbitrary", "arbitrary", "arbitrary"),
        name="moe_expert_down",
    )(idx.reshape(-1), hid, w2, gate[..., None], jnp.zeros((b, t, d), F32))


def _expert_choice_moe(h, router_t, w1, w3, w2, layer):
    b, t, d = h.shape
    cap = EC_CAPACITY_FACTOR * t // N_EXPERTS
    aff = _router_affinity(h, router_t)
    idx, gate = _moe_select(aff, cap)
    idx = jnp.clip(idx, 0, t - 1)
    xin = _moe_dispatch(h, idx)
    return _expert_ffn(xin, gate, idx, w1, w3, w2, layer, t)


def _project(h, w, out_dtype, name, col0=0, n=None):
    b, t, d = h.shape
    out = _matmul(h.reshape(b * t, d), w, 0, out_dtype, name, col0=col0, n=n)
    return out.reshape(b, t, out.shape[1])


def _mixer_na_s5(h, hc, w_in, w_out, rpb, s5_tables, glu_w, glu_b):
    b, t, _ = h.shape
    tc = hc.shape[1]
    qkv = _project(h, w_in, BF16, "ab_in_qkv", 0, 3 * NA_W)
    qkv_c = _project(hc, w_in, BF16, "ab_in_qkv_ctx", 0, 3 * NA_W)
    u = _project(h, w_in, F32, "ab_in_u", 3 * NA_W, S5_WIDTH)
    u_c = _project(hc, w_in, F32, "ab_in_u_ctx", 3 * NA_W, S5_WIDTH)
    o_na = _neighbourhood_attention(qkv, qkv_c, rpb)
    oc_na = _context_attention(qkv_c)
    y_s5, yc_s5 = _s5_scan(u, u_c, s5_tables)
    z = _s5_glu(y_s5.reshape(b * t, S5_WIDTH), glu_w, glu_b)
    z_c = _s5_glu(yc_s5.reshape(b * tc, S5_WIDTH), glu_w, glu_b)
    y = _matmul([o_na.reshape(b * t, NA_W), z], w_out, 0, F32, "ab_out")
    y_c = _matmul([oc_na.reshape(b * tc, NA_W), z_c], w_out, 0, F32, "ab_out_ctx")
    return y.reshape(b, t, -1), y_c.reshape(b, tc, -1)


def _mixer_hgrn2(h, hc, w_in, w_out, lb, norm_g):
    proj = _project(h, w_in, F32, "hg_in")
    proj_c = _project(hc, w_in, F32, "hg_in_ctx")
    return _project(_hgrn2_gla(proj, proj_c, lb, norm_g), w_out, F32, "hg_out")


def kernel(x, c, ctx, c_ctx, ada_w, ada_b, norm_g, ab_w_in, ab_w_out, na_rpb, s5_lam_re, s5_lam_im,
           s5_log_dt, s5_b_re, s5_b_im, s5_c_re, s5_c_im, s5_d, s5_glu_w, s5_glu_b, hg_w_in, hg_w_out,
           hg_lb_logits, hg_norm_g, moe_router, moe_w1, moe_w3, moe_w2):
    bsz, _, d = x.shape
    depth = ada_w.shape[0]
    assert depth == 2 and bsz + 1 <= 8
    assert ab_w_in.shape[0] == 1 and hg_w_in.shape[0] == 1
    xc = ctx
    cond = jax.nn.silu(c)
    cond_ctx = jax.nn.silu(c_ctx)[None]
    cond3 = jnp.concatenate([cond, cond_ctx, jnp.zeros((8 - bsz - 1, d), F32)], axis=0)
    hg_lb = jnp.cumsum(jax.nn.softmax(hg_lb_logits.astype(F32), axis=0), axis=0)
    moe_w = (moe_w1, moe_w3, moe_w2)
    ada_x, ada_c = [], []
    for layer in range(depth):
        ada = _matmul(cond3, ada_w, layer, F32, "adaln") + ada_b[layer][None]
        ada_x.append(jnp.split(ada[:bsz], 6, axis=-1))
        ada_c.append([jnp.broadcast_to(a, (bsz, d)) for a in jnp.split(ada[bsz:bsz + 1], 6, axis=-1)])
    ng0 = norm_g[0].astype(F32)
    h = _modulate(x, ng0[0], ada_x[0][0], ada_x[0][1])
    hc = _modulate(xc, ng0[0], ada_c[0][0], ada_c[0][1])
    for layer in range(depth):
        need_ctx = layer < depth - 1
        _, _, g1, sh2, sc2, g2 = ada_x[layer]
        _, _, cg1, csh2, csc2, cg2 = ada_c[layer]
        ng = norm_g[layer].astype(F32)
        if layer == 0:
            tables = _s5_tables(s5_lam_re[0], s5_lam_im[0], s5_log_dt[0], s5_b_re[0], s5_b_im[0],
                                s5_c_re[0], s5_c_im[0], s5_d[0])
            y, yc = _mixer_na_s5(h, hc, ab_w_in, ab_w_out, na_rpb[0], tables, s5_glu_w[0],
                                 s5_glu_b[0].astype(F32))
        else:
            y = _mixer_hgrn2(h, hc, hg_w_in, hg_w_out, hg_lb[0], hg_norm_g[0])
            yc = None
        router_t = moe_router[layer].T.astype(BF16)
        x, h2 = _gated_residual_modulate(x, y, ng[1], g1, ng[2], sh2, sc2, F32)
        moe = _expert_choice_moe(h2, router_t, *moe_w, layer)
        if not need_ctx:
            x = _gated_residual(x, moe, ng[3], g2)
            continue
        ng_next = norm_g[layer + 1].astype(F32)
        x, h = _gated_residual_modulate(x, moe, ng[3], g2, ng_next[0], ada_x[layer + 1][0],
                                        ada_x[layer + 1][1], BF16)
        xc, hc2 = _gated_residual_modulate(xc, yc, ng[1], cg1, ng[2], csh2, csc2, F32)
        moe_c = _expert_choice_moe(hc2, router_t, *moe_w, layer)
        xc, hc = _gated_residual_modulate(xc, moe_c, ng[3], cg2, ng_next[0], ada_c[layer + 1][0],
                                          ada_c[layer + 1][1], BF16)
    return x
```

```python
import functools
import math

import numpy as np
import jax
import jax.numpy as jnp
from jax import lax
from jax.experimental import pallas as pl
from jax.experimental.pallas import tpu as pltpu

F32 = jnp.float32
BF16 = jnp.bfloat16
HIGHEST = lax.Precision.HIGHEST

D_MODEL = 4096
GRID_W = 64
NORM_EPS = 1e-6
NA_HEADS = 16
NA_HEAD_DIM = 128
NA_WIN_R = 8
NA_WIN_C = 16
NA_W = NA_HEADS * NA_HEAD_DIM
S5_WIDTH = 2048
S5_GROUP = 16
S5_GROUPS = S5_WIDTH // S5_GROUP
S5_STATE = 64
S5_CHUNK = 16
HG_HEADS = 32
HG_DIM = 128
HG_CHUNK = 64
HG_Q = HG_HEADS * HG_DIM
HG_HEADS_PER_STEP = 2
ROPE_THETA = 10000.0
N_EXPERTS = 16
EXPERT_FF = 1024
EC_CAPACITY_FACTOR = 2

VMEM_LIMIT_BYTES = 56 * 1024 * 1024
MOE_FF_TILE = 512
MOE_OUT_TILE = 4096
NA_QROWS = 8
NA_KROWS = 16
NEG_BIG = -1e30

_NT = (((1,), (1,)), ((), ()))


def _params(*sem):
    return pltpu.CompilerParams(dimension_semantics=sem, vmem_limit_bytes=VMEM_LIMIT_BYTES)


def _mm_kernel(*refs, splits):
    *a_refs, w_ref, o_ref, wb_ref = refs

    @pl.when(pl.program_id(1) == 0)
    def _():
        wb_ref[...] = w_ref[...].astype(BF16)

    acc = None
    for a_ref, (k0, k1) in zip(a_refs, splits):
        part = jnp.dot(a_ref[...].astype(BF16), wb_ref[k0:k1, :], preferred_element_type=F32)
        acc = part if acc is None else acc + part
    o_ref[...] = acc.astype(o_ref.dtype)


def _matmul(a, w, layer, out_dtype, name, col0=0, n=None, tm=1024, tn=512):
    pieces = list(a) if isinstance(a, (list, tuple)) else [a]
    m = pieces[0].shape[0]
    k = w.shape[1]
    bounds = np.cumsum([0] + [piece.shape[1] for piece in pieces])
    assert bounds[-1] == k and all(piece.shape[0] == m for piece in pieces)
    n = w.shape[2] - col0 if n is None else n
    tm = min(tm, m)
    tn = min(tn, n)
    assert m % tm == 0 and n % tn == 0 and col0 % tn == 0, (m, n, col0, tm, tn)
    cb = col0 // tn
    return pl.pallas_call(
        functools.partial(_mm_kernel, splits=tuple(zip(bounds[:-1].tolist(), bounds[1:].tolist()))),
        grid=(n // tn, m // tm),
        in_specs=[pl.BlockSpec((tm, piece.shape[1]), lambda j, i: (i, 0)) for piece in pieces]
        + [pl.BlockSpec((None, k, tn), lambda j, i: (layer, 0, cb + j))],
        out_specs=pl.BlockSpec((tm, tn), lambda j, i: (i, j)),
        out_shape=jax.ShapeDtypeStruct((m, n), out_dtype),
        scratch_shapes=[pltpu.VMEM((k, tn), BF16)],
        compiler_params=_params("arbitrary", "arbitrary"),
        name=name,
    )(*pieces, w)


def _modulate_kernel(x_ref, g_ref, sh_ref, sc_ref, o_ref):
    x = x_ref[0]
    ms = jnp.mean(x * x, axis=-1, keepdims=True)
    y = x * lax.rsqrt(ms + NORM_EPS) * g_ref[...]
    o_ref[0] = (y * (1.0 + sc_ref[0]) + sh_ref[0]).astype(o_ref.dtype)


def _modulate(x, g, shift, scale, out_dtype=BF16, tt=256):
    b, t, d = x.shape
    tt = min(tt, t)
    row = lambda bi, ti: (bi, 0, 0)
    return pl.pallas_call(
        _modulate_kernel,
        grid=(b, t // tt),
        in_specs=[pl.BlockSpec((1, tt, d), lambda bi, ti: (bi, ti, 0)),
                  pl.BlockSpec((1, d), lambda bi, ti: (0, 0)),
                  pl.BlockSpec((1, 1, d), row),
                  pl.BlockSpec((1, 1, d), row)],
        out_specs=pl.BlockSpec((1, tt, d), lambda bi, ti: (bi, ti, 0)),
        out_shape=jax.ShapeDtypeStruct((b, t, d), out_dtype),
        compiler_params=_params("arbitrary", "arbitrary"),
        name="modulate",
    )(x, g.reshape(1, d), shift.reshape(b, 1, d), scale.reshape(b, 1, d))


def _residual_kernel(x_ref, y_ref, g_ref, gate_ref, o_ref):
    y = y_ref[0].astype(F32)
    ms = jnp.mean(y * y, axis=-1, keepdims=True)
    o_ref[0] = x_ref[0] + gate_ref[0] * (y * lax.rsqrt(ms + NORM_EPS) * g_ref[...])


def _gated_residual(x, y, g, gate, tt=256):
    b, t, d = x.shape
    tt = min(tt, t)
    blk = pl.BlockSpec((1, tt, d), lambda bi, ti: (bi, ti, 0))
    return pl.pallas_call(
        _residual_kernel,
        grid=(b, t // tt),
        in_specs=[blk, blk,
                  pl.BlockSpec((1, d), lambda bi, ti: (0, 0)),
                  pl.BlockSpec((1, 1, d), lambda bi, ti: (bi, 0, 0))],
        out_specs=blk,
        out_shape=jax.ShapeDtypeStruct((b, t, d), F32),
        compiler_params=_params("arbitrary", "arbitrary"),
        name="gated_residual",
    )(x, y, g.reshape(1, d), gate.reshape(b, 1, d))


def _residual_modulate_kernel(x_ref, y_ref, g_ref, gate_ref, g2_ref, sh_ref, sc_ref, xo_ref, ho_ref):
    y = y_ref[0].astype(F32)
    ms = jnp.mean(y * y, axis=-1, keepdims=True)
    x = x_ref[0] + gate_ref[0] * (y * lax.rsqrt(ms + NORM_EPS) * g_ref[...])
    xo_ref[0] = x
    ms2 = jnp.mean(x * x, axis=-1, keepdims=True)
    h = x * lax.rsqrt(ms2 + NORM_EPS) * g2_ref[...]
    ho_ref[0] = (h * (1.0 + sc_ref[0]) + sh_ref[0]).astype(ho_ref.dtype)


def _gated_residual_modulate(x, y, g, gate, g2, shift, scale, out_dtype, tt=256):
    b, t, d = x.shape
    tt = min(tt, t)
    blk = pl.BlockSpec((1, tt, d), lambda bi, ti: (bi, ti, 0))
    vec = pl.BlockSpec((1, d), lambda bi, ti: (0, 0))
    row = pl.BlockSpec((1, 1, d), lambda bi, ti: (bi, 0, 0))
    per_row = lambda a: a.reshape(b, 1, d)
    return pl.pallas_call(
        _residual_modulate_kernel,
        grid=(b, t // tt),
        in_specs=[blk, blk, vec, row, vec, row, row],
        out_specs=[blk, blk],
        out_shape=[jax.ShapeDtypeStruct((b, t, d), F32), jax.ShapeDtypeStruct((b, t, d), out_dtype)],
        compiler_params=_params("arbitrary", "arbitrary"),
        name="gated_residual_modulate",
    )(x, y, g.reshape(1, d), per_row(gate), g2.reshape(1, d), per_row(shift), per_row(scale))


def _na_bias_blocks(rpb):
    qc = np.arange(GRID_W)
    cs = np.clip(qc - NA_WIN_C // 2, 0, GRID_W - NA_WIN_C)
    kc = np.arange(GRID_W)
    in_c = (kc[None, :] >= cs[:, None]) & (kc[None, :] < cs[:, None] + NA_WIN_C)
    rel_c = kc[None, :] - qc[:, None] + NA_WIN_C - 1
    sel_c = (rel_c[:, :, None] == np.arange(2 * NA_WIN_C - 1)).astype(np.float32)
    z = jnp.einsum('hrc,xyc->hrxy', rpb.astype(F32), sel_c, precision=HIGHEST)
    z = jnp.where(in_c[None, None], z, NEG_BIG)
    neg = jnp.full_like(z, NEG_BIG)
    z_next = jnp.concatenate([z[:, 1:], neg[:, :1]], axis=1)
    cat = lambda left, right: jnp.concatenate([left, right], axis=-1)
    return jnp.stack([cat(z, z_next), cat(z, neg), cat(neg, z)], axis=1)


def _na_fill_bias(bias_ref, tiles_ref, j, rows):
    qr = NA_QROWS * j + np.arange(NA_QROWS)
    ws = int(np.clip(NA_QROWS * j - NA_WIN_R // 2, 0, rows - NA_KROWS))
    r0 = np.clip(qr - NA_WIN_R // 2, 0, rows - NA_WIN_R)
    visible = lambda a, kr: r0[a] <= kr < r0[a] + NA_WIN_R
    for a in range(NA_QROWS):
        for m in range(NA_KROWS // 2):
            ka, kb = ws + 2 * m, ws + 2 * m + 1
            rel = lambda kr: int(kr - qr[a] + NA_WIN_R - 1)
            if visible(a, ka) and visible(a, kb):
                tile = tiles_ref[0, 0, rel(ka)]
            elif visible(a, ka):
                tile = tiles_ref[0, 1, rel(ka)]
            elif visible(a, kb):
                tile = tiles_ref[0, 2, rel(kb)]
            else:
                tile = jnp.full((GRID_W, 2 * GRID_W), NEG_BIG, F32)
            bias_ref[GRID_W * a:GRID_W * (a + 1), 2 * GRID_W * m:2 * GRID_W * (m + 1)] = tile


def _na_kernel(q_ref, k_ref, v_ref, kc_ref, vc_ref, tiles_ref, o_ref, bias_ref, *, rows):
    j = pl.program_id(2)
    nblk = rows // NA_QROWS
    for j_static in (0, 1, nblk - 1):
        @pl.when(j == j_static)
        def _(j_static=j_static):
            _na_fill_bias(bias_ref, tiles_ref, j_static, rows)

    ws = jnp.clip(NA_QROWS * j - NA_WIN_R // 2, 0, rows - NA_KROWS) * GRID_W
    ws = pl.multiple_of(ws, 256)
    nk = NA_KROWS * GRID_W
    scale = NA_HEAD_DIM ** -0.5
    q = q_ref[0]
    kw = k_ref[0, pl.ds(ws, nk), :]
    vw = v_ref[0, pl.ds(ws, nk), :]
    s_nb = lax.dot_general(q, kw, _NT, preferred_element_type=F32) * scale + bias_ref[...]
    s_c = lax.dot_general(q, kc_ref[0], _NT, preferred_element_type=F32) * scale
    m = jnp.maximum(jnp.max(s_nb, axis=-1, keepdims=True), jnp.max(s_c, axis=-1, keepdims=True))
    p_nb = jnp.exp(s_nb - m)
    p_c = jnp.exp(s_c - m)
    denom = jnp.sum(p_nb, axis=-1, keepdims=True) + jnp.sum(p_c, axis=-1, keepdims=True)
    o = (jnp.dot(p_nb.astype(BF16), vw, preferred_element_type=F32)
         + jnp.dot(p_c.astype(BF16), vc_ref[0], preferred_element_type=F32))
    o_ref[0] = (o / denom).astype(o_ref.dtype)


def _neighbourhood_attention(qkv, qkv_c, rpb):
    b, t, _ = qkv.shape
    tc = qkv_c.shape[1]
    rows = t // GRID_W
    nblk = rows // NA_QROWS
    assert nblk >= 3
    tq = NA_QROWS * GRID_W
    tiles = _na_bias_blocks(rpb)
    return pl.pallas_call(
        functools.partial(_na_kernel, rows=rows),
        grid=(b, NA_HEADS, nblk),
        in_specs=[
            pl.BlockSpec((1, tq, NA_HEAD_DIM), lambda bi, h, j: (bi, j, h)),
            pl.BlockSpec((1, t, NA_HEAD_DIM), lambda bi, h, j: (bi, 0, NA_HEADS + h)),
            pl.BlockSpec((1, t, NA_HEAD_DIM), lambda bi, h, j: (bi, 0, 2 * NA_HEADS + h)),
            pl.BlockSpec((1, tc, NA_HEAD_DIM), lambda bi, h, j: (bi, 0, NA_HEADS + h)),
            pl.BlockSpec((1, tc, NA_HEAD_DIM), lambda bi, h, j: (bi, 0, 2 * NA_HEADS + h)),
            pl.BlockSpec((1,) + tiles.shape[1:], lambda bi, h, j: (h, 0, 0, 0, 0)),
        ],
        out_specs=pl.BlockSpec((1, tq, NA_HEAD_DIM), lambda bi, h, j: (bi, j, h)),
        out_shape=jax.ShapeDtypeStruct((b, t, NA_W), BF16),
        scratch_shapes=[pltpu.VMEM((tq, NA_KROWS * GRID_W), F32)],
        compiler_params=_params("arbitrary", "arbitrary", "arbitrary"),
        name="neighbourhood_attention",
    )(qkv, qkv, qkv, qkv_c, qkv_c, tiles)


def _ctx_attn_kernel(q_ref, k_ref, v_ref, o_ref):
    scale = NA_HEAD_DIM ** -0.5
    s = lax.dot_general(q_ref[0], k_ref[0], _NT, preferred_element_type=F32) * scale
    p = jnp.exp(s - jnp.max(s, axis=-1, keepdims=True))
    denom = jnp.sum(p, axis=-1, keepdims=True)
    o = jnp.dot(p.astype(BF16), v_ref[0], preferred_element_type=F32)
    o_ref[0] = (o / denom).astype(o_ref.dtype)


def _context_attention(qkv_c):
    b, tc, _ = qkv_c.shape
    blk = lambda off: pl.BlockSpec((1, tc, NA_HEAD_DIM), lambda bi, h: (bi, 0, off + h))
    return pl.pallas_call(
        _ctx_attn_kernel,
        grid=(b, NA_HEADS),
        in_specs=[blk(0), blk(NA_HEADS), blk(2 * NA_HEADS)],
        out_specs=blk(0),
        out_shape=jax.ShapeDtypeStruct((b, tc, NA_W), BF16),
        compiler_params=_params("arbitrary", "arbitrary"),
        name="context_attention",
    )(qkv_c, qkv_c, qkv_c)


def _s5_tables(lam_re, lam_im, log_dt, b_re, b_im, c_re, c_im, d_skip):
    ll, gi, p, g = S5_CHUNK, S5_GROUP, S5_STATE, S5_GROUPS
    f32 = F32
    lam = lax.complex(lam_re.astype(f32), lam_im.astype(f32))
    dt = jnp.exp(log_dt.astype(f32))[..., None]
    lam_dt = lam * dt
    lam_bar = jnp.exp(lam_dt)
    b_bar = ((lam_bar - 1.0) / lam)[..., None] * lax.complex(b_re.astype(f32), b_im.astype(f32))[None]
    c_mat = lax.complex(c_re.astype(f32), c_im.astype(f32))
    taus = jnp.arange(ll + 1, dtype=f32)
    pw = jnp.exp(lam_dt[None] * taus[:, None, None, None])

    def cmul_sum(a, b, spec):
        ar, ai, br, bi = jnp.real(a), jnp.imag(a), jnp.real(b), jnp.imag(b)
        e = lambda x, y: jnp.einsum(spec, x, y, precision=HIGHEST)
        return e(ar, br) - e(ai, bi), e(ar, bi) + e(ai, br)

    cp = c_mat[:, None] * jnp.moveaxis(pw[:ll], 0, 1)[:, :, :, None, :]
    k_re, _ = cmul_sum(cp, b_bar, 'dtgip,dgpj->dtgij')
    kt = jnp.swapaxes(k_re, -1, -2)
    skip = d_skip.astype(f32)[:, :, None] * jnp.eye(gi, dtype=f32)
    signed = jnp.concatenate([kt[1][:0:-1], (kt[0][0] + kt[1][0] + skip)[None], kt[0][1:]], axis=0)
    idx = np.arange(ll)
    lag = idx[None, :] - idx[:, None] + ll - 1
    sel = (lag[:, :, None] == np.arange(2 * ll - 1)).astype(np.float32)
    t_sum = jnp.einsum('stu,ugji->gsjti', sel, signed, precision=HIGHEST).reshape(g, ll * gi, ll * gi)

    wf = pw[:ll, 0][::-1][:, :, :, None] * b_bar[0][None]
    wb = pw[:ll, 1][:, :, :, None] * b_bar[1][None]
    to_rows = lambda w: jnp.transpose(w, (1, 0, 3, 2)).reshape(g, ll * gi, p)
    w_blocks = [to_rows(f(w)) for f in (jnp.real, jnp.imag) for w in (wf, wb)]
    m_in = jnp.concatenate([t_sum] + w_blocks, axis=-1)

    zf = c_mat[0][None] * pw[1:, 0][:, :, None, :]
    zb = c_mat[1][None] * pw[1:, 1][::-1][:, :, None, :]
    to_cols = lambda z: jnp.swapaxes(jnp.transpose(z, (1, 0, 2, 3)).reshape(g, ll * gi, p), 1, 2)
    z_blocks = [to_cols(f(z)) for f in (jnp.real, lambda v: -jnp.imag(v)) for z in (zf, zb)]
    z_out = jnp.concatenate(z_blocks, axis=1)

    lam_l = jnp.stack([jnp.concatenate([f(pw[ll, 0]), f(pw[ll, 1])], axis=-1)
                       for f in (jnp.real, jnp.imag)], axis=1)
    return m_in, z_out, lam_l


def _split_bf16(x):
    hi = x.astype(BF16)
    return hi, (x - hi.astype(F32)).astype(BF16)


def _dot3(a, b_hi, b_lo):
    a_hi, a_lo = _split_bf16(a)
    d = lambda p, q: jnp.dot(p, q, preferred_element_type=F32)
    return d(a_hi, b_hi) + d(a_lo, b_hi) + d(a_hi, b_lo)


def _s5_kernel(u_ref, uc_ref, mh_ref, ml_ref, zh_ref, zl_ref, lam_ref, y_ref, yc_ref,
               ug_ref, r_ref, s_ref, *, nbatch, nlat, nctx):
    ll, gi, p = S5_CHUNK, S5_GROUP, S5_STATE
    width = ll * gi
    per_vreg = 128 // gi
    nchunk = nctx + nlat
    g = pl.program_id(1)

    def group_rows(src_ref, bi, n):
        lane = lax.broadcasted_iota(jnp.int32, (n, 128), 1)
        cols = []
        for half in range(ll // per_vreg):
            acc = None
            for tt in range(per_vreg):
                ut = src_ref[bi, half * per_vreg + tt]
                rot = pltpu.roll(ut, jnp.mod(gi * (tt - g), 128), 1)
                keep = jnp.logical_and(lane >= gi * tt, lane < gi * (tt + 1))
                acc = rot if acc is None else jnp.where(keep, rot, acc)
            cols.append(acc)
        return jnp.concatenate(cols, axis=1)

    for bi in range(nbatch):
        ug_ref[bi * nchunk:bi * nchunk + nctx, :] = group_rows(uc_ref, bi, nctx)
        ug_ref[bi * nchunk + nctx:(bi + 1) * nchunk, :] = group_rows(u_ref, bi, nlat)

    r_ref[...] = _dot3(ug_ref[...], mh_ref[0], ml_ref[0])
    lam = lam_ref[0]
    lam_r, lam_i = lam[0:1], lam[1:2]
    fwd_lane = lax.broadcasted_iota(jnp.int32, (1, 2 * p), 1) < p
    v_re, v_im = slice(width, width + 2 * p), slice(width + 2 * p, width + 4 * p)

    sub = 8
    ngrp, nctx_g = nchunk // sub, nctx // sub

    def body(kg, carry):
        gb = jnp.where(kg < nctx_g, nctx_g - 1 - kg, ngrp - 1 - (kg - nctx_g))
        out = []
        for bi in range(nbatch):
            sr, si = carry[bi]
            rf = pl.ds(pl.multiple_of(bi * nchunk + kg * sub, sub), sub)
            rb = pl.ds(pl.multiple_of(bi * nchunk + gb * sub, sub), sub)
            vfr, vfi = r_ref[rf, v_re], r_ref[rf, v_im]
            vbr, vbi = r_ref[rb, v_re], r_ref[rb, v_im]
            before_r, before_i = [], []
            for i in range(sub):
                before_r.append(sr)
                before_i.append(si)
                j = sub - 1 - i
                vr = jnp.where(fwd_lane, vfr[i:i + 1], vbr[j:j + 1])
                vi = jnp.where(fwd_lane, vfi[i:i + 1], vbi[j:j + 1])
                sr, si = lam_r * sr - lam_i * si + vr, lam_r * si + lam_i * sr + vi
            s_ref[rf, 0:p] = jnp.concatenate(before_r, axis=0)[:, 0:p]
            s_ref[rb, p:2 * p] = jnp.concatenate(before_r[::-1], axis=0)[:, p:2 * p]
            s_ref[rf, 2 * p:3 * p] = jnp.concatenate(before_i, axis=0)[:, 0:p]
            s_ref[rb, 3 * p:4 * p] = jnp.concatenate(before_i[::-1], axis=0)[:, p:2 * p]
            out.append((sr, si))
        return tuple(out)

    zero = jnp.zeros((1, 2 * p), F32)
    lax.fori_loop(0, ngrp, body, tuple((zero, zero) for _ in range(nbatch)))
    yg = r_ref[:, 0:width] + _dot3(s_ref[...], zh_ref[0], zl_ref[0])

    @pl.when(g == 0)
    def _():
        y_ref[...] = jnp.zeros(y_ref.shape, F32)
        yc_ref[...] = jnp.zeros(yc_ref.shape, F32)

    def scatter_rows(dst_ref, bi, row0, n):
        lane = lax.broadcasted_iota(jnp.int32, (n, 128), 1)
        mine = jnp.logical_and(lane >= gi * g, lane < gi * (g + 1))
        for t in range(ll):
            half, tt = divmod(t, per_vreg)
            piece = yg[row0:row0 + n, 128 * half:128 * (half + 1)]
            rot = pltpu.roll(piece, jnp.mod(gi * (g - tt), 128), 1)
            dst_ref[bi, t] = jnp.where(mine, rot, dst_ref[bi, t])

    for bi in range(nbatch):
        scatter_rows(yc_ref, bi, bi * nchunk, nctx)
        scatter_rows(y_ref, bi, bi * nchunk + nctx, nlat)


def _s5_scan(u, u_c, tables):
    m_in, z_out, lam_l = tables
    b, t, _ = u.shape
    tc = u_c.shape[1]
    ll, gi, g, p = S5_CHUNK, S5_GROUP, S5_GROUPS, S5_STATE
    nlat, nctx = t // ll, tc // ll
    rows = b * (nlat + nctx)
    width = ll * gi
    per_slab = 128 // gi
    m_hi, m_lo = _split_bf16(m_in)
    z_hi, z_lo = _split_bf16(z_out)
    step_major = lambda x, n: x.reshape(b, n, ll, g * gi).transpose(0, 2, 1, 3)
    wspec = lambda shape: pl.BlockSpec((1,) + shape, lambda s, j: (s * per_slab + j, 0, 0))
    slab = lambda n: pl.BlockSpec((b, ll, n, 128), lambda s, j: (0, 0, 0, s))
    return pl.pallas_call(
        functools.partial(_s5_kernel, nbatch=b, nlat=nlat, nctx=nctx),
        grid=(g // per_slab, per_slab),
        in_specs=[slab(nlat), slab(nctx),
                  wspec((width, width + 4 * p)), wspec((width, width + 4 * p)),
                  wspec((4 * p, width)), wspec((4 * p, width)), wspec((2, 2 * p))],
        out_specs=[slab(nlat), slab(nctx)],
        out_shape=[jax.ShapeDtypeStruct((b, ll, nlat, g * gi), F32),
                   jax.ShapeDtypeStruct((b, ll, nctx, g * gi), F32)],
        scratch_shapes=[pltpu.VMEM((rows, width), F32), pltpu.VMEM((rows, width + 4 * p), F32),
                        pltpu.VMEM((rows, 4 * p), F32)],
        compiler_params=_params("arbitrary", "arbitrary"),
        name="s5_scan",
    )(step_major(u, nlat), step_major(u_c, nctx), m_hi, m_lo, z_hi, z_lo, lam_l)


def _glu_kernel(y_ref, w_ref, b_ref, o_ref, wb_ref):
    @pl.when(pl.program_id(0) == 0)
    def _():
        wb_ref[...] = w_ref[...].astype(BF16)

    z = jax.nn.gelu(y_ref[...])
    a = jnp.dot(z.astype(BF16), wb_ref[...], preferred_element_type=F32) + b_ref[...]
    o_ref[...] = (z * jax.nn.sigmoid(a)).astype(o_ref.dtype)


def _s5_glu(y, w, bias, tm=256):
    m, n = y.shape
    tm = min(tm, m)
    return pl.pallas_call(
        _glu_kernel,
        grid=(m // tm,),
        in_specs=[pl.BlockSpec((tm, n), lambda i: (i, 0)),
                  pl.BlockSpec((n, n), lambda i: (0, 0)),
                  pl.BlockSpec((1, n), lambda i: (0, 0))],
        out_specs=pl.BlockSpec((tm, n), lambda i: (i, 0)),
        out_shape=jax.ShapeDtypeStruct((m, n), BF16),
        scratch_shapes=[pltpu.VMEM((n, n), BF16)],
        compiler_params=_params("arbitrary"),
        name="s5_glu",
    )(y, w, bias.reshape(1, n))


def _gla_state_kernel(f_ref, v_ref, lb_ref, tri_ref, s_ref, *, nchunk):
    d = pl.program_id(2)
    lb = lb_ref[0]
    st = jnp.zeros((HG_DIM, HG_DIM), F32)
    for c in range(nchunk):
        cc = jnp.where(d == 0, c, nchunk - 1 - c)
        sl = pl.ds(pl.multiple_of(cc * HG_CHUNK, HG_CHUNK), HG_CHUNK)
        f = lb + (1.0 - lb) * jax.nn.sigmoid(f_ref[0, sl, :])
        logf = jnp.log(f)
        b = jnp.dot(tri_ref[0], logf, precision=HIGHEST, preferred_element_type=F32)
        tot = jnp.sum(logf, axis=0, keepdims=True)
        kd = ((1.0 - f) * jnp.exp(tot - b)).astype(BF16)
        st = st * jnp.exp(tot) + jnp.dot(v_ref[0, sl, :].T.astype(BF16), kd, preferred_element_type=F32)
    s_ref[0, 0, 0] = st


def _gla_kernel(q_ref, f_ref, v_ref, cos_ref, sin_ref, lb_ref, tri_ref, s0_ref, *rest, nchunk, reverse):
    if reverse:
        ofwd_ref, gate_ref, ng_ref, o_ref, st_ref, osum_ref = rest
    else:
        o_ref, st_ref = rest
    heads = q_ref.shape[2] // HG_DIM

    @pl.when(pl.program_id(2) == 0)
    def _():
        for h in range(heads):
            st_ref[h] = s0_ref[0, h, 0]

    pair = 2 * HG_CHUNK
    tri = tri_ref[...]
    keep = tri > 0
    quarter = HG_DIM // 4
    lane = lax.broadcasted_iota(jnp.int32, (pair, HG_DIM), 1)
    st = [st_ref[h] for h in range(heads)]
    pairs = range(nchunk // 2)
    for p in (reversed(pairs) if reverse else pairs):
        chunks = (2 * p + 1, 2 * p) if reverse else (2 * p, 2 * p + 1)
        rows = [slice(HG_CHUNK * c, HG_CHUNK * (c + 1)) for c in chunks]
        whole = slice(pair * p, pair * (p + 1))
        gather = ((lambda ref, *ix: ref[ix[:-1] + (whole,) + ix[-1:]]) if not reverse else
                  (lambda ref, *ix: jnp.concatenate([ref[ix[:-1] + (r,) + ix[-1:]] for r in rows], axis=0)))
        cos, sin = gather(cos_ref, slice(None)), gather(sin_ref, slice(None))
        for h in range(heads):
            hl = slice(HG_DIM * h, HG_DIM * (h + 1))
            load = lambda ref: gather(ref, 0, hl)
            lb = lb_ref[h]
            f = lb + (1.0 - lb) * jax.nn.sigmoid(load(f_ref))
            logf = jnp.log(f)
            k = 1.0 - f
            v = load(v_ref)
            sg = jax.nn.sigmoid(load(q_ref))
            partner = jnp.where((lane & quarter) == 0, pltpu.roll(sg, HG_DIM - quarter, 1),
                                pltpu.roll(sg, quarter, 1))
            q = sg * cos + partner * sin
            hi, lo = _split_bf16(logf)
            b = jnp.dot(tri, hi, preferred_element_type=F32) + jnp.dot(tri, lo, preferred_element_type=F32)
            qe = (q * jnp.exp(b)).astype(BF16)
            ke = (k * jnp.exp(-b)).astype(BF16)
            att = jnp.where(keep, lax.dot_general(qe, ke, _NT, preferred_element_type=F32), 0.0)
            o_intra = jnp.dot(att.astype(BF16), v.astype(BF16), preferred_element_type=F32)
            o_inter = []
            for half in range(2):
                hs = slice(HG_CHUNK * half, HG_CHUNK * (half + 1))
                tot = jnp.sum(logf[hs], axis=0, keepdims=True)
                kd = (k[hs] * jnp.exp(tot - b[hs])).astype(BF16)
                o_inter.append(lax.dot_general(qe[hs], st[h].astype(BF16), _NT,
                                               preferred_element_type=F32))
                st[h] = st[h] * jnp.exp(tot) + jnp.dot(v[hs].T.astype(BF16), kd,
                                                       preferred_element_type=F32)
            o = o_intra + jnp.concatenate(o_inter, axis=0)
            if reverse:
                for half, r in enumerate(rows):
                    hs = slice(HG_CHUNK * half, HG_CHUNK * (half + 1))
                    osum_ref[r, hl] = o[hs] + ofwd_ref[0, r, hl]
            else:
                o_ref[0, whole, hl] = o
    for h in range(heads):
        st_ref[h] = st[h]
    if reverse:
        for h in range(heads):
            hl = slice(HG_DIM * h, HG_DIM * (h + 1))
            o = osum_ref[:, hl]
            ms = jnp.mean(o * o, axis=-1, keepdims=True)
            gate = gate_ref[0, :, hl]
            o = o * lax.rsqrt(ms + NORM_EPS) * ng_ref[h] * (gate * jax.nn.sigmoid(gate))
            o_ref[0, :, hl] = o.astype(o_ref.dtype)


def _rope_tables(t):
    pos = np.arange(t)
    quarter = HG_DIM // 4
    inv_freq = np.float32(ROPE_THETA) ** (-np.arange(quarter, dtype=np.float32) / np.float32(quarter))

    def tab(p):
        ang = p.astype(np.float32)[:, None] * inv_freq[None, :]
        c, s = np.cos(ang, dtype=np.float32), np.sin(ang, dtype=np.float32)
        return np.concatenate([c, c], axis=-1), np.concatenate([-s, s], axis=-1)

    c_r, s_r = tab(pos // GRID_W)
    c_c, s_c = tab(pos % GRID_W)
    return (jnp.asarray(np.concatenate([c_r, c_c], axis=-1)),
            jnp.asarray(np.concatenate([s_r, s_c], axis=-1)))


def _gla_masks():
    i = np.arange(HG_CHUNK)
    lower = (i[:, None] >= i[None, :]).astype(np.float32)
    return jnp.asarray(np.stack([lower, lower.T]))


def _hgrn2_gla(proj, proj_c, lb, norm_g, tblock=2048):
    b, t, _ = proj.shape
    tc = proj_c.shape[1]
    tblock = min(tblock, t)
    nt = t // tblock
    tri = _gla_masks()
    lb3 = lb.astype(F32).reshape(HG_HEADS, 1, HG_DIM)

    s0 = pl.pallas_call(
        functools.partial(_gla_state_kernel, nchunk=tc // HG_CHUNK),
        grid=(b, HG_HEADS, 2),
        in_specs=[pl.BlockSpec((1, tc, HG_DIM), lambda bi, h, d: (bi, 0, HG_HEADS * (1 + d) + h)),
                  pl.BlockSpec((1, tc, HG_DIM), lambda bi, h, d: (bi, 0, 3 * HG_HEADS + h)),
                  pl.BlockSpec((1, 1, HG_DIM), lambda bi, h, d: (h, 0, 0)),
                  pl.BlockSpec((1, HG_CHUNK, HG_CHUNK), lambda bi, h, d: (d, 0, 0))],
        out_specs=pl.BlockSpec((1, 1, 1, HG_DIM, HG_DIM), lambda bi, h, d: (bi, h, d, 0, 0)),
        out_shape=jax.ShapeDtypeStruct((b, HG_HEADS, 2, HG_DIM, HG_DIM), F32),
        compiler_params=_params("arbitrary", "arbitrary", "arbitrary"),
        name="hgrn2_ctx_state",
    )(proj_c, proj_c, lb3, tri)

    cos, sin = _rope_tables(t)
    eye2 = jnp.eye(2, dtype=F32)
    ng3 = norm_g.astype(F32).reshape(HG_HEADS, 1, HG_DIM)
    hps = HG_HEADS_PER_STEP
    width = hps * HG_DIM
    out = None
    for d in (0, 1):
        tb = (lambda ti: ti) if d == 0 else (lambda ti: nt - 1 - ti)
        tri2 = jnp.kron(eye2, tri[d]).astype(BF16)
        col = lambda off, tb=tb: pl.BlockSpec((1, tblock, width),
                                               lambda bi, hp, ti: (bi, tb(ti), off // hps + hp))
        per_head = lambda: pl.BlockSpec((hps, 1, HG_DIM), lambda bi, hp, ti: (hp, 0, 0))
        in_specs = [col(0), col(HG_HEADS * (1 + d)), col(3 * HG_HEADS),
                    pl.BlockSpec((tblock, HG_DIM), lambda bi, hp, ti, tb=tb: (tb(ti), 0)),
                    pl.BlockSpec((tblock, HG_DIM), lambda bi, hp, ti, tb=tb: (tb(ti), 0)),
                    per_head(),
                    pl.BlockSpec((2 * HG_CHUNK, 2 * HG_CHUNK), lambda bi, hp, ti: (0, 0)),
                    pl.BlockSpec((1, hps, 1, HG_DIM, HG_DIM), lambda bi, hp, ti, d=d: (bi, hp, d, 0, 0))]
        args = [proj, proj, proj, cos, sin, lb3, tri2, s0]
        if d:
            in_specs += [col(0), col(4 * HG_HEADS), per_head()]
            args += [out, proj, ng3]
        out = pl.pallas_call(
            functools.partial(_gla_kernel, nchunk=tblock // HG_CHUNK, reverse=bool(d)),
            grid=(b, HG_HEADS // hps, nt),
            in_specs=in_specs,
            out_specs=col(0),
            out_shape=jax.ShapeDtypeStruct((b, t, HG_Q), BF16 if d else F32),
            scratch_shapes=[pltpu.VMEM((hps, HG_DIM, HG_DIM), F32)]
            + ([pltpu.VMEM((tblock, width), F32)] if d else []),
            compiler_params=_params("arbitrary", "arbitrary", "arbitrary"),
            name="hgrn2_gla_bwd" if d else "hgrn2_gla_fwd",
        )(*args)
    return out


def _router_kernel(h_ref, r_ref, o_ref):
    logits = lax.dot_general(r_ref[...], h_ref[0].astype(BF16), _NT, preferred_element_type=F32)
    p = jnp.exp(logits - jnp.max(logits, axis=0, keepdims=True))
    o_ref[0] = p / jnp.sum(p, axis=0, keepdims=True)


def _router_affinity(h, router_t, tt=512):
    b, t, d = h.shape
    tt = min(tt, t)
    return pl.pallas_call(
        _router_kernel,
        grid=(b, t // tt),
        in_specs=[pl.BlockSpec((1, tt, d), lambda bi, ti: (bi, ti, 0)),
                  pl.BlockSpec((N_EXPERTS, d), lambda bi, ti: (0, 0))],
        out_specs=pl.BlockSpec((1, N_EXPERTS, tt), lambda bi, ti: (bi, 0, ti)),
        out_shape=jax.ShapeDtypeStruct((b, N_EXPERTS, t), F32),
        compiler_params=_params("arbitrary", "arbitrary"),
        name="moe_router",
    )(h, router_t)


def _select_kernel(aff_ref, idx_ref, gate_ref, pos_ref, *, cap):
    e, r, _ = aff_ref.shape[1:]
    x = aff_ref[0]
    xb = pltpu.bitcast(x, jnp.int32)

    def count(mask):
        ones = jnp.where(mask, 1.0, 0.0)
        return jnp.sum(jnp.sum(ones, axis=1, keepdims=True), axis=2, keepdims=True)

    def search(_, bounds):
        lo, hi = bounds
        mid = lo + lax.shift_right_logical(hi - lo + 1, 1)
        ok = count(xb >= mid) >= cap
        return jnp.where(ok, mid, lo), jnp.where(ok, hi, mid - 1)

    inf_bits = 0x7F800000
    tau, _ = lax.fori_loop(0, 31, search, (jnp.zeros((e, 1, 1), jnp.int32),
                                           jnp.full((e, 1, 1), inf_bits, jnp.int32)))

    ki = lax.broadcasted_iota(jnp.int32, (128, 128), 0)
    li = lax.broadcasted_iota(jnp.int32, (128, 128), 1)
    upper = jnp.where(ki <= li, 1.0, 0.0).astype(BF16)
    ones = jnp.ones((128, 128), BF16)
    ri = lax.broadcasted_iota(jnp.int32, (r, r), 0)
    ci = lax.broadcasted_iota(jnp.int32, (r, r), 1)
    earlier_rows = jnp.where(ci < ri, 1.0, 0.0).astype(BF16)

    def running_count(mask):
        m2 = jnp.where(mask, 1.0, 0.0).reshape(e * r, 128).astype(BF16)
        within = jnp.dot(m2, upper, preferred_element_type=F32)
        row_tot = jnp.dot(m2, ones, preferred_element_type=F32).astype(BF16)
        before = [jnp.dot(earlier_rows, row_tot[i * r:(i + 1) * r], preferred_element_type=F32)
                  for i in range(e)]
        return (within + jnp.concatenate(before, axis=0)).reshape(e, r, 128)

    above = xb > tau
    tied = xb == tau
    need = cap - count(above)
    chosen = jnp.logical_or(above, jnp.logical_and(tied, running_count(tied) <= need))
    pos_ref[...] = jnp.where(chosen, running_count(chosen) - 1.0, -1.0)

    slot = lax.broadcasted_iota(jnp.int32, (cap, 128), 0).astype(F32)
    lane = lax.broadcasted_iota(jnp.int32, (1, 128), 1)
    for ei in range(e):
        def row_group(g, acc, ei=ei):
            rows = pl.ds(pl.multiple_of(g * 8, 8), 8)
            pos8 = pos_ref[ei, rows, :]
            aff8 = aff_ref[0, ei, rows, :]
            for i in range(8):
                onehot = jnp.where(slot == pos8[i:i + 1], 1.0, 0.0).astype(BF16)
                tok = (g * 8 + i) * 128 + lane
                gv = aff8[i:i + 1]
                g_hi = gv.astype(BF16).astype(F32)
                g_mid = (gv - g_hi).astype(BF16).astype(F32)
                g_lo = gv - g_hi - g_mid
                feats = jnp.concatenate(
                    [lax.shift_right_logical(tok, 6).astype(F32), (tok & 63).astype(F32),
                     g_hi, g_mid, g_lo, jnp.zeros((3, 128), F32)], axis=0).astype(BF16)
                acc = acc + lax.dot_general(feats, onehot, _NT, preferred_element_type=F32)
            return acc

        acc = lax.fori_loop(0, r // 8, row_group, jnp.zeros((8, cap), F32))
        idx_ref[0, ei:ei + 1, :] = (acc[0:1] * 64.0 + acc[1:2]).astype(jnp.int32)
        gate_ref[0, ei:ei + 1, :] = acc[2:3] + acc[3:4] + acc[4:5]


def _moe_select(aff, cap):
    b, e, t = aff.shape
    tpad = max(t, 1024)
    if tpad != t:
        aff = jnp.pad(aff, ((0, 0), (0, 0), (0, tpad - t)), constant_values=-1.0)
    r = tpad // 128
    return pl.pallas_call(
        functools.partial(_select_kernel, cap=cap),
        grid=(b,),
        in_specs=[pl.BlockSpec((1, e, r, 128), lambda bi: (bi, 0, 0, 0))],
        out_specs=[pl.BlockSpec((1, e, cap), lambda bi: (bi, 0, 0)),
                   pl.BlockSpec((1, e, cap), lambda bi: (bi, 0, 0))],
        out_shape=[jax.ShapeDtypeStruct((b, e, cap), jnp.int32),
                   jax.ShapeDtypeStruct((b, e, cap), F32)],
        scratch_shapes=[pltpu.VMEM((e, r, 128), F32)],
        compiler_params=_params("arbitrary"),
        name="moe_select",
    )(aff.reshape(b, e, r, 128))


def _dispatch_kernel(idx_ref, h_hbm, o_ref, buf_ref, sem, *, tm, steps_per_batch):
    i = pl.program_id(0)
    nsteps = pl.num_programs(0)

    def row_copy(step, r, slot):
        tok = idx_ref[step * tm + r]
        return pltpu.make_async_copy(h_hbm.at[lax.div(step, steps_per_batch), pl.ds(tok, 1), :],
                                     buf_ref.at[slot, pl.ds(r, 1), :], sem.at[slot])

    def start_rows(step):
        slot = lax.rem(step, 2)

        def body(r8, carry):
            for j in range(8):
                row_copy(step, r8 * 8 + j, slot).start(priority=j % 2)
            return carry

        lax.fori_loop(0, tm // 8, body, 0)

    @pl.when(i == 0)
    def _():
        start_rows(i)

    @pl.when(i + 1 < nsteps)
    def _():
        start_rows(i + 1)

    slot = lax.rem(i, 2)

    def wait_row(r, carry):
        row_copy(i, r, slot).wait()
        return carry

    lax.fori_loop(0, tm, wait_row, 0, unroll=8)
    o_ref[...] = buf_ref[slot].astype(o_ref.dtype)


def _moe_dispatch(h, idx, tm=256):
    b, t, d = h.shape
    _, e, cap = idx.shape
    tm = min(tm, cap)
    nsteps = b * e * cap // tm
    out = pl.pallas_call(
        functools.partial(_dispatch_kernel, tm=tm, steps_per_batch=e * cap // tm),
        grid_spec=pltpu.PrefetchScalarGridSpec(
            num_scalar_prefetch=1,
            grid=(nsteps,),
            in_specs=[pl.BlockSpec(memory_space=pl.ANY)],
            out_specs=pl.BlockSpec((tm, d), lambda i, idx_ref: (i, 0)),
            scratch_shapes=[pltpu.VMEM((2, tm, d), F32), pltpu.SemaphoreType.DMA((2,))],
        ),
        out_shape=jax.ShapeDtypeStruct((b * e * cap, d), BF16),
        compiler_params=_params("arbitrary"),
        name="moe_dispatch",
    )(idx.reshape(-1), h)
    return out.reshape(b, e, cap, d)


def _first_row_tile():
    return jnp.logical_and(pl.program_id(2) == 0, pl.program_id(3) == 0)


def _expert_up_kernel(x_ref, w1_ref, w3_ref, o_ref, w1b_ref, w3b_ref):
    @pl.when(_first_row_tile())
    def _():
        w1b_ref[...] = w1_ref[...].astype(BF16)
        w3b_ref[...] = w3_ref[...].astype(BF16)

    x = x_ref[0, 0]
    a = jnp.dot(x, w1b_ref[...], preferred_element_type=F32)
    g = jnp.dot(x, w3b_ref[...], preferred_element_type=F32)
    o_ref[0, 0] = (a * jax.nn.sigmoid(a) * g).astype(o_ref.dtype)


def _expert_down_kernel(idx_ref, h_ref, w2_ref, gate_ref, acc_hbm, out_hbm, w2b_ref, rows_ref,
                        gather_sem, scatter_sem, *, tm, ot, cap):
    del acc_hbm
    ei, ni, bi, mi = (pl.program_id(a) for a in range(4))
    nn, nb, nm = (pl.num_programs(a) for a in (1, 2, 3))
    q = (ni * nb + bi) * nm + mi
    slot = lax.rem(q, 2)
    base = (bi * pl.num_programs(0) + ei) * cap + mi * tm
    col = pl.multiple_of(ni * ot, ot)

    @pl.when(_first_row_tile())
    def _():
        w2b_ref[...] = w2_ref[...].astype(BF16)

    def gather(r, s):
        return pltpu.make_async_copy(out_hbm.at[bi, pl.ds(idx_ref[base + r], 1), pl.ds(col, ot)],
                                     rows_ref.at[s, pl.ds(r, 1), :], gather_sem.at[s])

    def scatter(r, s):
        return pltpu.make_async_copy(rows_ref.at[s, pl.ds(r, 1), :],
                                     out_hbm.at[bi, pl.ds(idx_ref[base + r], 1), pl.ds(col, ot)],
                                     scatter_sem.at[s])

    def for_rows(fn):
        def body(r8, carry):
            for j in range(8):
                fn(r8 * 8 + j, j % 2)
            return carry
        lax.fori_loop(0, tm // 8, body, 0)

    @pl.when(q >= 2)
    def _():
        for_rows(lambda r, p: scatter(r, slot).wait())

    for_rows(lambda r, p: gather(r, slot).start(priority=p))
    y = jnp.dot(h_ref[0, 0], w2b_ref[...], preferred_element_type=F32) * gate_ref[0, 0]
    for_rows(lambda r, p: gather(r, slot).wait())
    rows_ref[slot] = rows_ref[slot] + y
    for_rows(lambda r, p: scatter(r, slot).start(priority=p))

    @pl.when(q == nn * nb * nm - 1)
    def _():
        for_rows(lambda r, p: scatter(r, 1 - slot).wait())
        for_rows(lambda r, p: scatter(r, slot).wait())


def _expert_ffn(xin, gate, idx, w1, w3, w2, layer, t, tm_up=512, tm=256):
    b, e, cap, d = xin.shape
    ff = w1.shape[-1]
    tm_up, tm = min(tm_up, cap), min(tm, cap)
    ft, ot = min(MOE_FF_TILE, ff), min(MOE_OUT_TILE, d)
    assert ((d // ot) * b * (cap // tm)) % 2 == 0
    hid = pl.pallas_call(
        _expert_up_kernel,
        grid=(e, ff // ft, b, cap // tm_up),
        in_specs=[pl.BlockSpec((1, 1, tm_up, d), lambda ei, fi, bi, mi: (bi, ei, mi, 0)),
                  pl.BlockSpec((None, None, d, ft), lambda ei, fi, bi, mi: (layer, ei, 0, fi)),
                  pl.BlockSpec((None, None, d, ft), lambda ei, fi, bi, mi: (layer, ei, 0, fi))],
        out_specs=pl.BlockSpec((1, 1, tm_up, ft), lambda ei, fi, bi, mi: (bi, ei, mi, fi)),
        out_shape=jax.ShapeDtypeStruct((b, e, cap, ff), BF16),
        scratch_shapes=[pltpu.VMEM((d, ft), BF16), pltpu.VMEM((d, ft), BF16)],
        compiler_params=_params("arbitrary", "arbitrary", "arbitrary", "arbitrary"),
        name="moe_expert_up",
    )(xin, w1, w3)
    return pl.pallas_call(
        functools.partial(_expert_down_kernel, tm=tm, ot=ot, cap=cap),
        grid_spec=pltpu.PrefetchScalarGridSpec(
            num_scalar_prefetch=1,
            grid=(e, d // ot, b, cap // tm),
            in_specs=[pl.BlockSpec((1, 1, tm, ff), lambda ei, ni, bi, mi, idx_ref: (bi, ei, mi, 0)),
                      pl.BlockSpec((None, None, ff, ot), lambda ei, ni, bi, mi, idx_ref: (layer, ei, 0, ni)),
                      pl.BlockSpec((1, 1, tm, 1), lambda ei, ni, bi, mi, idx_ref: (bi, ei, mi, 0)),
                      pl.BlockSpec(memory_space=pl.ANY)],
            out_specs=pl.BlockSpec(memory_space=pl.ANY),
            scratch_shapes=[pltpu.VMEM((ff, ot), BF16), pltpu.VMEM((2, tm, ot), F32),
                            pltpu.SemaphoreType.DMA((2,)), pltpu.SemaphoreType.DMA((2,))],
        ),
        out_shape=jax.ShapeDtypeStruct((b, t, d), F32),
        input_output_aliases={4: 0},
        compiler_params=_params("arbitrary", "arbitrary", "arbitrary", "arbitrary"),
        name="moe_expert_down",
    )(idx.reshape(-1), hid, w2, gate[..., None], jnp.zeros((b, t, d), F32))


def _expert_choice_moe(h, router_t, w1, w3, w2, layer):
    b, t, d = h.shape
    cap = EC_CAPACITY_FACTOR * t // N_EXPERTS
    aff = _router_affinity(h, router_t)
    idx, gate = _moe_select(aff, cap)
    idx = jnp.clip(idx, 0, t - 1)
    xin = _moe_dispatch(h, idx)
    return _expert_ffn(xin, gate, idx, w1, w3, w2, layer, t)


def _project(h, w, out_dtype, name, col0=0, n=None):
    b, t, d = h.shape
    out = _matmul(h.reshape(b * t, d), w, 0, out_dtype, name, col0=col0, n=n)
    return out.reshape(b, t, out.shape[1])


def _mixer_na_s5(h, hc, w_in, w_out, rpb, s5_tables, glu_w, glu_b):
    b, t, _ = h.shape
    tc = hc.shape[1]
    qkv = _project(h, w_in, BF16, "ab_in_qkv", 0, 3 * NA_W)
    qkv_c = _project(hc, w_in, BF16, "ab_in_qkv_ctx", 0, 3 * NA_W)
    u = _project(h, w_in, F32, "ab_in_u", 3 * NA_W, S5_WIDTH)
    u_c = _project(hc, w_in, F32, "ab_in_u_ctx", 3 * NA_W, S5_WIDTH)
    o_na = _neighbourhood_attention(qkv, qkv_c, rpb)
    oc_na = _context_attention(qkv_c)
    y_s5, yc_s5 = _s5_scan(u, u_c, s5_tables)
    token_major = lambda zz, n: zz.reshape(b, S5_CHUNK, n // S5_CHUNK, S5_WIDTH).transpose(
        0, 2, 1, 3).reshape(b * n, S5_WIDTH)
    z = token_major(_s5_glu(y_s5.reshape(b * t, S5_WIDTH), glu_w, glu_b), t)
    z_c = token_major(_s5_glu(yc_s5.reshape(b * tc, S5_WIDTH), glu_w, glu_b), tc)
    y = _matmul([o_na.reshape(b * t, NA_W), z], w_out, 0, F32, "ab_out")
    y_c = _matmul([oc_na.reshape(b * tc, NA_W), z_c], w_out, 0, F32, "ab_out_ctx")
    return y.reshape(b, t, -1), y_c.reshape(b, tc, -1)


def _mixer_hgrn2(h, hc, w_in, w_out, lb, norm_g):
    proj = _project(h, w_in, F32, "hg_in")
    proj_c = _project(hc, w_in, F32, "hg_in_ctx")
    return _project(_hgrn2_gla(proj, proj_c, lb, norm_g), w_out, F32, "hg_out")


def kernel(x, c, ctx, c_ctx, ada_w, ada_b, norm_g, ab_w_in, ab_w_out, na_rpb, s5_lam_re, s5_lam_im,
           s5_log_dt, s5_b_re, s5_b_im, s5_c_re, s5_c_im, s5_d, s5_glu_w, s5_glu_b, hg_w_in, hg_w_out,
           hg_lb_logits, hg_norm_g, moe_router, moe_w1, moe_w3, moe_w2):
    bsz, _, d = x.shape
    depth = ada_w.shape[0]
    assert depth == 2 and bsz + 1 <= 8
    assert ab_w_in.shape[0] == 1 and hg_w_in.shape[0] == 1
    xc = ctx
    cond = jax.nn.silu(c)
    cond_ctx = jax.nn.silu(c_ctx)[None]
    cond3 = jnp.concatenate([cond, cond_ctx, jnp.zeros((8 - bsz - 1, d), F32)], axis=0)
    hg_lb = jnp.cumsum(jax.nn.softmax(hg_lb_logits.astype(F32), axis=0), axis=0)
    moe_w = (moe_w1, moe_w3, moe_w2)
    ada_x, ada_c = [], []
    for layer in range(depth):
        ada = _matmul(cond3, ada_w, layer, F32, "adaln") + ada_b[layer][None]
        ada_x.append(jnp.split(ada[:bsz], 6, axis=-1))
        ada_c.append([jnp.broadcast_to(a, (bsz, d)) for a in jnp.split(ada[bsz:bsz + 1], 6, axis=-1)])
    ng0 = norm_g[0].astype(F32)
    h = _modulate(x, ng0[0], ada_x[0][0], ada_x[0][1])
    hc = _modulate(xc, ng0[0], ada_c[0][0], ada_c[0][1])
    for layer in range(depth):
        need_ctx = layer < depth - 1
        _, _, g1, sh2, sc2, g2 = ada_x[layer]
        _, _, cg1, csh2, csc2, cg2 = ada_c[layer]
        ng = norm_g[layer].astype(F32)
        if layer == 0:
            tables = _s5_tables(s5_lam_re[0], s5_lam_im[0], s5_log_dt[0], s5_b_re[0], s5_b_im[0],
                                s5_c_re[0], s5_c_im[0], s5_d[0])
            y, yc = _mixer_na_s5(h, hc, ab_w_in, ab_w_out, na_rpb[0], tables, s5_glu_w[0],
                                 s5_glu_b[0].astype(F32))
        else:
            y = _mixer_hgrn2(h, hc, hg_w_in, hg_w_out, hg_lb[0], hg_norm_g[0])
            yc = None
        router_t = moe_router[layer].T.astype(BF16)
        x, h2 = _gated_residual_modulate(x, y, ng[1], g1, ng[2], sh2, sc2, F32)
        moe = _expert_choice_moe(h2, router_t, *moe_w, layer)
        if not need_ctx:
            x = _gated_residual(x, moe, ng[3], g2)
            continue
        ng_next = norm_g[layer + 1].astype(F32)
        x, h = _gated_residual_modulate(x, moe, ng[3], g2, ng_next[0], ada_x[layer + 1][0],
                                        ada_x[layer + 1][1], BF16)
        xc, hc2 = _gated_residual_modulate(xc, yc, ng[1], cg1, ng[2], csh2, csc2, F32)
        moe_c = _expert_choice_moe(hc2, router_t, *moe_w, layer)
        xc, hc = _gated_residual_modulate(xc, moe_c, ng[3], cg2, ng_next[0], ada_c[layer + 1][0],
                                          ada_c[layer + 1][1], BF16)
    return x
```

```python
import functools
import math

import numpy as np
import jax
import jax.numpy as jnp
from jax import lax
from jax.experimental import pallas as pl
from jax.experimental.pallas import tpu as pltpu

F32 = jnp.float32
BF16 = jnp.bfloat16
HIGHEST = lax.Precision.HIGHEST

D_MODEL = 4096
GRID_W = 64
NORM_EPS = 1e-6
NA_HEADS = 16
NA_HEAD_DIM = 128
NA_WIN_R = 8
NA_WIN_C = 16
NA_W = NA_HEADS * NA_HEAD_DIM
S5_WIDTH = 2048
S5_GROUP = 16
S5_GROUPS = S5_WIDTH // S5_GROUP
S5_STATE = 64
S5_CHUNK = 16
HG_HEADS = 32
HG_DIM = 128
HG_CHUNK = 64
HG_Q = HG_HEADS * HG_DIM
HG_HEADS_PER_STEP = 2
ROPE_THETA = 10000.0
N_EXPERTS = 16
EXPERT_FF = 1024
EC_CAPACITY_FACTOR = 2

VMEM_LIMIT_BYTES = 56 * 1024 * 1024
MOE_FF_TILE = 512
MOE_OUT_TILE = 4096
NA_QROWS = 8
NA_KROWS = 16
NEG_BIG = -1e30

_NT = (((1,), (1,)), ((), ()))


def _params(*sem):
    return pltpu.CompilerParams(dimension_semantics=sem, vmem_limit_bytes=VMEM_LIMIT_BYTES)


def _mm_kernel(*refs, splits):
    *a_refs, w_ref, o_ref, wb_ref = refs

    @pl.when(pl.program_id(1) == 0)
    def _():
        wb_ref[...] = w_ref[...].astype(BF16)

    acc = None
    for a_ref, (k0, k1) in zip(a_refs, splits):
        part = jnp.dot(a_ref[...].astype(BF16), wb_ref[k0:k1, :], preferred_element_type=F32)
        acc = part if acc is None else acc + part
    o_ref[...] = acc.astype(o_ref.dtype)


def _matmul(a, w, layer, out_dtype, name, col0=0, n=None, tm=1024, tn=512):
    pieces = list(a) if isinstance(a, (list, tuple)) else [a]
    m = pieces[0].shape[0]
    k = w.shape[1]
    bounds = np.cumsum([0] + [piece.shape[1] for piece in pieces])
    assert bounds[-1] == k and all(piece.shape[0] == m for piece in pieces)
    n = w.shape[2] - col0 if n is None else n
    tm = min(tm, m)
    tn = min(tn, n)
    assert m % tm == 0 and n % tn == 0 and col0 % tn == 0, (m, n, col0, tm, tn)
    cb = col0 // tn
    return pl.pallas_call(
        functools.partial(_mm_kernel, splits=tuple(zip(bounds[:-1].tolist(), bounds[1:].tolist()))),
        grid=(n // tn, m // tm),
        in_specs=[pl.BlockSpec((tm, piece.shape[1]), lambda j, i: (i, 0)) for piece in pieces]
        + [pl.BlockSpec((None, k, tn), lambda j, i: (layer, 0, cb + j))],
        out_specs=pl.BlockSpec((tm, tn), lambda j, i: (i, j)),
        out_shape=jax.ShapeDtypeStruct((m, n), out_dtype),
        scratch_shapes=[pltpu.VMEM((k, tn), BF16)],
        compiler_params=_params("arbitrary", "arbitrary"),
        name=name,
    )(*pieces, w)


def _modulate_kernel(x_ref, g_ref, sh_ref, sc_ref, o_ref):
    x = x_ref[0]
    ms = jnp.mean(x * x, axis=-1, keepdims=True)
    y = x * lax.rsqrt(ms + NORM_EPS) * g_ref[...]
    o_ref[0] = (y * (1.0 + sc_ref[0]) + sh_ref[0]).astype(o_ref.dtype)


def _modulate(x, g, shift, scale, out_dtype=BF16, tt=256):
    b, t, d = x.shape
    tt = min(tt, t)
    row = lambda bi, ti: (bi, 0, 0)
    return pl.pallas_call(
        _modulate_kernel,
        grid=(b, t // tt),
        in_specs=[pl.BlockSpec((1, tt, d), lambda bi, ti: (bi, ti, 0)),
                  pl.BlockSpec((1, d), lambda bi, ti: (0, 0)),
                  pl.BlockSpec((1, 1, d), row),
                  pl.BlockSpec((1, 1, d), row)],
        out_specs=pl.BlockSpec((1, tt, d), lambda bi, ti: (bi, ti, 0)),
        out_shape=jax.ShapeDtypeStruct((b, t, d), out_dtype),
        compiler_params=_params("arbitrary", "arbitrary"),
        name="modulate",
    )(x, g.reshape(1, d), shift.reshape(b, 1, d), scale.reshape(b, 1, d))


def _residual_kernel(x_ref, y_ref, g_ref, gate_ref, o_ref):
    y = y_ref[0].astype(F32)
    ms = jnp.mean(y * y, axis=-1, keepdims=True)
    o_ref[0] = x_ref[0] + gate_ref[0] * (y * lax.rsqrt(ms + NORM_EPS) * g_ref[...])


def _gated_residual(x, y, g, gate, tt=256):
    b, t, d = x.shape
    tt = min(tt, t)
    blk = pl.BlockSpec((1, tt, d), lambda bi, ti: (bi, ti, 0))
    return pl.pallas_call(
        _residual_kernel,
        grid=(b, t // tt),
        in_specs=[blk, blk,
                  pl.BlockSpec((1, d), lambda bi, ti: (0, 0)),
                  pl.BlockSpec((1, 1, d), lambda bi, ti: (bi, 0, 0))],
        out_specs=blk,
        out_shape=jax.ShapeDtypeStruct((b, t, d), F32),
        compiler_params=_params("arbitrary", "arbitrary"),
        name="gated_residual",
    )(x, y, g.reshape(1, d), gate.reshape(b, 1, d))


def _residual_modulate_kernel(x_ref, y_ref, g_ref, gate_ref, g2_ref, sh_ref, sc_ref, xo_ref, ho_ref):
    y = y_ref[0].astype(F32)
    ms = jnp.mean(y * y, axis=-1, keepdims=True)
    x = x_ref[0] + gate_ref[0] * (y * lax.rsqrt(ms + NORM_EPS) * g_ref[...])
    xo_ref[0] = x
    ms2 = jnp.mean(x * x, axis=-1, keepdims=True)
    h = x * lax.rsqrt(ms2 + NORM_EPS) * g2_ref[...]
    ho_ref[0] = (h * (1.0 + sc_ref[0]) + sh_ref[0]).astype(ho_ref.dtype)


def _gated_residual_modulate(x, y, g, gate, g2, shift, scale, out_dtype, tt=256):
    b, t, d = x.shape
    tt = min(tt, t)
    blk = pl.BlockSpec((1, tt, d), lambda bi, ti: (bi, ti, 0))
    vec = pl.BlockSpec((1, d), lambda bi, ti: (0, 0))
    row = pl.BlockSpec((1, 1, d), lambda bi, ti: (bi, 0, 0))
    per_row = lambda a: a.reshape(b, 1, d)
    return pl.pallas_call(
        _residual_modulate_kernel,
        grid=(b, t // tt),
        in_specs=[blk, blk, vec, row, vec, row, row],
        out_specs=[blk, blk],
        out_shape=[jax.ShapeDtypeStruct((b, t, d), F32), jax.ShapeDtypeStruct((b, t, d), out_dtype)],
        compiler_params=_params("arbitrary", "arbitrary"),
        name="gated_residual_modulate",
    )(x, y, g.reshape(1, d), per_row(gate), g2.reshape(1, d), per_row(shift), per_row(scale))


def _na_bias_blocks(rpb):
    qc = np.arange(GRID_W)
    cs = np.clip(qc - NA_WIN_C // 2, 0, GRID_W - NA_WIN_C)
    kc = np.arange(GRID_W)
    in_c = (kc[None, :] >= cs[:, None]) & (kc[None, :] < cs[:, None] + NA_WIN_C)
    rel_c = kc[None, :] - qc[:, None] + NA_WIN_C - 1
    sel_c = (rel_c[:, :, None] == np.arange(2 * NA_WIN_C - 1)).astype(np.float32)
    z = jnp.einsum('hrc,xyc->hrxy', rpb.astype(F32), sel_c, precision=HIGHEST)
    z = jnp.where(in_c[None, None], z, NEG_BIG)
    neg = jnp.full_like(z, NEG_BIG)
    z_next = jnp.concatenate([z[:, 1:], neg[:, :1]], axis=1)
    cat = lambda left, right: jnp.concatenate([left, right], axis=-1)
    return jnp.stack([cat(z, z_next), cat(z, neg), cat(neg, z)], axis=1)


def _na_fill_bias(bias_ref, tiles_ref, j, rows):
    qr = NA_QROWS * j + np.arange(NA_QROWS)
    ws = int(np.clip(NA_QROWS * j - NA_WIN_R // 2, 0, rows - NA_KROWS))
    r0 = np.clip(qr - NA_WIN_R // 2, 0, rows - NA_WIN_R)
    visible = lambda a, kr: r0[a] <= kr < r0[a] + NA_WIN_R
    for a in range(NA_QROWS):
        for m in range(NA_KROWS // 2):
            ka, kb = ws + 2 * m, ws + 2 * m + 1
            rel = lambda kr: int(kr - qr[a] + NA_WIN_R - 1)
            if visible(a, ka) and visible(a, kb):
                tile = tiles_ref[0, 0, rel(ka)]
            elif visible(a, ka):
                tile = tiles_ref[0, 1, rel(ka)]
            elif visible(a, kb):
                tile = tiles_ref[0, 2, rel(kb)]
            else:
                tile = jnp.full((GRID_W, 2 * GRID_W), NEG_BIG, F32)
            bias_ref[GRID_W * a:GRID_W * (a + 1), 2 * GRID_W * m:2 * GRID_W * (m + 1)] = tile


def _na_kernel(q_ref, k_ref, v_ref, kc_ref, vc_ref, tiles_ref, o_ref, bias_ref, *, rows):
    j = pl.program_id(2)
    nblk = rows // NA_QROWS
    for j_static in (0, 1, nblk - 1):
        @pl.when(j == j_static)
        def _(j_static=j_static):
            _na_fill_bias(bias_ref, tiles_ref, j_static, rows)

    ws = jnp.clip(NA_QROWS * j - NA_WIN_R // 2, 0, rows - NA_KROWS) * GRID_W
    ws = pl.multiple_of(ws, 256)
    nk = NA_KROWS * GRID_W
    scale = NA_HEAD_DIM ** -0.5
    q = q_ref[0]
    kw = k_ref[0, pl.ds(ws, nk), :]
    vw = v_ref[0, pl.ds(ws, nk), :]
    s_nb = lax.dot_general(q, kw, _NT, preferred_element_type=F32) * scale + bias_ref[...]
    s_c = lax.dot_general(q, kc_ref[0], _NT, preferred_element_type=F32) * scale
    m = jnp.maximum(jnp.max(s_nb, axis=-1, keepdims=True), jnp.max(s_c, axis=-1, keepdims=True))
    p_nb = jnp.exp(s_nb - m)
    p_c = jnp.exp(s_c - m)
    denom = jnp.sum(p_nb, axis=-1, keepdims=True) + jnp.sum(p_c, axis=-1, keepdims=True)
    o = (jnp.dot(p_nb.astype(BF16), vw, preferred_element_type=F32)
         + jnp.dot(p_c.astype(BF16), vc_ref[0], preferred_element_type=F32))
    o_ref[0] = (o / denom).astype(o_ref.dtype)


def _neighbourhood_attention(qkv, qkv_c, rpb):
    b, t, _ = qkv.shape
    tc = qkv_c.shape[1]
    rows = t // GRID_W
    nblk = rows // NA_QROWS
    assert nblk >= 3
    tq = NA_QROWS * GRID_W
    tiles = _na_bias_blocks(rpb)
    return pl.pallas_call(
        functools.partial(_na_kernel, rows=rows),
        grid=(b, NA_HEADS, nblk),
        in_specs=[
            pl.BlockSpec((1, tq, NA_HEAD_DIM), lambda bi, h, j: (bi, j, h)),
            pl.BlockSpec((1, t, NA_HEAD_DIM), lambda bi, h, j: (bi, 0, NA_HEADS + h)),
            pl.BlockSpec((1, t, NA_HEAD_DIM), lambda bi, h, j: (bi, 0, 2 * NA_HEADS + h)),
            pl.BlockSpec((1, tc, NA_HEAD_DIM), lambda bi, h, j: (bi, 0, NA_HEADS + h)),
            pl.BlockSpec((1, tc, NA_HEAD_DIM), lambda bi, h, j: (bi, 0, 2 * NA_HEADS + h)),
            pl.BlockSpec((1,) + tiles.shape[1:], lambda bi, h, j: (h, 0, 0, 0, 0)),
        ],
        out_specs=pl.BlockSpec((1, tq, NA_HEAD_DIM), lambda bi, h, j: (bi, j, h)),
        out_shape=jax.ShapeDtypeStruct((b, t, NA_W), BF16),
        scratch_shapes=[pltpu.VMEM((tq, NA_KROWS * GRID_W), F32)],
        compiler_params=_params("arbitrary", "arbitrary", "arbitrary"),
        name="neighbourhood_attention",
    )(qkv, qkv, qkv, qkv_c, qkv_c, tiles)


def _ctx_attn_kernel(q_ref, k_ref, v_ref, o_ref):
    scale = NA_HEAD_DIM ** -0.5
    s = lax.dot_general(q_ref[0], k_ref[0], _NT, preferred_element_type=F32) * scale
    p = jnp.exp(s - jnp.max(s, axis=-1, keepdims=True))
    denom = jnp.sum(p, axis=-1, keepdims=True)
    o = jnp.dot(p.astype(BF16), v_ref[0], preferred_element_type=F32)
    o_ref[0] = (o / denom).astype(o_ref.dtype)


def _context_attention(qkv_c):
    b, tc, _ = qkv_c.shape
    blk = lambda off: pl.BlockSpec((1, tc, NA_HEAD_DIM), lambda bi, h: (bi, 0, off + h))
    return pl.pallas_call(
        _ctx_attn_kernel,
        grid=(b, NA_HEADS),
        in_specs=[blk(0), blk(NA_HEADS), blk(2 * NA_HEADS)],
        out_specs=blk(0),
        out_shape=jax.ShapeDtypeStruct((b, tc, NA_W), BF16),
        compiler_params=_params("arbitrary", "arbitrary"),
        name="context_attention",
    )(qkv_c, qkv_c, qkv_c)


def _s5_tables(lam_re, lam_im, log_dt, b_re, b_im, c_re, c_im, d_skip):
    ll, gi, p, g = S5_CHUNK, S5_GROUP, S5_STATE, S5_GROUPS
    f32 = F32
    lam = lax.complex(lam_re.astype(f32), lam_im.astype(f32))
    dt = jnp.exp(log_dt.astype(f32))[..., None]
    lam_dt = lam * dt
    lam_bar = jnp.exp(lam_dt)
    b_bar = ((lam_bar - 1.0) / lam)[..., None] * lax.complex(b_re.astype(f32), b_im.astype(f32))[None]
    c_mat = lax.complex(c_re.astype(f32), c_im.astype(f32))
    taus = jnp.arange(ll + 1, dtype=f32)
    pw = jnp.exp(lam_dt[None] * taus[:, None, None, None])

    cp = c_mat[:, None] * jnp.moveaxis(pw[:ll], 0, 1)[:, :, :, None, :]
    cp_ri = jnp.concatenate([jnp.real(cp), jnp.imag(cp)], axis=-1)
    b_ri = jnp.concatenate([jnp.real(b_bar), -jnp.imag(b_bar)], axis=-2)
    k_re = jnp.einsum('dtgiq,dgqj->dtgij', cp_ri, b_ri, precision=HIGHEST)
    kt = jnp.swapaxes(k_re, -1, -2)
    skip = d_skip.astype(f32)[:, :, None] * jnp.eye(gi, dtype=f32)
    signed = jnp.concatenate([kt[1][:0:-1], (kt[0][0] + kt[1][0] + skip)[None], kt[0][1:]], axis=0)
    idx = np.arange(ll)
    lag = idx[None, :] - idx[:, None] + ll - 1
    sel = (lag[:, :, None] == np.arange(2 * ll - 1)).astype(np.float32)
    t_sum = jnp.einsum('stu,ugji->gsjti', sel, signed, precision=HIGHEST).reshape(g, ll * gi, ll * gi)

    wf = pw[:ll, 0][::-1][:, :, :, None] * b_bar[0][None]
    wb = pw[:ll, 1][:, :, :, None] * b_bar[1][None]
    to_rows = lambda w: jnp.transpose(w, (1, 0, 3, 2)).reshape(g, ll * gi, p)
    w_state = jnp.concatenate([to_rows(f(w)) for f in (jnp.real, jnp.imag) for w in (wf, wb)], axis=-1)

    zf = c_mat[0][None] * pw[1:, 0][:, :, None, :]
    zb = c_mat[1][None] * pw[1:, 1][::-1][:, :, None, :]
    to_cols = lambda z: jnp.swapaxes(jnp.transpose(z, (1, 0, 2, 3)).reshape(g, ll * gi, p), 1, 2)
    z_blocks = [to_cols(f(z)) for f in (jnp.real, lambda v: -jnp.imag(v)) for z in (zf, zb)]
    z_out = jnp.concatenate(z_blocks, axis=1)

    lam_l = jnp.stack([jnp.concatenate([f(pw[ll, 0]), f(pw[ll, 1])], axis=-1)
                       for f in (jnp.real, jnp.imag)], axis=1)
    return t_sum, w_state, z_out, lam_l


def _split_bf16(x):
    hi = x.astype(BF16)
    return hi, (x - hi.astype(F32)).astype(BF16)


def _dot3(a, b):
    a_hi, a_lo = a if isinstance(a, tuple) else _split_bf16(a)
    b_hi, b_lo = _split_bf16(b)
    d = lambda p, q: jnp.dot(p, q, preferred_element_type=F32)
    return d(a_hi, b_hi) + d(a_lo, b_hi) + d(a_hi, b_lo)


def _s5_kernel(u_ref, uc_ref, t_ref, w_ref, z_ref, lam_ref, y_ref, yc_ref,
               ug_ref, r_ref, s_ref, *, nbatch, nlat, nctx):
    ll, gi, p = S5_CHUNK, S5_GROUP, S5_STATE
    width = ll * gi
    per_vreg = 128 // gi
    nchunk = nctx + nlat
    g = pl.program_id(1)

    def group_rows(src_ref, bi, n):
        lane = lax.broadcasted_iota(jnp.int32, (n, 128), 1)
        cols = []
        for half in range(ll // per_vreg):
            acc = None
            for tt in range(per_vreg):
                ut = src_ref[bi, half * per_vreg + tt]
                rot = pltpu.roll(ut, jnp.mod(gi * (tt - g), 128), 1)
                keep = jnp.logical_and(lane >= gi * tt, lane < gi * (tt + 1))
                acc = rot if acc is None else jnp.where(keep, rot, acc)
            cols.append(acc)
        return jnp.concatenate(cols, axis=1)

    for bi in range(nbatch):
        ug_ref[bi * nchunk:bi * nchunk + nctx, :] = group_rows(uc_ref, bi, nctx)
        ug_ref[bi * nchunk + nctx:(bi + 1) * nchunk, :] = group_rows(u_ref, bi, nlat)

    ug = _split_bf16(ug_ref[...])
    r_ref[:, 0:width] = _dot3(ug, t_ref[0])
    r_ref[:, width:] = _dot3(ug, w_ref[0])
    lam = lam_ref[0]
    lam_r, lam_i = lam[0:1], lam[1:2]
    fwd_lane = lax.broadcasted_iota(jnp.int32, (1, 2 * p), 1) < p
    v_re, v_im = slice(width, width + 2 * p), slice(width + 2 * p, width + 4 * p)

    sub = 8
    ngrp, nctx_g = nchunk // sub, nctx // sub

    def body(kg, carry):
        gb = jnp.where(kg < nctx_g, nctx_g - 1 - kg, ngrp - 1 - (kg - nctx_g))
        out = []
        for bi in range(nbatch):
            sr, si = carry[bi]
            rf = pl.ds(pl.multiple_of(bi * nchunk + kg * sub, sub), sub)
            rb = pl.ds(pl.multiple_of(bi * nchunk + gb * sub, sub), sub)
            vfr, vfi = r_ref[rf, v_re], r_ref[rf, v_im]
            vbr, vbi = r_ref[rb, v_re], r_ref[rb, v_im]
            before_r, before_i = [], []
            for i in range(sub):
                before_r.append(sr)
                before_i.append(si)
                j = sub - 1 - i
                vr = jnp.where(fwd_lane, vfr[i:i + 1], vbr[j:j + 1])
                vi = jnp.where(fwd_lane, vfi[i:i + 1], vbi[j:j + 1])
                sr, si = lam_r * sr - lam_i * si + vr, lam_r * si + lam_i * sr + vi
            s_ref[rf, 0:p] = jnp.concatenate(before_r, axis=0)[:, 0:p]
            s_ref[rb, p:2 * p] = jnp.concatenate(before_r[::-1], axis=0)[:, p:2 * p]
            s_ref[rf, 2 * p:3 * p] = jnp.concatenate(before_i, axis=0)[:, 0:p]
            s_ref[rb, 3 * p:4 * p] = jnp.concatenate(before_i[::-1], axis=0)[:, p:2 * p]
            out.append((sr, si))
        return tuple(out)

    zero = jnp.zeros((1, 2 * p), F32)
    lax.fori_loop(0, ngrp, body, tuple((zero, zero) for _ in range(nbatch)))
    yg = r_ref[:, 0:width] + _dot3(s_ref[...], z_ref[0])

    @pl.when(g == 0)
    def _():
        y_ref[...] = jnp.zeros(y_ref.shape, F32)
        yc_ref[...] = jnp.zeros(yc_ref.shape, F32)

    def scatter_rows(dst_ref, bi, row0, n):
        lane = lax.broadcasted_iota(jnp.int32, (n, 128), 1)
        mine = jnp.logical_and(lane >= gi * g, lane < gi * (g + 1))
        for t in range(ll):
            half, tt = divmod(t, per_vreg)
            piece = yg[row0:row0 + n, 128 * half:128 * (half + 1)]
            rot = pltpu.roll(piece, jnp.mod(gi * (g - tt), 128), 1)
            dst_ref[bi, t] = jnp.where(mine, rot, dst_ref[bi, t])

    for bi in range(nbatch):
        scatter_rows(yc_ref, bi, bi * nchunk, nctx)
        scatter_rows(y_ref, bi, bi * nchunk + nctx, nlat)


def _s5_scan(u, u_c, tables):
    t_sum, w_state, z_out, lam_l = tables
    b, t, _ = u.shape
    tc = u_c.shape[1]
    ll, gi, g, p = S5_CHUNK, S5_GROUP, S5_GROUPS, S5_STATE
    nlat, nctx = t // ll, tc // ll
    rows = b * (nlat + nctx)
    width = ll * gi
    per_slab = 128 // gi
    step_major = lambda x, n: x.reshape(b, n, ll, g * gi).transpose(0, 2, 1, 3)
    wspec = lambda shape: pl.BlockSpec((1,) + shape, lambda s, j: (s * per_slab + j, 0, 0))
    slab = lambda n: pl.BlockSpec((b, ll, n, 128), lambda s, j: (0, 0, 0, s))
    return pl.pallas_call(
        functools.partial(_s5_kernel, nbatch=b, nlat=nlat, nctx=nctx),
        grid=(g // per_slab, per_slab),
        in_specs=[slab(nlat), slab(nctx),
                  wspec((width, width)), wspec((width, 4 * p)), wspec((4 * p, width)),
                  wspec((2, 2 * p))],
        out_specs=[slab(nlat), slab(nctx)],
        out_shape=[jax.ShapeDtypeStruct((b, ll, nlat, g * gi), F32),
                   jax.ShapeDtypeStruct((b, ll, nctx, g * gi), F32)],
        scratch_shapes=[pltpu.VMEM((rows, width), F32), pltpu.VMEM((rows, width + 4 * p), F32),
                        pltpu.VMEM((rows, 4 * p), F32)],
        compiler_params=_params("arbitrary", "arbitrary"),
        name="s5_scan",
    )(step_major(u, nlat), step_major(u_c, nctx), t_sum, w_state, z_out, lam_l)


def _glu_kernel(y_ref, w_ref, b_ref, o_ref, wb_ref):
    @pl.when(pl.program_id(0) == 0)
    def _():
        wb_ref[...] = w_ref[...].astype(BF16)

    z = jax.nn.gelu(y_ref[...])
    a = jnp.dot(z.astype(BF16), wb_ref[...], preferred_element_type=F32) + b_ref[...]
    o_ref[...] = (z * jax.nn.sigmoid(a)).astype(o_ref.dtype)


def _s5_glu(y, w, bias, tm=256):
    m, n = y.shape
    tm = min(tm, m)
    return pl.pallas_call(
        _glu_kernel,
        grid=(m // tm,),
        in_specs=[pl.BlockSpec((tm, n), lambda i: (i, 0)),
                  pl.BlockSpec((n, n), lambda i: (0, 0)),
                  pl.BlockSpec((1, n), lambda i: (0, 0))],
        out_specs=pl.BlockSpec((tm, n), lambda i: (i, 0)),
        out_shape=jax.ShapeDtypeStruct((m, n), BF16),
        scratch_shapes=[pltpu.VMEM((n, n), BF16)],
        compiler_params=_params("arbitrary"),
        name="s5_glu",
    )(y, w, bias.reshape(1, n))


def _gla_state_kernel(f_ref, v_ref, lb_ref, tri_ref, s_ref, *, nchunk):
    d = pl.program_id(2)
    lb = lb_ref[0]
    st = jnp.zeros((HG_DIM, HG_DIM), F32)
    for c in range(nchunk):
        cc = jnp.where(d == 0, c, nchunk - 1 - c)
        sl = pl.ds(pl.multiple_of(cc * HG_CHUNK, HG_CHUNK), HG_CHUNK)
        f = lb + (1.0 - lb) * jax.nn.sigmoid(f_ref[0, sl, :])
        logf = jnp.log(f)
        b = jnp.dot(tri_ref[0], logf, precision=HIGHEST, preferred_element_type=F32)
        tot = jnp.sum(logf, axis=0, keepdims=True)
        kd = ((1.0 - f) * jnp.exp(tot - b)).astype(BF16)
        st = st * jnp.exp(tot) + jnp.dot(v_ref[0, sl, :].T.astype(BF16), kd, preferred_element_type=F32)
    s_ref[0, 0, 0] = st


def _gla_kernel(q_ref, f_ref, v_ref, cos_ref, sin_ref, lb_ref, tri_ref, s0_ref, *rest, nchunk, reverse):
    if reverse:
        ofwd_ref, gate_ref, ng_ref, o_ref, st_ref, osum_ref = rest
    else:
        o_ref, st_ref = rest
    heads = q_ref.shape[2] // HG_DIM

    @pl.when(pl.program_id(2) == 0)
    def _():
        for h in range(heads):
            st_ref[h] = s0_ref[0, h, 0]

    pair = 2 * HG_CHUNK
    tri = tri_ref[...]
    keep = tri > 0
    quarter = HG_DIM // 4
    lane = lax.broadcasted_iota(jnp.int32, (pair, HG_DIM), 1)
    st = [st_ref[h] for h in range(heads)]
    pairs = range(nchunk // 2)
    for p in (reversed(pairs) if reverse else pairs):
        chunks = (2 * p + 1, 2 * p) if reverse else (2 * p, 2 * p + 1)
        rows = [slice(HG_CHUNK * c, HG_CHUNK * (c + 1)) for c in chunks]
        whole = slice(pair * p, pair * (p + 1))
        gather = ((lambda ref, *ix: ref[ix[:-1] + (whole,) + ix[-1:]]) if not reverse else
                  (lambda ref, *ix: jnp.concatenate([ref[ix[:-1] + (r,) + ix[-1:]] for r in rows], axis=0)))
        cos, sin = gather(cos_ref, slice(None)), gather(sin_ref, slice(None))
        for h in range(heads):
            hl = slice(HG_DIM * h, HG_DIM * (h + 1))
            load = lambda ref: gather(ref, 0, hl)
            lb = lb_ref[h]
            f = lb + (1.0 - lb) * jax.nn.sigmoid(load(f_ref))
            logf = jnp.log(f)
            k = 1.0 - f
            v = load(v_ref)
            sg = jax.nn.sigmoid(load(q_ref))
            partner = jnp.where((lane & quarter) == 0, pltpu.roll(sg, HG_DIM - quarter, 1),
                                pltpu.roll(sg, quarter, 1))
            q = sg * cos + partner * sin
            hi, lo = _split_bf16(logf)
            b = jnp.dot(tri, hi, preferred_element_type=F32) + jnp.dot(tri, lo, preferred_element_type=F32)
            qe = (q * jnp.exp(b)).astype(BF16)
            ke = (k * jnp.exp(-b)).astype(BF16)
            att = jnp.where(keep, lax.dot_general(qe, ke, _NT, preferred_element_type=F32), 0.0)
            o_intra = jnp.dot(att.astype(BF16), v.astype(BF16), preferred_element_type=F32)
            o_inter = []
            for half in range(2):
                hs = slice(HG_CHUNK * half, HG_CHUNK * (half + 1))
                tot = jnp.sum(logf[hs], axis=0, keepdims=True)
                kd = (k[hs] * jnp.exp(tot - b[hs])).astype(BF16)
                o_inter.append(lax.dot_general(qe[hs], st[h].astype(BF16), _NT,
                                               preferred_element_type=F32))
                st[h] = st[h] * jnp.exp(tot) + jnp.dot(v[hs].T.astype(BF16), kd,
                                                       preferred_element_type=F32)
            o = o_intra + jnp.concatenate(o_inter, axis=0)
            if reverse:
                for half, r in enumerate(rows):
                    hs = slice(HG_CHUNK * half, HG_CHUNK * (half + 1))
                    osum_ref[r, hl] = o[hs] + ofwd_ref[0, r, hl]
            else:
                o_ref[0, whole, hl] = o
    for h in range(heads):
        st_ref[h] = st[h]
    if reverse:
        for h in range(heads):
            hl = slice(HG_DIM * h, HG_DIM * (h + 1))
            o = osum_ref[:, hl]
            ms = jnp.mean(o * o, axis=-1, keepdims=True)
            gate = gate_ref[0, :, hl]
            o = o * lax.rsqrt(ms + NORM_EPS) * ng_ref[h] * (gate * jax.nn.sigmoid(gate))
            o_ref[0, :, hl] = o.astype(o_ref.dtype)


def _rope_tables(t):
    pos = np.arange(t)
    quarter = HG_DIM // 4
    inv_freq = np.float32(ROPE_THETA) ** (-np.arange(quarter, dtype=np.float32) / np.float32(quarter))

    def tab(p):
        ang = p.astype(np.float32)[:, None] * inv_freq[None, :]
        c, s = np.cos(ang, dtype=np.float32), np.sin(ang, dtype=np.float32)
        return np.concatenate([c, c], axis=-1), np.concatenate([-s, s], axis=-1)

    c_r, s_r = tab(pos // GRID_W)
    c_c, s_c = tab(pos % GRID_W)
    return (jnp.asarray(np.concatenate([c_r, c_c], axis=-1)),
            jnp.asarray(np.concatenate([s_r, s_c], axis=-1)))


def _gla_masks():
    i = np.arange(HG_CHUNK)
    lower = (i[:, None] >= i[None, :]).astype(np.float32)
    return jnp.asarray(np.stack([lower, lower.T]))


def _hgrn2_gla(proj, proj_c, lb, norm_g, tblock=2048):
    b, t, _ = proj.shape
    tc = proj_c.shape[1]
    tblock = min(tblock, t)
    nt = t // tblock
    tri = _gla_masks()
    lb3 = lb.astype(F32).reshape(HG_HEADS, 1, HG_DIM)

    s0 = pl.pallas_call(
        functools.partial(_gla_state_kernel, nchunk=tc // HG_CHUNK),
        grid=(b, HG_HEADS, 2),
        in_specs=[pl.BlockSpec((1, tc, HG_DIM), lambda bi, h, d: (bi, 0, HG_HEADS * (1 + d) + h)),
                  pl.BlockSpec((1, tc, HG_DIM), lambda bi, h, d: (bi, 0, 3 * HG_HEADS + h)),
                  pl.BlockSpec((1, 1, HG_DIM), lambda bi, h, d: (h, 0, 0)),
                  pl.BlockSpec((1, HG_CHUNK, HG_CHUNK), lambda bi, h, d: (d, 0, 0))],
        out_specs=pl.BlockSpec((1, 1, 1, HG_DIM, HG_DIM), lambda bi, h, d: (bi, h, d, 0, 0)),
        out_shape=jax.ShapeDtypeStruct((b, HG_HEADS, 2, HG_DIM, HG_DIM), F32),
        compiler_params=_params("arbitrary", "arbitrary", "arbitrary"),
        name="hgrn2_ctx_state",
    )(proj_c, proj_c, lb3, tri)

    cos, sin = _rope_tables(t)
    eye2 = jnp.eye(2, dtype=F32)
    ng3 = norm_g.astype(F32).reshape(HG_HEADS, 1, HG_DIM)
    hps = HG_HEADS_PER_STEP
    width = hps * HG_DIM
    out = None
    for d in (0, 1):
        tb = (lambda ti: ti) if d == 0 else (lambda ti: nt - 1 - ti)
        tri2 = jnp.kron(eye2, tri[d]).astype(BF16)
        col = lambda off, tb=tb: pl.BlockSpec((1, tblock, width),
                                               lambda bi, hp, ti: (bi, tb(ti), off // hps + hp))
        per_head = lambda: pl.BlockSpec((hps, 1, HG_DIM), lambda bi, hp, ti: (hp, 0, 0))
        in_specs = [col(0), col(HG_HEADS * (1 + d)), col(3 * HG_HEADS),
                    pl.BlockSpec((tblock, HG_DIM), lambda bi, hp, ti, tb=tb: (tb(ti), 0)),
                    pl.BlockSpec((tblock, HG_DIM), lambda bi, hp, ti, tb=tb: (tb(ti), 0)),
                    per_head(),
                    pl.BlockSpec((2 * HG_CHUNK, 2 * HG_CHUNK), lambda bi, hp, ti: (0, 0)),
                    pl.BlockSpec((1, hps, 1, HG_DIM, HG_DIM), lambda bi, hp, ti, d=d: (bi, hp, d, 0, 0))]
        args = [proj, proj, proj, cos, sin, lb3, tri2, s0]
        if d:
            in_specs += [col(0), col(4 * HG_HEADS), per_head()]
            args += [out, proj, ng3]
        out = pl.pallas_call(
            functools.partial(_gla_kernel, nchunk=tblock // HG_CHUNK, reverse=bool(d)),
            grid=(b, HG_HEADS // hps, nt),
            in_specs=in_specs,
            out_specs=col(0),
            out_shape=jax.ShapeDtypeStruct((b, t, HG_Q), BF16 if d else F32),
            scratch_shapes=[pltpu.VMEM((hps, HG_DIM, HG_DIM), F32)]
            + ([pltpu.VMEM((tblock, width), F32)] if d else []),
            compiler_params=_params("arbitrary", "arbitrary", "arbitrary"),
            name="hgrn2_gla_bwd" if d else "hgrn2_gla_fwd",
        )(*args)
    return out


def _router_kernel(h_ref, r_ref, o_ref):
    logits = lax.dot_general(r_ref[...], h_ref[0].astype(BF16), _NT, preferred_element_type=F32)
    p = jnp.exp(logits - jnp.max(logits, axis=0, keepdims=True))
    o_ref[0] = p / jnp.sum(p, axis=0, keepdims=True)


def _router_affinity(h, router_t, tt=512):
    b, t, d = h.shape
    tt = min(tt, t)
    return pl.pallas_call(
        _router_kernel,
        grid=(b, t // tt),
        in_specs=[pl.BlockSpec((1, tt, d), lambda bi, ti: (bi, ti, 0)),
                  pl.BlockSpec((N_EXPERTS, d), lambda bi, ti: (0, 0))],
        out_specs=pl.BlockSpec((1, N_EXPERTS, tt), lambda bi, ti: (bi, 0, ti)),
        out_shape=jax.ShapeDtypeStruct((b, N_EXPERTS, t), F32),
        compiler_params=_params("arbitrary", "arbitrary"),
        name="moe_router",
    )(h, router_t)


def _select_kernel(aff_ref, idx_ref, gate_ref, pos_ref, *, cap):
    e, r, _ = aff_ref.shape[1:]
    x = aff_ref[0]
    xb = pltpu.bitcast(x, jnp.int32)

    def count(mask):
        ones = jnp.where(mask, 1.0, 0.0)
        return jnp.sum(jnp.sum(ones, axis=1, keepdims=True), axis=2, keepdims=True)

    def search(_, bounds):
        lo, hi = bounds
        mid = lo + lax.shift_right_logical(hi - lo + 1, 1)
        ok = count(xb >= mid) >= cap
        return jnp.where(ok, mid, lo), jnp.where(ok, hi, mid - 1)

    inf_bits = 0x7F800000
    tau, _ = lax.fori_loop(0, 31, search, (jnp.zeros((e, 1, 1), jnp.int32),
                                           jnp.full((e, 1, 1), inf_bits, jnp.int32)))

    ki = lax.broadcasted_iota(jnp.int32, (128, 128), 0)
    li = lax.broadcasted_iota(jnp.int32, (128, 128), 1)
    upper = jnp.where(ki <= li, 1.0, 0.0).astype(BF16)
    ones = jnp.ones((128, 128), BF16)
    ri = lax.broadcasted_iota(jnp.int32, (r, r), 0)
    ci = lax.broadcasted_iota(jnp.int32, (r, r), 1)
    earlier_rows = jnp.where(ci < ri, 1.0, 0.0).astype(BF16)

    def running_count(mask):
        m2 = jnp.where(mask, 1.0, 0.0).reshape(e * r, 128).astype(BF16)
        within = jnp.dot(m2, upper, preferred_element_type=F32)
        row_tot = jnp.dot(m2, ones, preferred_element_type=F32).astype(BF16)
        before = [jnp.dot(earlier_rows, row_tot[i * r:(i + 1) * r], preferred_element_type=F32)
                  for i in range(e)]
        return (within + jnp.concatenate(before, axis=0)).reshape(e, r, 128)

    above = xb > tau
    tied = xb == tau
    need = cap - count(above)
    chosen = jnp.logical_or(above, jnp.logical_and(tied, running_count(tied) <= need))
    pos_ref[...] = jnp.where(chosen, running_count(chosen) - 1.0, -1.0)

    slot = lax.broadcasted_iota(jnp.int32, (cap, 128), 0).astype(F32)
    lane = lax.broadcasted_iota(jnp.int32, (1, 128), 1)
    for ei in range(e):
        def row_group(g, acc, ei=ei):
            rows = pl.ds(pl.multiple_of(g * 8, 8), 8)
            pos8 = pos_ref[ei, rows, :]
            aff8 = aff_ref[0, ei, rows, :]
            for i in range(8):
                onehot = jnp.where(slot == pos8[i:i + 1], 1.0, 0.0).astype(BF16)
                tok = (g * 8 + i) * 128 + lane
                gv = aff8[i:i + 1]
                g_hi = gv.astype(BF16).astype(F32)
                g_mid = (gv - g_hi).astype(BF16).astype(F32)
                g_lo = gv - g_hi - g_mid
                feats = jnp.concatenate(
                    [lax.shift_right_logical(tok, 6).astype(F32), (tok & 63).astype(F32),
                     g_hi, g_mid, g_lo, jnp.zeros((3, 128), F32)], axis=0).astype(BF16)
                acc = acc + lax.dot_general(feats, onehot, _NT, preferred_element_type=F32)
            return acc

        acc = lax.fori_loop(0, r // 8, row_group, jnp.zeros((8, cap), F32))
        idx_ref[0, ei:ei + 1, :] = (acc[0:1] * 64.0 + acc[1:2]).astype(jnp.int32)
        gate_ref[0, ei:ei + 1, :] = acc[2:3] + acc[3:4] + acc[4:5]


def _moe_select(aff, cap):
    b, e, t = aff.shape
    tpad = max(t, 1024)
    if tpad != t:
        aff = jnp.pad(aff, ((0, 0), (0, 0), (0, tpad - t)), constant_values=-1.0)
    r = tpad // 128
    return pl.pallas_call(
        functools.partial(_select_kernel, cap=cap),
        grid=(b,),
        in_specs=[pl.BlockSpec((1, e, r, 128), lambda bi: (bi, 0, 0, 0))],
        out_specs=[pl.BlockSpec((1, e, cap), lambda bi: (bi, 0, 0)),
                   pl.BlockSpec((1, e, cap), lambda bi: (bi, 0, 0))],
        out_shape=[jax.ShapeDtypeStruct((b, e, cap), jnp.int32),
                   jax.ShapeDtypeStruct((b, e, cap), F32)],
        scratch_shapes=[pltpu.VMEM((e, r, 128), F32)],
        compiler_params=_params("arbitrary"),
        name="moe_select",
    )(aff.reshape(b, e, r, 128))


def _dispatch_kernel(idx_ref, h_hbm, o_ref, buf_ref, sem, *, tm, steps_per_batch):
    i = pl.program_id(0)
    nsteps = pl.num_programs(0)

    def row_copy(step, r, slot):
        tok = idx_ref[step * tm + r]
        return pltpu.make_async_copy(h_hbm.at[lax.div(step, steps_per_batch), pl.ds(tok, 1), :],
                                     buf_ref.at[slot, pl.ds(r, 1), :], sem.at[slot])

    def start_rows(step):
        slot = lax.rem(step, 2)

        def body(r8, carry):
            for j in range(8):
                row_copy(step, r8 * 8 + j, slot).start(priority=j % 2)
            return carry

        lax.fori_loop(0, tm // 8, body, 0)

    @pl.when(i == 0)
    def _():
        start_rows(i)

    @pl.when(i + 1 < nsteps)
    def _():
        start_rows(i + 1)

    slot = lax.rem(i, 2)

    def wait_row(r, carry):
        row_copy(i, r, slot).wait()
        return carry

    lax.fori_loop(0, tm, wait_row, 0, unroll=8)
    o_ref[...] = buf_ref[slot].astype(o_ref.dtype)


def _moe_dispatch(h, idx, tm=256):
    b, t, d = h.shape
    _, e, cap = idx.shape
    tm = min(tm, cap)
    nsteps = b * e * cap // tm
    out = pl.pallas_call(
        functools.partial(_dispatch_kernel, tm=tm, steps_per_batch=e * cap // tm),
        grid_spec=pltpu.PrefetchScalarGridSpec(
            num_scalar_prefetch=1,
            grid=(nsteps,),
            in_specs=[pl.BlockSpec(memory_space=pl.ANY)],
            out_specs=pl.BlockSpec((tm, d), lambda i, idx_ref: (i, 0)),
            scratch_shapes=[pltpu.VMEM((2, tm, d), F32), pltpu.SemaphoreType.DMA((2,))],
        ),
        out_shape=jax.ShapeDtypeStruct((b * e * cap, d), BF16),
        compiler_params=_params("arbitrary"),
        name="moe_dispatch",
    )(idx.reshape(-1), h)
    return out.reshape(b, e, cap, d)


def _first_row_tile():
    return jnp.logical_and(pl.program_id(2) == 0, pl.program_id(3) == 0)


def _expert_up_kernel(x_ref, w1_ref, w3_ref, o_ref, w1b_ref, w3b_ref):
    @pl.when(_first_row_tile())
    def _():
        w1b_ref[...] = w1_ref[...].astype(BF16)
        w3b_ref[...] = w3_ref[...].astype(BF16)

    x = x_ref[0, 0]
    a = jnp.dot(x, w1b_ref[...], preferred_element_type=F32)
    g = jnp.dot(x, w3b_ref[...], preferred_element_type=F32)
    o_ref[0, 0] = (a * jax.nn.sigmoid(a) * g).astype(o_ref.dtype)


def _expert_down_kernel(idx_ref, h_ref, w2_ref, gate_ref, acc_hbm, out_hbm, w2b_ref, rows_ref,
                        gather_sem, scatter_sem, *, tm, ot, cap):
    del acc_hbm
    ei, ni, bi, mi = (pl.program_id(a) for a in range(4))
    nn, nb, nm = (pl.num_programs(a) for a in (1, 2, 3))
    q = (ni * nb + bi) * nm + mi
    slot = lax.rem(q, 2)
    base = (bi * pl.num_programs(0) + ei) * cap + mi * tm
    col = pl.multiple_of(ni * ot, ot)

    @pl.when(_first_row_tile())
    def _():
        w2b_ref[...] = w2_ref[...].astype(BF16)

    def gather(r, s):
        return pltpu.make_async_copy(out_hbm.at[bi, pl.ds(idx_ref[base + r], 1), pl.ds(col, ot)],
                                     rows_ref.at[s, pl.ds(r, 1), :], gather_sem.at[s])

    def scatter(r, s):
        return pltpu.make_async_copy(rows_ref.at[s, pl.ds(r, 1), :],
                                     out_hbm.at[bi, pl.ds(idx_ref[base + r], 1), pl.ds(col, ot)],
                                     scatter_sem.at[s])

    def for_rows(fn):
        def body(r8, carry):
            for j in range(8):
                fn(r8 * 8 + j, j % 2)
            return carry
        lax.fori_loop(0, tm // 8, body, 0)

    @pl.when(q >= 2)
    def _():
        for_rows(lambda r, p: scatter(r, slot).wait())

    for_rows(lambda r, p: gather(r, slot).start(priority=p))
    y = jnp.dot(h_ref[0, 0], w2b_ref[...], preferred_element_type=F32) * gate_ref[0, 0]
    for_rows(lambda r, p: gather(r, slot).wait())
    rows_ref[slot] = rows_ref[slot] + y
    for_rows(lambda r, p: scatter(r, slot).start(priority=p))

    @pl.when(q == nn * nb * nm - 1)
    def _():
        for_rows(lambda r, p: scatter(r, 1 - slot).wait())
        for_rows(lambda r, p: scatter(r, slot).wait())


def _expert_ffn(xin, gate, idx, w1, w3, w2, layer, t, tm_up=512, tm=256):
    b, e, cap, d = xin.shape
    ff = w1.shape[-1]
    tm_up, tm = min(tm_up, cap), min(tm, cap)
    ft, ot = min(MOE_FF_TILE, ff), min(MOE_OUT_TILE, d)
    assert ((d // ot) * b * (cap // tm)) % 2 == 0
    hid = pl.pallas_call(
        _expert_up_kernel,
        grid=(e, ff // ft, b, cap // tm_up),
        in_specs=[pl.BlockSpec((1, 1, tm_up, d), lambda ei, fi, bi, mi: (bi, ei, mi, 0)),
                  pl.BlockSpec((None, None, d, ft), lambda ei, fi, bi, mi: (layer, ei, 0, fi)),
                  pl.BlockSpec((None, None, d, ft), lambda ei, fi, bi, mi: (layer, ei, 0, fi))],
        out_specs=pl.BlockSpec((1, 1, tm_up, ft), lambda ei, fi, bi, mi: (bi, ei, mi, fi)),
        out_shape=jax.ShapeDtypeStruct((b, e, cap, ff), BF16),
        scratch_shapes=[pltpu.VMEM((d, ft), BF16), pltpu.VMEM((d, ft), BF16)],
        compiler_params=_params("arbitrary", "arbitrary", "arbitrary", "arbitrary"),
        name="moe_expert_up",
    )(xin, w1, w3)
    return pl.pallas_call(
        functools.partial(_expert_down_kernel, tm=tm, ot=ot, cap=cap),
        grid_spec=pltpu.PrefetchScalarGridSpec(
            num_scalar_prefetch=1,
            grid=(e, d // ot, b, cap // tm),
            in_specs=[pl.BlockSpec((1, 1, tm, ff), lambda ei, ni, bi, mi, idx_ref: (bi, ei, mi, 0)),
                      pl.BlockSpec((None, None, ff, ot), lambda ei, ni, bi, mi, idx_ref: (layer, ei, 0, ni)),
                      pl.BlockSpec((1, 1, tm, 1), lambda ei, ni, bi, mi, idx_ref: (bi, ei, mi, 0)),
                      pl.BlockSpec(memory_space=pl.ANY)],
            out_specs=pl.BlockSpec(memory_space=pl.ANY),
            scratch_shapes=[pltpu.VMEM((ff, ot), BF16), pltpu.VMEM((2, tm, ot), F32),
                            pltpu.SemaphoreType.DMA((2,)), pltpu.SemaphoreType.DMA((2,))],
        ),
        out_shape=jax.ShapeDtypeStruct((b, t, d), F32),
        input_output_aliases={4: 0},
        compiler_params=_params("arbitrary", "arbitrary", "arbitrary", "arbitrary"),
        name="moe_expert_down",
    )(idx.reshape(-1), hid, w2, gate[..., None], jnp.zeros((b, t, d), F32))


def _expert_choice_moe(h, router_t, w1, w3, w2, layer):
    b, t, d = h.shape
    cap = EC_CAPACITY_FACTOR * t // N_EXPERTS
    aff = _router_affinity(h, router_t)
    idx, gate = _moe_select(aff, cap)
    idx = jnp.clip(idx, 0, t - 1)
    xin = _moe_dispatch(h, idx)
    return _expert_ffn(xin, gate, idx, w1, w3, w2, layer, t)


def _project(h, w, out_dtype, name, col0=0, n=None):
    b, t, d = h.shape
    out = _matmul(h.reshape(b * t, d), w, 0, out_dtype, name, col0=col0, n=n)
    return out.reshape(b, t, out.shape[1])


def _mixer_na_s5(h, hc, w_in, w_out, rpb, s5_tables, glu_w, glu_b):
    b, t, _ = h.shape
    tc = hc.shape[1]
    qkv = _project(h, w_in, BF16, "ab_in_qkv", 0, 3 * NA_W)
    qkv_c = _project(hc, w_in, BF16, "ab_in_qkv_ctx", 0, 3 * NA_W)
    u = _project(h, w_in, F32, "ab_in_u", 3 * NA_W, S5_WIDTH)
    u_c = _project(hc, w_in, F32, "ab_in_u_ctx", 3 * NA_W, S5_WIDTH)
    o_na = _neighbourhood_attention(qkv, qkv_c, rpb)
    oc_na = _context_attention(qkv_c)
    y_s5, yc_s5 = _s5_scan(u, u_c, s5_tables)
    token_major = lambda zz, n: zz.reshape(b, S5_CHUNK, n // S5_CHUNK, S5_WIDTH).transpose(
        0, 2, 1, 3).reshape(b * n, S5_WIDTH)
    z = token_major(_s5_glu(y_s5.reshape(b * t, S5_WIDTH), glu_w, glu_b), t)
    z_c = token_major(_s5_glu(yc_s5.reshape(b * tc, S5_WIDTH), glu_w, glu_b), tc)
    y = _matmul([o_na.reshape(b * t, NA_W), z], w_out, 0, F32, "ab_out")
    y_c = _matmul([oc_na.reshape(b * tc, NA_W), z_c], w_out, 0, F32, "ab_out_ctx")
    return y.reshape(b, t, -1), y_c.reshape(b, tc, -1)


def _mixer_hgrn2(h, hc, w_in, w_out, lb, norm_g):
    proj = _project(h, w_in, F32, "hg_in")
    proj_c = _project(hc, w_in, F32, "hg_in_ctx")
    return _project(_hgrn2_gla(proj, proj_c, lb, norm_g), w_out, F32, "hg_out")


def kernel(x, c, ctx, c_ctx, ada_w, ada_b, norm_g, ab_w_in, ab_w_out, na_rpb, s5_lam_re, s5_lam_im,
           s5_log_dt, s5_b_re, s5_b_im, s5_c_re, s5_c_im, s5_d, s5_glu_w, s5_glu_b, hg_w_in, hg_w_out,
           hg_lb_logits, hg_norm_g, moe_router, moe_w1, moe_w3, moe_w2):
    bsz, _, d = x.shape
    depth = ada_w.shape[0]
    assert depth == 2 and bsz + 1 <= 8
    assert ab_w_in.shape[0] == 1 and hg_w_in.shape[0] == 1
    xc = ctx
    cond = jax.nn.silu(c)
    cond_ctx = jax.nn.silu(c_ctx)[None]
    cond3 = jnp.concatenate([cond, cond_ctx, jnp.zeros((8 - bsz - 1, d), F32)], axis=0)
    hg_lb = jnp.cumsum(jax.nn.softmax(hg_lb_logits.astype(F32), axis=0), axis=0)
    moe_w = (moe_w1, moe_w3, moe_w2)
    ada_x, ada_c = [], []
    for layer in range(depth):
        ada = _matmul(cond3, ada_w, layer, F32, "adaln") + ada_b[layer][None]
        ada_x.append(jnp.split(ada[:bsz], 6, axis=-1))
        ada_c.append([jnp.broadcast_to(a, (bsz, d)) for a in jnp.split(ada[bsz:bsz + 1], 6, axis=-1)])
    ng0 = norm_g[0].astype(F32)
    h = _modulate(x, ng0[0], ada_x[0][0], ada_x[0][1])
    hc = _modulate(xc, ng0[0], ada_c[0][0], ada_c[0][1])
    for layer in range(depth):
        need_ctx = layer < depth - 1
        _, _, g1, sh2, sc2, g2 = ada_x[layer]
        _, _, cg1, csh2, csc2, cg2 = ada_c[layer]
        ng = norm_g[layer].astype(F32)
        if layer == 0:
            tables = _s5_tables(s5_lam_re[0], s5_lam_im[0], s5_log_dt[0], s5_b_re[0], s5_b_im[0],
                                s5_c_re[0], s5_c_im[0], s5_d[0])
            y, yc = _mixer_na_s5(h, hc, ab_w_in, ab_w_out, na_rpb[0], tables, s5_glu_w[0],
                                 s5_glu_b[0].astype(F32))
        else:
            y = _mixer_hgrn2(h, hc, hg_w_in, hg_w_out, hg_lb[0], hg_norm_g[0])
            yc = None
        router_t = moe_router[layer].T.astype(BF16)
        x, h2 = _gated_residual_modulate(x, y, ng[1], g1, ng[2], sh2, sc2, F32)
        moe = _expert_choice_moe(h2, router_t, *moe_w, layer)
        if not need_ctx:
            x = _gated_residual(x, moe, ng[3], g2)
            continue
        ng_next = norm_g[layer + 1].astype(F32)
        x, h = _gated_residual_modulate(x, moe, ng[3], g2, ng_next[0], ada_x[layer + 1][0],
                                        ada_x[layer + 1][1], BF16)
        xc, hc2 = _gated_residual_modulate(xc, yc, ng[1], cg1, ng[2], csh2, csc2, F32)
        moe_c = _expert_choice_moe(hc2, router_t, *moe_w, layer)
        xc, hc = _gated_residual_modulate(xc, moe_c, ng[3], cg2, ng_next[0], ada_c[layer + 1][0],
                                          ada_c[layer + 1][1], BF16)
    return x
```
